```python
import jax, jax.numpy as jnp
from jax import lax
import numpy as np

D_MODEL = 1024
BATCH = 8
SEQ = 2048
DEPTH = 2
DEC_BATCH = 128
DEC_SEQ = 8
PAST_LEN = 2048
PAGE_SIZE = 128

D_MIX = D_MODEL
HEAD_DIM = 64
H_GLA = (3 * D_MIX) // (8 * HEAD_DIM)
GLA_DK = HEAD_DIM // 2
GLA_DV = HEAD_DIM
GLA_LR = 16
GLA_GATE_NORM = 16.0
GLA_CHUNK = 64
GLA_W = H_GLA * GLA_DV
POOL_WIDTH = D_MIX // 4
POOL_WINDOWS = (2, 4, 8, 16)
POOL_GROUP = POOL_WIDTH // 4
POOL_HIST = 16 - 1
H_FOX = (3 * D_MIX) // (8 * HEAD_DIM)
FOX_W = H_FOX * HEAD_DIM
Q_BLOCK = 128
N_META = 16
EPS = 1e-6

SPLIT_SIZES = (H_GLA * GLA_DK, H_GLA * GLA_DK, GLA_W, GLA_LR, GLA_W,
               POOL_WIDTH, POOL_WIDTH,
               FOX_W, FOX_W, FOX_W, H_FOX, FOX_W)
D_IN = sum(SPLIT_SIZES)
SPLIT_IDX = tuple(sum(SPLIT_SIZES[:i + 1]) for i in range(len(SPLIT_SIZES) - 1))

kernel_name = "hymba_gla_pool_fox_step"


def rmsnorm(x, g):
    xf = x.astype(jnp.float32)
    y = xf * lax.rsqrt(jnp.mean(xf * xf, axis=-1, keepdims=True) + EPS)
    return (y * g.astype(jnp.float32)).astype(x.dtype)


def mixer_inputs(h, pre, wi, w_up, b_gate, bf):
    B, T, _ = h.shape
    (gq, gk, gv, glr, gg, pv, pg, fq, fk, fv, ff, fg) = jnp.split(rmsnorm(h, pre) @ wi, SPLIT_IDX, axis=-1)
    f32 = jnp.float32
    q = gq.reshape(B, T, H_GLA, GLA_DK).astype(f32) * (GLA_DK ** -0.5)
    k = gk.reshape(B, T, H_GLA, GLA_DK).astype(f32)
    v = gv.reshape(B, T, H_GLA, GLA_DV).astype(f32)
    g = (jax.nn.log_sigmoid((glr @ w_up + b_gate).astype(f32)) / GLA_GATE_NORM).reshape(B, T, H_GLA, GLA_DK)
    fq = fq.reshape(B, T, H_FOX, HEAD_DIM)
    fk = fk.reshape(B, T, H_FOX, HEAD_DIM)
    fv = fv.reshape(B, T, H_FOX, HEAD_DIM)
    logf = jax.nn.log_sigmoid((ff + bf).astype(f32))
    return q, k, v, g, gg, pv, pg, fq, fk, fv, logf, fg


def gla_chunked(q, k, v, g, s0):
    B, T, H, _ = q.shape
    dv = v.shape[-1]
    c = GLA_CHUNK if T % GLA_CHUNK == 0 else T
    n = T // c

    def chunks(a):
        return a.reshape(B, n, c, H, a.shape[-1]).transpose(1, 0, 3, 2, 4)

    causal = jnp.tril(jnp.ones((c, c), dtype=bool))

    def step(s, inp):
        qc, kc, vc, gc = inp
        b = jnp.cumsum(gc, axis=2)
        qe = qc * jnp.exp(b)
        ke = kc * jnp.exp(-b)
        a = jnp.where(causal, jnp.einsum('bhtk,bhsk->bhts', qe, ke), 0.0)
        o = jnp.einsum('bhts,bhsv->bhtv', a, vc) + jnp.einsum('bhtk,bhkv->bhtv', qe, s)
        b_end = b[:, :, -1:, :]
        s_new = (jnp.exp(b_end[:, :, 0, :])[..., None] * s
                 + jnp.einsum('bhsk,bhsv->bhkv', kc * jnp.exp(b_end - b), vc))
        return s_new, o

    s, o = lax.scan(step, s0.astype(jnp.float32), (chunks(q), chunks(k), chunks(v), chunks(g)))
    return o.transpose(1, 0, 3, 2, 4).reshape(B, T, H, dv), s


def pool_mix(v, hist, pos0, pw, ps):
    B, T, C = v.shape
    P = POOL_HIST
    ext = jnp.concatenate([hist.astype(jnp.float32), v.astype(jnp.float32)], axis=1)
    c = jnp.concatenate([jnp.zeros((B, 1, C), jnp.float32), jnp.cumsum(ext, axis=1)], axis=1)
    end = c[:, P + 1:P + 1 + T]
    pos = pos0 + jnp.arange(T)
    means = []
    for gi, w in enumerate(POOL_WINDOWS):
        sl = slice(gi * POOL_GROUP, (gi + 1) * POOL_GROUP)
        cnt = jnp.minimum(pos + 1, w).astype(jnp.float32)[None, :, None]
        means.append((end[..., sl] - c[:, P + 1 - w:P + 1 - w + T, sl]) / cnt)
    d = (jnp.concatenate(means, axis=-1) - v.astype(jnp.float32)).reshape(B, T, 4, POOL_GROUP)
    y = jnp.einsum('btgc,gcd->btgd', d, pw.astype(jnp.float32)).reshape(B, T, C) * ps.astype(jnp.float32)
    return y, ext[:, -P:].astype(v.dtype)


def fox_attend(q, fq, pq, k, v, fk, pk):
    s = jnp.einsum('bqhd,bkhd->bhqk', q, k).astype(jnp.float32) * (HEAD_DIM ** -0.5)
    s = s + jnp.swapaxes(fq, 1, 2)[..., :, None] - jnp.swapaxes(fk, 1, 2)[..., None, :]
    s = jnp.where(pq[:, None] >= pk[None, :], s, -jnp.inf)
    p = jax.nn.softmax(s, axis=-1)
    return jnp.einsum('bhqk,bkhd->bqhd', p.astype(v.dtype), v)


def fox_prompt(q, k, v, F):
    B, L, H, d = q.shape
    pos = jnp.arange(L)
    m = slice(0, N_META)
    o_meta = fox_attend(q[:, m], F[:, m], pos[m], k[:, m], v[:, m], F[:, m], pos[m])
    T = L - N_META
    n_blk = T // Q_BLOCK

    def blocks(a):
        return jnp.swapaxes(a[:, N_META:].reshape((B, n_blk, Q_BLOCK) + a.shape[2:]), 0, 1)

    pq = pos[N_META:].reshape(n_blk, Q_BLOCK)
    o_real = lax.map(lambda blk: fox_attend(blk[0], blk[1], blk[2], k, v, F, pos), (blocks(q), blocks(F), pq))
    o_real = jnp.swapaxes(o_real, 0, 1).reshape(B, T, H, d)
    return jnp.concatenate([o_meta, o_real], axis=1)


def mixer_output(h, o_gla, gg, o_pool, pg, o_fox, fg, gla_g, wo, post):
    B, T, _ = h.shape
    dt = h.dtype
    o_gla = rmsnorm(o_gla, gla_g).reshape(B, T, GLA_W).astype(dt)
    y = jnp.concatenate([o_gla * jax.nn.silu(gg),
                         o_pool.astype(dt) * jax.nn.silu(pg),
                         o_fox.reshape(B, T, FOX_W).astype(dt) * jax.nn.silu(fg)], axis=-1) @ wo
    return h + rmsnorm(y, post)


def prompt_layer(h, pre, wi, w_up, b_gate, gla_g, pw, ps, bf, wo, post):
    B = h.shape[0]
    q, k, v, g, gg, pv, pg, fq, fk, fv, logf, fg = mixer_inputs(h, pre, wi, w_up, b_gate, bf)
    m = slice(0, N_META)
    r = slice(N_META, None)
    s0 = jnp.zeros((B, H_GLA, GLA_DK, GLA_DV), jnp.float32)
    o_m, s_m = gla_chunked(q[:, m], k[:, m], v[:, m], g[:, m], s0)
    o_r, s_gla = gla_chunked(q[:, r], k[:, r], v[:, r], g[:, r], s_m)
    o_gla = jnp.concatenate([o_m, o_r], axis=1)
    o_pool, hist = pool_mix(pv, jnp.zeros((B, POOL_HIST, POOL_WIDTH), pv.dtype), 0, pw, ps)
    o_fox = fox_prompt(fq, fk, fv, jnp.cumsum(logf, axis=1))
    h = mixer_output(h, o_gla, gg, o_pool, pg, o_fox, fg, gla_g, wo, post)
    return h, fk, fv, logf, s_gla, hist


def sample_layer(h, k_pages, v_pages, lf_pages, page_table, s_gla, hist, pre, wi, w_up, b_gate, gla_g, pw, ps, bf, wo, post):
    B, T, _ = h.shape
    q, k, v, g, gg, pv, pg, fq, fk, fv, logf, fg = mixer_inputs(h, pre, wi, w_up, b_gate, bf)
    past = page_table.shape[1] * k_pages.shape[1]
    o_gla, s_new = gla_chunked(q, k, v, g, s_gla)
    o_pool, hist_new = pool_mix(pv, hist, past, pw, ps)
    kp = k_pages[page_table].reshape(B, past, H_FOX, HEAD_DIM)
    vp = v_pages[page_table].reshape(B, past, H_FOX, HEAD_DIM)
    lp = lf_pages[page_table].reshape(B, past, H_FOX).astype(jnp.float32)
    F = jnp.cumsum(jnp.concatenate([lp, logf], axis=1), axis=1)
    keys = jnp.concatenate([kp.astype(fk.dtype), fk], axis=1)
    vals = jnp.concatenate([vp.astype(fv.dtype), fv], axis=1)
    pos = jnp.arange(past + T)
    o_fox = fox_attend(fq, F[:, past:], pos[past:], keys, vals, F, pos)
    h = mixer_output(h, o_gla, gg, o_pool, pg, o_fox, fg, gla_g, wo, post)
    return h, fk, fv, logf, s_new, hist_new


def setup_inputs(seed: int = 0) -> dict:
    key = jax.random.key(seed)
    ks = jax.random.split(key, 24)
    n_pages = PAST_LEN // PAGE_SIZE
    n_used = DEC_BATCH * n_pages
    n_phys = (n_used * 5) // 4
    nrm = jax.random.normal
    page_table = jax.random.permutation(ks[0], n_phys)[:n_used].reshape(DEC_BATCH, n_pages).astype(jnp.int32)
    fox_bias_base = jnp.linspace(1.0, 4.0, H_FOX)[None, :]
    return {
        "x_prompt": nrm(ks[1], (BATCH, SEQ, D_MODEL), jnp.float32),
        "x_sample": nrm(ks[2], (DEC_BATCH, DEC_SEQ, D_MODEL), jnp.float32),
        "cache_fox_k": nrm(ks[3], (DEPTH, n_phys, PAGE_SIZE, H_FOX, HEAD_DIM), jnp.float32),
        "cache_fox_v": nrm(ks[4], (DEPTH, n_phys, PAGE_SIZE, H_FOX, HEAD_DIM), jnp.float32),
        "cache_fox_logf": jax.nn.log_sigmoid(2.0 + 0.5 * nrm(ks[5], (DEPTH, n_phys, PAGE_SIZE, H_FOX), jnp.float32)),
        "state_gla": 0.5 * nrm(ks[6], (DEPTH, DEC_BATCH, H_GLA, GLA_DK, GLA_DV), jnp.float32),
        "state_pool": nrm(ks[7], (DEPTH, DEC_BATCH, POOL_HIST, POOL_WIDTH), jnp.float32),
        "page_table": page_table,
        "meta_tokens": nrm(ks[8], (N_META, D_MODEL), jnp.float32),
        "norm_pre": 1.0 + 0.05 * nrm(ks[9], (DEPTH, D_MODEL), jnp.float32),
        "norm_post": 1.0 + 0.05 * nrm(ks[10], (DEPTH, D_MODEL), jnp.float32),
        "w_in": nrm(ks[11], (DEPTH, D_MODEL, D_IN), jnp.float32) * D_MODEL ** -0.5,
        "gla_w_up": nrm(ks[12], (DEPTH, GLA_LR, H_GLA * GLA_DK), jnp.float32) * GLA_LR ** -0.5,
        "gla_b_gate": 0.1 * nrm(ks[13], (DEPTH, H_GLA * GLA_DK), jnp.float32),
        "gla_norm": 1.0 + 0.05 * nrm(ks[14], (DEPTH, GLA_DV), jnp.float32),
        "pool_w": nrm(ks[15], (DEPTH, 4, POOL_GROUP, POOL_GROUP), jnp.float32) * POOL_GROUP ** -0.5,
        "pool_scale": 1.0 + 0.1 * nrm(ks[16], (DEPTH, POOL_WIDTH), jnp.float32),
        "fox_b_f": fox_bias_base + 0.3 * nrm(ks[17], (DEPTH, H_FOX), jnp.float32),
        "w_out": nrm(ks[18], (DEPTH, D_MIX, D_MODEL), jnp.float32) * D_MIX ** -0.5,
    }


def reference(x_prompt, x_sample, cache_fox_k, cache_fox_v, cache_fox_logf, state_gla, state_pool, page_table,
              meta_tokens, norm_pre, norm_post, w_in, gla_w_up, gla_b_gate, gla_norm, pool_w, pool_scale,
              fox_b_f, w_out):
    B, _, D = x_prompt.shape
    meta = jnp.broadcast_to(meta_tokens.astype(x_prompt.dtype)[None], (B, N_META, D))
    h_p = jnp.concatenate([meta, x_prompt], axis=1)
    h_s = x_sample
    kp_l, vp_l, lfp_l, sgp_l, spp_l = [], [], [], [], []
    ks_l, vs_l, lfs_l, sgs_l, sps_l = [], [], [], [], []
    for l in range(DEPTH):
        w = (norm_pre[l], w_in[l], gla_w_up[l], gla_b_gate[l], gla_norm[l], pool_w[l], pool_scale[l],
             fox_b_f[l], w_out[l], norm_post[l])
        h_p, kp, vp, lfp, sgp, spp = prompt_layer(h_p, *w)
        h_s, ks, vs, lfs, sgs, sps = sample_layer(h_s, cache_fox_k[l], cache_fox_v[l], cache_fox_logf[l],
                                                  page_table, state_gla[l], state_pool[l], *w)
        kp_l.append(kp); vp_l.append(vp); lfp_l.append(lfp); sgp_l.append(sgp); spp_l.append(spp)
        ks_l.append(ks); vs_l.append(vs); lfs_l.append(lfs); sgs_l.append(sgs); sps_l.append(sps)
    return (h_p[:, N_META:], h_s,
            jnp.stack(kp_l), jnp.stack(vp_l), jnp.stack(lfp_l), jnp.stack(sgp_l), jnp.stack(spp_l),
            jnp.stack(ks_l), jnp.stack(vs_l), jnp.stack(lfs_l), jnp.stack(sgs_l), jnp.stack(sps_l))
```

```python
import functools

import jax
import jax.numpy as jnp
from jax import lax
from jax.experimental import pallas as pl
from jax.experimental.pallas import tpu as pltpu

F32 = jnp.float32
BF16 = jnp.bfloat16

D_MODEL = 1024
N_HEADS = 6
GLA_DK = 32
HEAD_DIM = 64
GLA_LR = 16
GLA_GATE_NORM = 16.0
GLA_CHUNK = 64
QK_W = N_HEADS * GLA_DK
HEADS_W = N_HEADS * HEAD_DIM
POOL_W = 256
POOL_GROUP = 64
POOL_WINDOWS = (2, 4, 8, 16)
POOL_HIST = 15
HIST_ROWS = 16
N_META = 16
PAGE = 128
EPS = 1e-6
LANES = 128
NEG = -1e30

COL_Q, COL_K, COL_GV, COL_GATES, COL_PV = 0, 256, 512, 896, 1920
COL_FQ, COL_FK, COL_FV, COL_AUX, W_IN_COLS = 2176, 2560, 2944, 3328, 3456
GATES_W = HEADS_W + POOL_W + HEADS_W
AUX_FF, AUX_LR = 0, 8
SPLIT_SIZES = (QK_W, QK_W, HEADS_W, GLA_LR, HEADS_W, POOL_W, POOL_W, HEADS_W, HEADS_W, HEADS_W, N_HEADS, HEADS_W)

AUXK_ONES = (18, 36)


def _cparams(sem, vmem_mb):
    return pltpu.CompilerParams(dimension_semantics=sem, vmem_limit_bytes=vmem_mb * 1024 * 1024)


def _const_spec(shape):
    return pl.BlockSpec(shape, lambda *_: (0,) * len(shape))


def _dot(a, b):
    return jnp.dot(a, b, preferred_element_type=F32)


def _dot_nt(a, b):
    return lax.dot_general(a, b, (((1,), (1,)), ((), ())), preferred_element_type=F32)


def _dot_tn(a, b):
    return lax.dot_general(a, b, (((0,), (0,)), ((), ())), preferred_element_type=F32)


def _log_sigmoid(x):
    return jnp.minimum(x, 0.0) - jnp.log1p(jnp.exp(-jnp.abs(x)))


def _silu(x):
    return x * (1.0 / (1.0 + jnp.exp(-x)))


def _split3(x):
    hi = x.astype(BF16)
    r = x - hi.astype(F32)
    mid = r.astype(BF16)
    lo = (r - mid.astype(F32)).astype(BF16)
    return hi, mid, lo


def _cumsum_rows(x):
    n = x.shape[0]
    row = lax.broadcasted_iota(jnp.int32, x.shape, 0)
    k = 1
    while k < n:
        x = x + jnp.where(row >= k, pltpu.roll(x, k, axis=0), 0.0)
        k *= 2
    return x


def _inproj_kernel(*refs, tm, n_pre):
    if n_pre:
        (x_ref, pre_ref, w_ref, wup_ref, bg_ref, bf_ref, kpre_ref, vpre_ref, apre_ref,
         q_ref, k_ref, g_ref, v_ref, gates_ref, pv_ref, fq_ref, fk_ref, fv_ref, aux_ref) = refs
    else:
        (x_ref, pre_ref, w_ref, wup_ref, bg_ref, bf_ref,
         q_ref, k_ref, g_ref, v_ref, gates_ref, pv_ref, fq_ref, fk_ref, fv_ref, aux_ref) = refs
    t = pl.program_id(1)
    x = x_ref[0]
    xn = x * lax.rsqrt(jnp.mean(x * x, axis=-1, keepdims=True) + EPS) * pre_ref[...]
    xb = xn.astype(BF16)

    def seg(c0, width):
        return _dot(xb, w_ref[:, c0:c0 + width])

    q_ref[0] = seg(COL_Q, QK_W) * (GLA_DK ** -0.5)
    k_ref[0] = seg(COL_K, QK_W)
    v_ref[0] = seg(COL_GV, HEADS_W).astype(BF16)
    gates_ref[0] = seg(COL_GATES, GATES_W)
    pv_ref[0] = seg(COL_PV, POOL_W)
    fq_ref[0] = (seg(COL_FQ, HEADS_W) * (HEAD_DIM ** -0.5)).astype(BF16)
    aux = seg(COL_AUX, LANES)
    g_ref[0] = _log_sigmoid(_dot(aux.astype(BF16), wup_ref[...]) + bg_ref[...]) * (1.0 / GLA_GATE_NORM)
    rows = pl.ds(pl.multiple_of(n_pre + t * tm, 8), tm)
    fk_ref[0, rows, :] = seg(COL_FK, HEADS_W)
    fv_ref[0, rows, :] = seg(COL_FV, HEADS_W)
    aux_ref[0, rows, :] = _log_sigmoid(aux + bf_ref[...])
    if n_pre:
        @pl.when(t == 0)
        def _():
            fk_ref[0, 0:n_pre, :] = kpre_ref[...]
            fv_ref[0, 0:n_pre, :] = vpre_ref[...]
            aux_ref[0, 0:n_pre, :] = apre_ref[...]


def _inproj(h, lw, tm, prefix=None):
    B, T, _ = h.shape
    n_pre = 0 if prefix is None else prefix["k"].shape[0]
    L = n_pre + T
    tile = lambda w: pl.BlockSpec((1, tm, w), lambda b, t: (b, t, 0))
    whole = lambda w: pl.BlockSpec((1, L, w), lambda b, t: (b, 0, 0))
    in_specs = [tile(D_MODEL), _const_spec((1, D_MODEL)), _const_spec((D_MODEL, W_IN_COLS)),
                _const_spec((LANES, QK_W)), _const_spec((1, QK_W)), _const_spec((1, LANES))]
    args = [h, lw["pre"], lw["w_in"], lw["w_up"], lw["b_gate"], lw["bf"]]
    if n_pre:
        in_specs += [_const_spec((n_pre, HEADS_W)), _const_spec((n_pre, HEADS_W)), _const_spec((n_pre, LANES))]
        args += [prefix["k"], prefix["v"], prefix["aux"]]
    out_shape = [jax.ShapeDtypeStruct((B, T, QK_W), F32), jax.ShapeDtypeStruct((B, T, QK_W), F32),
                 jax.ShapeDtypeStruct((B, T, QK_W), F32), jax.ShapeDtypeStruct((B, T, HEADS_W), BF16),
                 jax.ShapeDtypeStruct((B, T, GATES_W), F32), jax.ShapeDtypeStruct((B, T, POOL_W), F32),
                 jax.ShapeDtypeStruct((B, T, HEADS_W), BF16), jax.ShapeDtypeStruct((B, L, HEADS_W), F32),
                 jax.ShapeDtypeStruct((B, L, HEADS_W), F32), jax.ShapeDtypeStruct((B, L, LANES), F32)]
    out_specs = [tile(QK_W), tile(QK_W), tile(QK_W), tile(HEADS_W), tile(GATES_W), tile(POOL_W),
                 tile(HEADS_W), whole(HEADS_W), whole(HEADS_W), whole(LANES)]
    names = ("q", "k", "g", "v", "gates", "pv", "fq", "fk", "fv", "aux")
    outs = pl.pallas_call(
        functools.partial(_inproj_kernel, tm=tm, n_pre=n_pre),
        grid=(B, T // tm), in_specs=in_specs, out_specs=out_specs, out_shape=out_shape,
        compiler_params=_cparams(("parallel", "arbitrary"), 56), name="inproj")(*args)
    return dict(zip(names, outs))


def _gla_masks(chunk):
    lane_qk = lax.broadcasted_iota(jnp.int32, (1, QK_W), 1) // GLA_DK
    lane_v = lax.broadcasted_iota(jnp.int32, (1, HEADS_W), 1) // HEAD_DIM
    r = lax.broadcasted_iota(jnp.int32, (N_HEADS * chunk, chunk), 0)
    c = lax.broadcasted_iota(jnp.int32, (N_HEADS * chunk, chunk), 1)
    tril = (r % chunk) >= c
    return lane_qk, lane_v, tril


def _gla_intra(q, k, g, v_bf, masks, chunk):
    lane_qk, lane_v, tril = masks
    bcum = _cumsum_rows(g)
    qe_bf = (q * jnp.exp(bcum)).astype(BF16)
    ke_bf = (k * jnp.exp(-bcum)).astype(BF16)
    b_end = bcum[chunk - 1:chunk, :]
    kd_bf = (k * jnp.exp(b_end - bcum)).astype(BF16)
    zero = jnp.zeros_like(qe_bf)
    qe_stack = jnp.concatenate([jnp.where(lane_qk == h, qe_bf, zero) for h in range(N_HEADS)], axis=0)
    a = jnp.where(tril, _dot_nt(qe_stack, ke_bf), 0.0)
    o_full = _dot(a.astype(BF16), v_bf)
    o_intra = jnp.zeros((chunk, HEADS_W), F32)
    for h in range(N_HEADS):
        o_intra = o_intra + jnp.where(lane_v == h, o_full[h * chunk:(h + 1) * chunk, :], 0.0)
    return o_intra, qe_bf, kd_bf, jnp.exp(b_end)


def _head_norm(o, gmat_bf, gn):
    o2 = o * o
    hi = o2.astype(BF16)
    lo = (o2 - hi.astype(F32)).astype(BF16)
    ms = _dot(hi, gmat_bf) + _dot(lo, gmat_bf)
    return o * lax.rsqrt(ms + EPS) * gn


def _pool_delta(ext_ref, pv, pos0, tm):
    ext_ref[HIST_ROWS:HIST_ROWS + tm, :] = pv
    lane_g = lax.broadcasted_iota(jnp.int32, (1, POOL_W), 1) // POOL_GROUP
    pos = pos0 + lax.broadcasted_iota(jnp.int32, (tm, 1), 0)
    run = pv
    sums = jnp.zeros((tm, POOL_W), F32)
    cnt = jnp.zeros((tm, POOL_W), F32)
    for k in range(1, POOL_WINDOWS[-1]):
        run = run + ext_ref[HIST_ROWS - k:HIST_ROWS - k + tm, :]
        if k + 1 in POOL_WINDOWS:
            gi = POOL_WINDOWS.index(k + 1)
            sums = jnp.where(lane_g == gi, run, sums)
            cnt = jnp.where(lane_g == gi, jnp.minimum(pos + 1, k + 1).astype(F32), cnt)
    return sums / cnt - pv


def _glapool_kernel(q_ref, k_ref, g_ref, v_ref, pv_ref, s0_ref, hist0_ref, gn_ref, gmat_ref, pw_ref, ps_ref,
                    og_ref, op_ref, sout_ref, st_ref, ext_ref, *, tm, chunk, pos0):
    t = pl.program_id(1)

    @pl.when(t == 0)
    def _():
        st_ref[...] = s0_ref[...]
        ext_ref[0:HIST_ROWS, :] = hist0_ref[...]

    masks = _gla_masks(chunk)
    rr = lax.broadcasted_iota(jnp.int32, (HEADS_W, QK_W), 0) // HEAD_DIM
    cc = lax.broadcasted_iota(jnp.int32, (HEADS_W, QK_W), 1) // GLA_DK
    bd = rr == cc
    st = st_ref[...]
    outs = []
    for c in range(tm // chunk):
        r = slice(c * chunk, (c + 1) * chunk)
        v_bf = v_ref[0, r, :]
        o_intra, qe_bf, kd_bf, decay = _gla_intra(q_ref[0, r, :], k_ref[0, r, :], g_ref[0, r, :], v_bf, masks, chunk)
        outs.append(o_intra + _dot_nt(qe_bf, st.astype(BF16)))
        st = st * decay + jnp.where(bd, _dot_tn(v_bf, kd_bf), 0.0)
    st_ref[...] = st
    o = outs[0] if len(outs) == 1 else jnp.concatenate(outs, axis=0)
    og_ref[0] = _head_norm(o, gmat_ref[...], gn_ref[...])
    d = _pool_delta(ext_ref, pv_ref[0], pos0 + t * tm, tm)
    op_ref[0] = _dot(d.astype(BF16), pw_ref[...]) * ps_ref[...]
    ext_ref[0:HIST_ROWS, :] = ext_ref[tm:tm + HIST_ROWS, :]

    @pl.when(t == pl.num_programs(1) - 1)
    def _():
        sout_ref[0] = st


def _glapool(x, lw, consts, tm, chunk, pos0, s0, hist0):
    B, T, _ = x["q"].shape
    tile = lambda w: pl.BlockSpec((1, tm, w), lambda b, t: (b, t, 0))
    in_specs = [tile(QK_W), tile(QK_W), tile(QK_W), tile(HEADS_W), tile(POOL_W),
                _const_spec((HEADS_W, QK_W)), _const_spec((HIST_ROWS, POOL_W)), _const_spec((1, HEADS_W)),
                _const_spec((HEADS_W, HEADS_W)), _const_spec((POOL_W, POOL_W)), _const_spec((1, POOL_W))]
    out_shape = [jax.ShapeDtypeStruct((B, T, HEADS_W), F32), jax.ShapeDtypeStruct((B, T, POOL_W), F32),
                 jax.ShapeDtypeStruct((B, HEADS_W, QK_W), F32)]
    out_specs = [tile(HEADS_W), tile(POOL_W), pl.BlockSpec((1, HEADS_W, QK_W), lambda b, t: (b, 0, 0))]
    return pl.pallas_call(
        functools.partial(_glapool_kernel, tm=tm, chunk=chunk, pos0=pos0),
        grid=(B, T // tm), in_specs=in_specs, out_specs=out_specs, out_shape=out_shape,
        scratch_shapes=[pltpu.VMEM((HEADS_W, QK_W), F32), pltpu.VMEM((HIST_ROWS + tm, POOL_W), F32)],
        compiler_params=_cparams(("parallel", "arbitrary"), 48), name="glapool")(
            x["q"], x["k"], x["g"], x["v"], x["pv"], s0, hist0, lw["gla_norm"], consts["gmat"], lw["pool_w"], lw["pool_scale"])


def _fox_features(lf, carry, tri_bf, place_ref):
    r = lf.shape[0]
    lane = lax.broadcasted_iota(jnp.int32, (1, LANES), 1)
    tri = tri_bf[0:r, 0:r]
    hi, mid, lo = _split3(lf)
    f = _dot(tri, hi) + _dot(tri, mid) + _dot(tri, lo) + carry
    fh, fm, fl = _split3(f)
    ones_k = jnp.where((lane >= AUXK_ONES[0]) & (lane < AUXK_ONES[1]), 1.0, 0.0)
    ones_q = jnp.where(lane < AUXK_ONES[0], 1.0, 0.0)
    kaux = ones_k - (_dot(fh, place_ref[0]) + _dot(fm, place_ref[1]) + _dot(fl, place_ref[2]))
    qaux = ones_q + (_dot(fh, place_ref[3]) + _dot(fm, place_ref[4]) + _dot(fl, place_ref[5]))
    return kaux.astype(BF16), qaux.astype(BF16), f[r - 1:r, :]


def _softmax_step(s, v2, m, l, acc):
    m_new = jnp.maximum(m, jnp.max(s, axis=-1, keepdims=True))
    alpha = jnp.exp(m - m_new)
    p = jnp.exp(s - m_new)
    l = alpha * l + jnp.sum(p, axis=-1, keepdims=True)
    acc = alpha * acc + _dot(p.astype(BF16), v2)
    return m_new, l, acc


def _fox_kernel(fq_ref, k_ref, v_ref, aux_ref, tri_ref, place_ref, o_ref,
                kb_ref, vb_ref, kaux_ref, qaux_ref, *, T, n_pre, tq):
    t = pl.program_id(1)

    @pl.when(t == 0)
    def _():
        kb_ref[...] = k_ref[0].astype(BF16)
        vb_ref[...] = v_ref[0].astype(BF16)
        cb = min(PAGE, T)
        blocks = ([(0, n_pre)] if n_pre else []) + [(n_pre + i * cb, cb) for i in range(T // cb)]
        carry = jnp.zeros((1, LANES), F32)
        for r0, r in blocks:
            ka, qa, carry = _fox_features(aux_ref[0, r0:r0 + r, :], carry, tri_ref[...], place_ref)
            kaux_ref[r0:r0 + r, :] = ka
            qaux_ref[r0:r0 + r, :] = qa

    lane = lax.broadcasted_iota(jnp.int32, (1, LANES), 1)
    q0 = pl.multiple_of(n_pre + t * tq, 16)
    qa = qaux_ref[pl.ds(q0, tq), :]
    fq = fq_ref[0]
    rowi = lax.broadcasted_iota(jnp.int32, (tq, tq), 0)
    coli = lax.broadcasted_iota(jnp.int32, (tq, tq), 1)
    zero_bf = jnp.zeros((tq, LANES), BF16)
    for p in range(N_HEADS // 2):
        cols = slice(p * LANES, (p + 1) * LANES)
        fq_p = fq[:, cols]
        halves = []
        for e in range(2):
            h = 2 * p + e
            qmask = (lane == h) | (lane == 6 + h) | (lane == 12 + h) | (lane == 18 + h) | (lane == 24 + h) | (lane == 30 + h)
            q2 = jnp.concatenate([jnp.where((lane // HEAD_DIM) == e, fq_p, zero_bf), jnp.where(qmask, qa, zero_bf)], axis=1)

            def kv(r0, n):
                k2 = jnp.concatenate([kb_ref[pl.ds(r0, n), cols], kaux_ref[pl.ds(r0, n), :]], axis=1)
                return k2, vb_ref[pl.ds(r0, n), cols]

            m = jnp.full((tq, 1), NEG, F32)
            l = jnp.zeros((tq, 1), F32)
            acc = jnp.zeros((tq, LANES), F32)
            if n_pre:
                k2, v2 = kv(0, n_pre)
                m, l, acc = _softmax_step(_dot_nt(q2, k2), v2, m, l, acc)

            def body(j, carry):
                k2, v2 = kv(pl.multiple_of(n_pre + j * tq, 16), tq)
                return _softmax_step(_dot_nt(q2, k2), v2, *carry)

            m, l, acc = lax.fori_loop(0, t, body, (m, l, acc))
            k2, v2 = kv(q0, tq)
            s = jnp.where(rowi >= coli, _dot_nt(q2, k2), NEG)
            m, l, acc = _softmax_step(s, v2, m, l, acc)
            halves.append(acc / l)
        o_ref[0, :, cols] = jnp.where((lane // HEAD_DIM) == 0, halves[0], halves[1])


def _fox(x, consts, tq, n_pre):
    B, T, _ = x["fq"].shape
    L = n_pre + T
    whole = lambda w: pl.BlockSpec((1, L, w), lambda b, t: (b, 0, 0))
    in_specs = [pl.BlockSpec((1, tq, HEADS_W), lambda b, t: (b, t, 0)), whole(HEADS_W), whole(HEADS_W), whole(LANES),
                _const_spec((PAGE, PAGE)), _const_spec((6, LANES, LANES))]
    return pl.pallas_call(
        functools.partial(_fox_kernel, T=T, n_pre=n_pre, tq=tq),
        grid=(B, T // tq), in_specs=in_specs,
        out_specs=pl.BlockSpec((1, tq, HEADS_W), lambda b, t: (b, t, 0)),
        out_shape=jax.ShapeDtypeStruct((B, T, HEADS_W), F32),
        scratch_shapes=[pltpu.VMEM((L, HEADS_W), BF16), pltpu.VMEM((L, HEADS_W), BF16),
                        pltpu.VMEM((L, LANES), BF16), pltpu.VMEM((L, LANES), BF16)],
        compiler_params=_cparams(("parallel", "arbitrary"), 48), name="fox")(
            x["fq"], x["fk"], x["fv"], x["aux"], consts["tri"], consts["place"])


def _outproj_kernel(h_ref, og_ref, op_ref, of_ref, gates_ref, wo_ref, post_ref, out_ref):
    y_in = jnp.concatenate([og_ref[0], op_ref[0], of_ref[0]], axis=1) * _silu(gates_ref[0])
    y = _dot(y_in.astype(BF16), wo_ref[...])
    out_ref[0] = h_ref[0] + y * lax.rsqrt(jnp.mean(y * y, axis=-1, keepdims=True) + EPS) * post_ref[...]


def _outproj(h, og, op, of, gates, lw, tm):
    B, T, _ = h.shape
    tile = lambda w: pl.BlockSpec((1, tm, w), lambda b, t: (b, t, 0))
    return pl.pallas_call(
        _outproj_kernel, grid=(B, T // tm),
        in_specs=[tile(D_MODEL), tile(HEADS_W), tile(POOL_W), tile(HEADS_W), tile(GATES_W),
                  _const_spec((GATES_W, D_MODEL)), _const_spec((1, D_MODEL))],
        out_specs=tile(D_MODEL), out_shape=jax.ShapeDtypeStruct((B, T, D_MODEL), F32),
        compiler_params=_cparams(("parallel", "parallel"), 48), name="outproj")(
            h, og, op, of, gates, lw["w_out"], lw["post"])


def _sample_mix_kernel(q_ref, k_ref, g_ref, v_ref, pv_ref, s_ref, hist_ref, gn_ref, gmat_ref, pw_ref, ps_ref,
                       og_ref, op_ref, sout_ref, ext_ref, *, nseq, ts, pos0):
    masks = _gla_masks(ts)
    rr = lax.broadcasted_iota(jnp.int32, (QK_W, HEADS_W), 0) // GLA_DK
    cc = lax.broadcasted_iota(jnp.int32, (QK_W, HEADS_W), 1) // HEAD_DIM
    bd = rr == cc
    ones = jnp.ones((3 * ts, LANES), BF16)
    outs, deltas = [], []
    for i in range(nseq):
        r = slice(i * ts, (i + 1) * ts)
        g = g_ref[r, :]
        v_bf = v_ref[r, :].astype(BF16)
        o_intra, qe_bf, kd_bf, _ = _gla_intra(q_ref[r, :], k_ref[r, :], g, v_bf, masks, ts)
        s2 = s_ref[i]
        s_bd = jnp.where(bd, jnp.concatenate([s2, s2, s2], axis=1), 0.0).astype(BF16)
        outs.append(o_intra + _dot(qe_bf, s_bd))
        u = jnp.where(bd, _dot_tn(kd_bf, v_bf), 0.0)
        x = jnp.concatenate([u[64 * j:64 * (j + 1), LANES * j:LANES * (j + 1)] for j in range(N_HEADS // 2)], axis=0)
        x = x + pltpu.roll(x, HEAD_DIM, axis=1)
        gh, gm, gl = _split3(g)
        gparts = jnp.concatenate([gh.astype(F32), gm.astype(F32), gl.astype(F32)], axis=0).astype(BF16)
        sout_ref[i] = s2 * jnp.exp(_dot_tn(gparts, ones)) + x
        ext_ref[0:HIST_ROWS, :] = hist_ref[i]
        deltas.append(_pool_delta(ext_ref, pv_ref[r, :], pos0, ts))
    og_ref[...] = _head_norm(jnp.concatenate(outs, axis=0), gmat_ref[...], gn_ref[...])
    op_ref[...] = _dot(jnp.concatenate(deltas, axis=0).astype(BF16), pw_ref[...]) * ps_ref[...]


def _sample_mix(x, state2, hist16, lw, consts, nseq, ts, pos0):
    n = x["q"].shape[0]
    rows = nseq * ts
    tile = lambda w: pl.BlockSpec((rows, w), lambda i: (i, 0))
    in_specs = [tile(QK_W), tile(QK_W), tile(QK_W), tile(HEADS_W), tile(POOL_W),
                pl.BlockSpec((nseq, QK_W, LANES), lambda i: (i, 0, 0)), pl.BlockSpec((nseq, HIST_ROWS, POOL_W), lambda i: (i, 0, 0)),
                _const_spec((1, HEADS_W)), _const_spec((HEADS_W, HEADS_W)), _const_spec((POOL_W, POOL_W)), _const_spec((1, POOL_W))]
    out_shape = [jax.ShapeDtypeStruct((n, HEADS_W), F32), jax.ShapeDtypeStruct((n, POOL_W), F32),
                 jax.ShapeDtypeStruct(state2.shape, F32)]
    out_specs = [tile(HEADS_W), tile(POOL_W), pl.BlockSpec((nseq, QK_W, LANES), lambda i: (i, 0, 0))]
    return pl.pallas_call(
        functools.partial(_sample_mix_kernel, nseq=nseq, ts=ts, pos0=pos0),
        grid=(n // rows,), in_specs=in_specs, out_specs=out_specs, out_shape=out_shape,
        scratch_shapes=[pltpu.VMEM((HIST_ROWS + ts, POOL_W), F32)],
        compiler_params=_cparams(("parallel",), 48), name="sample_mix")(
            x["q"], x["k"], x["g"], x["v"], x["pv"], state2, hist16, lw["gla_norm"], consts["gmat"], lw["pool_w"], lw["pool_scale"])


def _lfprep_kernel(lp_ref, t_ref, o_ref):
    hi, mid, lo = _split3(lp_ref[...])
    w = PAGE * N_HEADS
    for j in range(2):
        tj = t_ref[j]
        o_ref[:, j * w:(j + 1) * w] = _dot(hi, tj) + _dot(mid, tj) + _dot(lo, tj)


def _lfprep(lp2, tmat, tp):
    n, w = lp2.shape
    return pl.pallas_call(
        _lfprep_kernel, grid=(n // tp,),
        in_specs=[pl.BlockSpec((tp, w), lambda i: (i, 0)), _const_spec((2, w, w))],
        out_specs=pl.BlockSpec((tp, 2 * w), lambda i: (i, 0)), out_shape=jax.ShapeDtypeStruct((n, 2 * w), F32),
        compiler_params=_cparams(("parallel",), 32), name="lfprep")(lp2, tmat)


def _sample_fox_kernel(pt_ref, fq_ref, kn_ref, vn_ref, an_ref, *rest, n_pages, ts):
    rt_refs, k_refs, v_refs = rest[:n_pages], rest[n_pages:2 * n_pages], rest[2 * n_pages:3 * n_pages]
    o_ref = rest[3 * n_pages]
    w = PAGE * N_HEADS
    rt = jnp.concatenate([r[0] for r in rt_refs], axis=0)
    incl = _cumsum_rows(rt[:, w:])
    rfull = rt[:, :w] + (incl[n_pages - 1:n_pages, :] - incl)
    cn = _cumsum_rows(an_ref[0])
    cparts = _split3(cn)
    lane = lax.broadcasted_iota(jnp.int32, (ts, LANES), 1)
    ti = lax.broadcasted_iota(jnp.int32, (ts, ts), 0)
    ui = lax.broadcasted_iota(jnp.int32, (ts, ts), 1)
    fq, kn, vn = fq_ref[0], kn_ref[0], vn_ref[0]
    for h in range(N_HEADS):
        hs = slice(h * HEAD_DIM, (h + 1) * HEAD_DIM)
        q_h = fq[:, hs].astype(BF16)
        cq = cn[:, h:h + 1]
        sel = jnp.where(lane == h, 1.0, 0.0).astype(BF16)
        cn_t = _dot_nt(sel, cparts[0]) + _dot_nt(sel, cparts[1]) + _dot_nt(sel, cparts[2])
        s_new = jnp.where(ti >= ui, _dot_nt(q_h, kn[:, hs].astype(BF16)) + cq - cn_t, NEG)
        s_pages = []
        for p in range(n_pages):
            k_h = k_refs[p][0, 0, :, h, :].astype(BF16)
            s_pages.append(_dot_nt(q_h, k_h) + rfull[p:p + 1, h * PAGE:(h + 1) * PAGE] + cq)
        mm = s_pages[0]
        for s in s_pages[1:]:
            mm = jnp.maximum(mm, s)
        m = jnp.maximum(jnp.max(mm, axis=-1, keepdims=True), jnp.max(s_new, axis=-1, keepdims=True))
        p_new = jnp.exp(s_new - m)
        l = jnp.sum(p_new, axis=-1, keepdims=True)
        acc = _dot(p_new.astype(BF16), vn[:, hs].astype(BF16))
        for p in range(n_pages):
            pe = jnp.exp(s_pages[p] - m)
            l = l + jnp.sum(pe, axis=-1, keepdims=True)
            acc = acc + _dot(pe.astype(BF16), v_refs[p][0, 0, :, h, :].astype(BF16))
        o_ref[0, :, hs] = acc / l


def _sample_fox(layer, page_table, fq, kn, vn, an, rt3, cache_k, cache_v):
    nb, ts, _ = fq.shape
    n_pages = page_table.shape[1]
    n_phys = cache_k.shape[1]
    seq = lambda w: pl.BlockSpec((1, ts, w), lambda b, pt: (b, 0, 0))
    rt_specs = [pl.BlockSpec((1, 1, rt3.shape[2]), lambda b, pt, p=p: (layer * n_phys + pt[b, p], 0, 0)) for p in range(n_pages)]
    page_specs = [pl.BlockSpec((1, 1, PAGE, N_HEADS, HEAD_DIM), lambda b, pt, p=p: (layer, pt[b, p], 0, 0, 0)) for p in range(n_pages)]
    grid_spec = pltpu.PrefetchScalarGridSpec(
        num_scalar_prefetch=1, grid=(nb,),
        in_specs=[seq(HEADS_W), seq(HEADS_W), seq(HEADS_W), seq(LANES)] + rt_specs + page_specs + page_specs,
        out_specs=pl.BlockSpec((1, ts, HEADS_W), lambda b, pt: (b, 0, 0)))
    return pl.pallas_call(
        functools.partial(_sample_fox_kernel, n_pages=n_pages, ts=ts), grid_spec=grid_spec,
        out_shape=jax.ShapeDtypeStruct((nb, ts, HEADS_W), F32),
        compiler_params=_cparams(("arbitrary",), 56), name="sample_fox")(
            page_table, fq, kn, vn, an, *([rt3] * n_pages), *([cache_k] * n_pages), *([cache_v] * n_pages))


def _layer_weights(l, norm_pre, norm_post, w_in, gla_w_up, gla_b_gate, gla_norm, pool_w, pool_scale, fox_b_f, w_out):
    idx = [sum(SPLIT_SIZES[:i + 1]) for i in range(len(SPLIT_SIZES) - 1)]
    gq, gk, gv, glr, gg, pv, pg, fq, fk, fv, ff, fg = jnp.split(w_in[l], idx, axis=1)
    zc = lambda n: jnp.zeros((D_MODEL, n), F32)
    aux = jnp.concatenate([ff, zc(AUX_LR - N_HEADS), glr, zc(LANES - AUX_LR - GLA_LR)], axis=1)
    w = jnp.concatenate([gq, zc(COL_K - QK_W), gk, zc(COL_GV - COL_K - QK_W), gv, gg, pg, fg, pv, fq, fk, fv, aux], axis=1)
    w_up = jnp.zeros((LANES, QK_W), F32).at[AUX_LR:AUX_LR + GLA_LR].set(gla_w_up[l])
    bf = jnp.zeros((1, LANES), F32).at[0, AUX_FF:AUX_FF + N_HEADS].set(fox_b_f[l])
    pw = jnp.zeros((POOL_W, POOL_W), F32)
    for gi in range(len(POOL_WINDOWS)):
        sl = slice(gi * POOL_GROUP, (gi + 1) * POOL_GROUP)
        pw = pw.at[sl, sl].set(pool_w[l, gi])
    return {"pre": norm_pre[l][None], "post": norm_post[l][None], "w_in": w.astype(BF16), "w_up": w_up.astype(BF16),
            "b_gate": gla_b_gate[l][None], "bf": bf, "gla_norm": jnp.tile(gla_norm[l], N_HEADS)[None],
            "pool_w": pw.astype(BF16), "pool_scale": pool_scale[l][None], "w_out": w_out[l].astype(BF16)}


def _constants():
    i = jnp.arange(HEADS_W)
    gmat = jnp.where((i[:, None] // HEAD_DIM) == (i[None, :] // HEAD_DIM), 1.0 / HEAD_DIM, 0.0).astype(BF16)
    r = jnp.arange(PAGE)
    tri = (r[:, None] >= r[None, :]).astype(BF16)
    a = jnp.arange(LANES)
    place = jnp.stack([((a[:, None] < N_HEADS) & (a[None, :] == a[:, None] + off)).astype(BF16)
                       for off in (0, 6, 12, 18, 24, 30)])
    src = jnp.arange(PAGE * N_HEADS)
    dst = jnp.arange(PAGE * N_HEADS)
    same_head = (src[:, None] % N_HEADS) == (dst[None, :] // PAGE)
    later = (src[:, None] // N_HEADS) > (dst[None, :] % PAGE)
    tmat = jnp.stack([same_head & later, same_head]).astype(BF16)
    return {"gmat": gmat, "tri": tri, "place": place, "tmat": tmat}


def _state_from_bd(st):
    B = st.shape[0]
    s = st.reshape(B, N_HEADS, HEAD_DIM, N_HEADS, GLA_DK)
    s = jnp.stack([s[:, h, :, h, :] for h in range(N_HEADS)], axis=1)
    return jnp.swapaxes(s, 2, 3)


def _sequence_layer(h, lw, consts, prefix, tm, tq, chunk):
    n_pre = 0 if prefix is None else prefix["k"].shape[0]
    x = _inproj(h, lw, tm, prefix)
    s0 = jnp.zeros((HEADS_W, QK_W), F32) if prefix is None else prefix["state"]
    hist0 = jnp.zeros((HIST_ROWS, POOL_W), F32) if prefix is None else prefix["hist"]
    og, op, st = _glapool(x, lw, consts, tm, chunk, n_pre, s0, hist0)
    of = _fox(x, consts, tq, n_pre)
    h_new = _outproj(h, og, op, of, x["gates"], lw, tm)
    return h_new, x, st


def _sample_layer(l, h, lw, consts, page_table, rt3, cache_k, cache_v, state2, hist16, nb, ts, tm, nseq):
    past = page_table.shape[1] * PAGE
    x = _inproj(h, lw, tm)
    flat = {n: x[n][0] for n in ("q", "k", "g", "pv")}
    flat["v"] = x["v"][0].astype(F32)
    og, op, s_new = _sample_mix(flat, state2, hist16, lw, consts, nseq, ts, past)
    per_seq = lambda a, w: a.reshape(nb, ts, w)
    of = _sample_fox(l, page_table, per_seq(x["fq"][0].astype(F32), HEADS_W), per_seq(x["fk"][0], HEADS_W),
                     per_seq(x["fv"][0], HEADS_W), per_seq(x["aux"][0], LANES), rt3, cache_k, cache_v)
    h_new = _outproj(h, og[None], op[None], of.reshape(1, nb * ts, HEADS_W), x["gates"], lw, tm)
    return h_new, x, s_new


def _largest_tile(n, cap):
    t = min(n, cap)
    while n % t:
        t -= 8
    return t


def kernel(x_prompt, x_sample, cache_fox_k, cache_fox_v, cache_fox_logf, state_gla, state_pool, page_table,
           meta_tokens, norm_pre, norm_post, w_in, gla_w_up, gla_b_gate, gla_norm, pool_w, pool_scale,
           fox_b_f, w_out):
    B, T, _ = x_prompt.shape
    nb, ts, _ = x_sample.shape
    depth, n_phys = cache_fox_k.shape[:2]
    consts = _constants()
    lp2 = cache_fox_logf.reshape(depth * n_phys, PAGE * N_HEADS)
    rt3 = _lfprep(lp2, consts["tmat"], _largest_tile(depth * n_phys, 256)).reshape(depth * n_phys, 1, 2 * PAGE * N_HEADS)
    state2 = state_gla.reshape(depth, nb, QK_W, HEAD_DIM)
    state2 = jnp.concatenate([state2, state2], axis=-1)
    hist16 = jnp.pad(state_pool, ((0, 0), (0, 0), (HIST_ROWS - POOL_HIST, 0), (0, 0)))
    tm_p, tm_s = _largest_tile(T, 256), _largest_tile(nb * ts, 512)

    h_m, h_p, h_s = meta_tokens[None], x_prompt, x_sample.reshape(1, nb * ts, D_MODEL)
    out = [[] for _ in range(10)]
    for l in range(depth):
        lw = _layer_weights(l, norm_pre, norm_post, w_in, gla_w_up, gla_b_gate, gla_norm, pool_w, pool_scale, fox_b_f, w_out)
        h_m_new, xm, st_m = _sequence_layer(h_m, lw, consts, None, N_META, N_META, N_META)
        prefix = {"k": xm["fk"][0], "v": xm["fv"][0], "aux": xm["aux"][0], "state": st_m[0], "hist": xm["pv"][0]}
        h_p, xp, st_p = _sequence_layer(h_p, lw, consts, prefix, tm_p, tm_p, GLA_CHUNK)
        h_m = h_m_new
        h_s, xs, s_new = _sample_layer(l, h_s, lw, consts, page_table, rt3, cache_fox_k, cache_fox_v,
                                       state2[l], hist16[l], nb, ts, tm_s, 8)
        L = N_META + T
        out[0].append(xp["fk"].reshape(B, L, N_HEADS, HEAD_DIM))
        out[1].append(xp["fv"].reshape(B, L, N_HEADS, HEAD_DIM))
        out[2].append(xp["aux"][:, :, :N_HEADS])
        out[3].append(_state_from_bd(st_p))
        out[4].append(xp["pv"][:, T - POOL_HIST:])
        out[5].append(xs["fk"].reshape(nb, ts, N_HEADS, HEAD_DIM))
        out[6].append(xs["fv"].reshape(nb, ts, N_HEADS, HEAD_DIM))
        out[7].append(xs["aux"][0, :, :N_HEADS].reshape(nb, ts, N_HEADS))
        out[8].append(s_new[:, :, :HEAD_DIM].reshape(nb, N_HEADS, GLA_DK, HEAD_DIM))
        out[9].append(jnp.concatenate([state_pool[l], xs["pv"][0].reshape(nb, ts, POOL_W)], axis=1)[:, ts:])
    return (h_p, h_s.reshape(nb, ts, D_MODEL)) + tuple(jnp.stack(o) for o in out)
```

```python
import functools

import jax
import jax.numpy as jnp
from jax import lax
from jax.experimental import pallas as pl
from jax.experimental.pallas import tpu as pltpu

F32 = jnp.float32
BF16 = jnp.bfloat16

D_MODEL = 1024
N_HEADS = 6
GLA_DK = 32
HEAD_DIM = 64
GLA_LR = 16
GLA_GATE_NORM = 16.0
GLA_CHUNK = 64
QK_W = N_HEADS * GLA_DK
HEADS_W = N_HEADS * HEAD_DIM
POOL_W = 256
POOL_GROUP = 64
POOL_WINDOWS = (2, 4, 8, 16)
POOL_HIST = 15
HIST_ROWS = 16
N_META = 16
PAGE = 128
EPS = 1e-6
LANES = 128
NEG = -1e30

COL_Q, COL_K, COL_GV, COL_GATES, COL_PV = 0, 256, 512, 896, 1920
COL_FQ, COL_FK, COL_FV, COL_AUX, W_IN_COLS = 2176, 2560, 2944, 3328, 3456
GATES_W = HEADS_W + POOL_W + HEADS_W
AUX_FF, AUX_LR = 0, 8
SPLIT_SIZES = (QK_W, QK_W, HEADS_W, GLA_LR, HEADS_W, POOL_W, POOL_W, HEADS_W, HEADS_W, HEADS_W, N_HEADS, HEADS_W)

AUXK_ONES = (18, 36)


def _cparams(sem, vmem_mb):
    return pltpu.CompilerParams(dimension_semantics=sem, vmem_limit_bytes=vmem_mb * 1024 * 1024)


def _const_spec(shape):
    return pl.BlockSpec(shape, lambda *_: (0,) * len(shape))


def _dot(a, b):
    return jnp.dot(a, b, preferred_element_type=F32)


def _dot_nt(a, b):
    return lax.dot_general(a, b, (((1,), (1,)), ((), ())), preferred_element_type=F32)


def _dot_tn(a, b):
    return lax.dot_general(a, b, (((0,), (0,)), ((), ())), preferred_element_type=F32)


def _log_sigmoid(x):
    return jnp.minimum(x, 0.0) - jnp.log1p(jnp.exp(-jnp.abs(x)))


def _silu(x):
    return x * (1.0 / (1.0 + jnp.exp(-x)))


def _split3(x):
    hi = x.astype(BF16)
    r = x - hi.astype(F32)
    mid = r.astype(BF16)
    lo = (r - mid.astype(F32)).astype(BF16)
    return hi, mid, lo


def _cumsum_rows(x):
    n = x.shape[0]
    row = lax.broadcasted_iota(jnp.int32, x.shape, 0)
    k = 1
    while k < n:
        x = x + jnp.where(row >= k, pltpu.roll(x, k, axis=0), 0.0)
        k *= 2
    return x


def _inproj_kernel(*refs, tm, n_pre):
    if n_pre:
        (x_ref, pre_ref, w_ref, wup_ref, bg_ref, bf_ref, kpre_ref, vpre_ref, apre_ref,
         q_ref, k_ref, g_ref, v_ref, gates_ref, pv_ref, fq_ref, fk_ref, fv_ref, aux_ref) = refs
    else:
        (x_ref, pre_ref, w_ref, wup_ref, bg_ref, bf_ref,
         q_ref, k_ref, g_ref, v_ref, gates_ref, pv_ref, fq_ref, fk_ref, fv_ref, aux_ref) = refs
    t = pl.program_id(1)
    x = x_ref[0]
    xn = x * lax.rsqrt(jnp.mean(x * x, axis=-1, keepdims=True) + EPS) * pre_ref[...]
    xb = xn.astype(BF16)

    def seg(c0, width):
        return _dot(xb, w_ref[:, c0:c0 + width])

    q_ref[0] = seg(COL_Q, QK_W) * (GLA_DK ** -0.5)
    k_ref[0] = seg(COL_K, QK_W)
    v_ref[0] = seg(COL_GV, HEADS_W).astype(BF16)
    gates_ref[0] = seg(COL_GATES, GATES_W)
    pv_ref[0] = seg(COL_PV, POOL_W)
    fq_ref[0] = (seg(COL_FQ, HEADS_W) * (HEAD_DIM ** -0.5)).astype(BF16)
    aux = seg(COL_AUX, LANES)
    g_ref[0] = _log_sigmoid(_dot(aux.astype(BF16), wup_ref[...]) + bg_ref[...]) * (1.0 / GLA_GATE_NORM)
    rows = pl.ds(pl.multiple_of(n_pre + t * tm, 8), tm)
    fk_ref[0, rows, :] = seg(COL_FK, HEADS_W)
    fv_ref[0, rows, :] = seg(COL_FV, HEADS_W)
    aux_ref[0, rows, :] = _log_sigmoid(aux + bf_ref[...])
    if n_pre:
        @pl.when(t == 0)
        def _():
            fk_ref[0, 0:n_pre, :] = kpre_ref[...]
            fv_ref[0, 0:n_pre, :] = vpre_ref[...]
            aux_ref[0, 0:n_pre, :] = apre_ref[...]


def _inproj(h, lw, tm, prefix=None):
    B, T, _ = h.shape
    n_pre = 0 if prefix is None else prefix["k"].shape[0]
    L = n_pre + T
    tile = lambda w: pl.BlockSpec((1, tm, w), lambda b, t: (b, t, 0))
    whole = lambda w: pl.BlockSpec((1, L, w), lambda b, t: (b, 0, 0))
    in_specs = [tile(D_MODEL), _const_spec((1, D_MODEL)), _const_spec((D_MODEL, W_IN_COLS)),
                _const_spec((LANES, QK_W)), _const_spec((1, QK_W)), _const_spec((1, LANES))]
    args = [h, lw["pre"], lw["w_in"], lw["w_up"], lw["b_gate"], lw["bf"]]
    if n_pre:
        in_specs += [_const_spec((n_pre, HEADS_W)), _const_spec((n_pre, HEADS_W)), _const_spec((n_pre, LANES))]
        args += [prefix["k"], prefix["v"], prefix["aux"]]
    out_shape = [jax.ShapeDtypeStruct((B, T, QK_W), F32), jax.ShapeDtypeStruct((B, T, QK_W), F32),
                 jax.ShapeDtypeStruct((B, T, QK_W), F32), jax.ShapeDtypeStruct((B, T, HEADS_W), BF16),
                 jax.ShapeDtypeStruct((B, T, GATES_W), F32), jax.ShapeDtypeStruct((B, T, POOL_W), F32),
                 jax.ShapeDtypeStruct((B, T, HEADS_W), BF16), jax.ShapeDtypeStruct((B, L, HEADS_W), F32),
                 jax.ShapeDtypeStruct((B, L, HEADS_W), F32), jax.ShapeDtypeStruct((B, L, LANES), F32)]
    out_specs = [tile(QK_W), tile(QK_W), tile(QK_W), tile(HEADS_W), tile(GATES_W), tile(POOL_W),
                 tile(HEADS_W), whole(HEADS_W), whole(HEADS_W), whole(LANES)]
    names = ("q", "k", "g", "v", "gates", "pv", "fq", "fk", "fv", "aux")
    outs = pl.pallas_call(
        functools.partial(_inproj_kernel, tm=tm, n_pre=n_pre),
        grid=(B, T // tm), in_specs=in_specs, out_specs=out_specs, out_shape=out_shape,
        compiler_params=_cparams(("parallel", "arbitrary"), 56), name="inproj")(*args)
    return dict(zip(names, outs))


def _gla_masks(chunk):
    lane_qk = lax.broadcasted_iota(jnp.int32, (1, QK_W), 1) // GLA_DK
    lane_v = lax.broadcasted_iota(jnp.int32, (1, HEADS_W), 1) // HEAD_DIM
    r = lax.broadcasted_iota(jnp.int32, (N_HEADS * chunk, chunk), 0)
    c = lax.broadcasted_iota(jnp.int32, (N_HEADS * chunk, chunk), 1)
    tril = (r % chunk) >= c
    return lane_qk, lane_v, tril


def _gla_intra(q, k, g, v_bf, masks, chunk):
    lane_qk, lane_v, tril = masks
    bcum = _cumsum_rows(g)
    qe_bf = (q * jnp.exp(bcum)).astype(BF16)
    ke_bf = (k * jnp.exp(-bcum)).astype(BF16)
    b_end = bcum[chunk - 1:chunk, :]
    kd_bf = (k * jnp.exp(b_end - bcum)).astype(BF16)
    zero = jnp.zeros_like(qe_bf)
    qe_stack = jnp.concatenate([jnp.where(lane_qk == h, qe_bf, zero) for h in range(N_HEADS)], axis=0)
    a = jnp.where(tril, _dot_nt(qe_stack, ke_bf), 0.0)
    o_full = _dot(a.astype(BF16), v_bf)
    o_intra = jnp.zeros((chunk, HEADS_W), F32)
    for h in range(N_HEADS):
        o_intra = o_intra + jnp.where(lane_v == h, o_full[h * chunk:(h + 1) * chunk, :], 0.0)
    return o_intra, qe_bf, kd_bf, jnp.exp(b_end)


def _head_norm(o, gmat_bf, gn):
    o2 = o * o
    hi = o2.astype(BF16)
    lo = (o2 - hi.astype(F32)).astype(BF16)
    ms = _dot(hi, gmat_bf) + _dot(lo, gmat_bf)
    return o * lax.rsqrt(ms + EPS) * gn


def _pool_delta(ext_ref, pv, pos0, tm):
    ext_ref[HIST_ROWS:HIST_ROWS + tm, :] = pv
    lane_g = lax.broadcasted_iota(jnp.int32, (1, POOL_W), 1) // POOL_GROUP
    pos = pos0 + lax.broadcasted_iota(jnp.int32, (tm, 1), 0)
    run = pv
    sums = jnp.zeros((tm, POOL_W), F32)
    cnt = jnp.zeros((tm, POOL_W), F32)
    for k in range(1, POOL_WINDOWS[-1]):
        run = run + ext_ref[HIST_ROWS - k:HIST_ROWS - k + tm, :]
        if k + 1 in POOL_WINDOWS:
            gi = POOL_WINDOWS.index(k + 1)
            sums = jnp.where(lane_g == gi, run, sums)
            cnt = jnp.where(lane_g == gi, jnp.minimum(pos + 1, k + 1).astype(F32), cnt)
    return sums / cnt - pv


def _glapool_kernel(q_ref, k_ref, g_ref, v_ref, pv_ref, s0_ref, hist0_ref, gn_ref, gmat_ref, pw_ref, ps_ref,
                    og_ref, op_ref, sout_ref, st_ref, ext_ref, *, tm, chunk, pos0):
    t = pl.program_id(1)

    @pl.when(t == 0)
    def _():
        st_ref[...] = s0_ref[...]
        ext_ref[0:HIST_ROWS, :] = hist0_ref[...]

    masks = _gla_masks(chunk)
    rr = lax.broadcasted_iota(jnp.int32, (HEADS_W, QK_W), 0) // HEAD_DIM
    cc = lax.broadcasted_iota(jnp.int32, (HEADS_W, QK_W), 1) // GLA_DK
    bd = rr == cc
    st = st_ref[...]
    outs = []
    for c in range(tm // chunk):
        r = slice(c * chunk, (c + 1) * chunk)
        v_bf = v_ref[0, r, :]
        o_intra, qe_bf, kd_bf, decay = _gla_intra(q_ref[0, r, :], k_ref[0, r, :], g_ref[0, r, :], v_bf, masks, chunk)
        outs.append(o_intra + _dot_nt(qe_bf, st.astype(BF16)))
        st = st * decay + jnp.where(bd, _dot_tn(v_bf, kd_bf), 0.0)
    st_ref[...] = st
    o = outs[0] if len(outs) == 1 else jnp.concatenate(outs, axis=0)
    og_ref[0] = _head_norm(o, gmat_ref[...], gn_ref[...])
    d = _pool_delta(ext_ref, pv_ref[0], pos0 + t * tm, tm)
    op_ref[0] = _dot(d.astype(BF16), pw_ref[...]) * ps_ref[...]
    ext_ref[0:HIST_ROWS, :] = ext_ref[tm:tm + HIST_ROWS, :]

    @pl.when(t == pl.num_programs(1) - 1)
    def _():
        sout_ref[0] = st


def _glapool(x, lw, consts, tm, chunk, pos0, s0, hist0):
    B, T, _ = x["q"].shape
    tile = lambda w: pl.BlockSpec((1, tm, w), lambda b, t: (b, t, 0))
    in_specs = [tile(QK_W), tile(QK_W), tile(QK_W), tile(HEADS_W), tile(POOL_W),
                _const_spec((HEADS_W, QK_W)), _const_spec((HIST_ROWS, POOL_W)), _const_spec((1, HEADS_W)),
                _const_spec((HEADS_W, HEADS_W)), _const_spec((POOL_W, POOL_W)), _const_spec((1, POOL_W))]
    out_shape = [jax.ShapeDtypeStruct((B, T, HEADS_W), F32), jax.ShapeDtypeStruct((B, T, POOL_W), F32),
                 jax.ShapeDtypeStruct((B, HEADS_W, QK_W), F32)]
    out_specs = [tile(HEADS_W), tile(POOL_W), pl.BlockSpec((1, HEADS_W, QK_W), lambda b, t: (b, 0, 0))]
    return pl.pallas_call(
        functools.partial(_glapool_kernel, tm=tm, chunk=chunk, pos0=pos0),
        grid=(B, T // tm), in_specs=in_specs, out_specs=out_specs, out_shape=out_shape,
        scratch_shapes=[pltpu.VMEM((HEADS_W, QK_W), F32), pltpu.VMEM((HIST_ROWS + tm, POOL_W), F32)],
        compiler_params=_cparams(("parallel", "arbitrary"), 48), name="glapool")(
            x["q"], x["k"], x["g"], x["v"], x["pv"], s0, hist0, lw["gla_norm"], consts["gmat"], lw["pool_w"], lw["pool_scale"])


def _fox_features(lf, carry, tri_bf, place_ref):
    r = lf.shape[0]
    lane = lax.broadcasted_iota(jnp.int32, (1, LANES), 1)
    tri = tri_bf[0:r, 0:r]
    hi, mid, lo = _split3(lf)
    f = _dot(tri, hi) + _dot(tri, mid) + _dot(tri, lo) + carry
    fh, fm, fl = _split3(f)
    ones_k = jnp.where((lane >= AUXK_ONES[0]) & (lane < AUXK_ONES[1]), 1.0, 0.0)
    ones_q = jnp.where(lane < AUXK_ONES[0], 1.0, 0.0)
    kaux = ones_k - (_dot(fh, place_ref[0]) + _dot(fm, place_ref[1]) + _dot(fl, place_ref[2]))
    qaux = ones_q + (_dot(fh, place_ref[3]) + _dot(fm, place_ref[4]) + _dot(fl, place_ref[5]))
    return kaux.astype(BF16), qaux.astype(BF16), f[r - 1:r, :]


def _softmax_step(s, v2, m, l, acc):
    m_new = jnp.maximum(m, jnp.max(s, axis=-1, keepdims=True))
    alpha = jnp.exp(m - m_new)
    p = jnp.exp(s - m_new)
    l = alpha * l + jnp.sum(p, axis=-1, keepdims=True)
    acc = alpha * acc + _dot(p.astype(BF16), v2)
    return m_new, l, acc


def _fox_kernel(fq_ref, k_ref, v_ref, aux_ref, tri_ref, place_ref, o_ref,
                kb_ref, vb_ref, kaux_ref, qaux_ref, q2_ref, m_ref, acc_ref, *, T, n_pre, tq):
    t = pl.program_id(1)

    @pl.when(t == 0)
    def _():
        kb_ref[...] = k_ref[0].astype(BF16)
        vb_ref[...] = v_ref[0].astype(BF16)
        cb = min(PAGE, T)
        blocks = ([(0, n_pre)] if n_pre else []) + [(n_pre + i * cb, cb) for i in range(T // cb)]
        carry = jnp.zeros((1, LANES), F32)
        for r0, r in blocks:
            ka, qa, carry = _fox_features(aux_ref[0, r0:r0 + r, :], carry, tri_ref[...], place_ref)
            kaux_ref[r0:r0 + r, :] = ka
            qaux_ref[r0:r0 + r, :] = qa

    lane = lax.broadcasted_iota(jnp.int32, (1, LANES), 1)
    q0 = pl.multiple_of(n_pre + t * tq, 16)
    qa = qaux_ref[pl.ds(q0, tq), :]
    fq = fq_ref[0]
    zero_bf = jnp.zeros((tq, LANES), BF16)
    for h in range(N_HEADS):
        fq_p = fq[:, (h // 2) * LANES:(h // 2 + 1) * LANES]
        qmask = (lane == h) | (lane == 6 + h) | (lane == 12 + h) | (lane == 18 + h) | (lane == 24 + h) | (lane == 30 + h)
        q2_ref[h] = jnp.concatenate([jnp.where((lane // HEAD_DIM) == (h % 2), fq_p, zero_bf), jnp.where(qmask, qa, zero_bf)], axis=1)
    m_ref[...] = jnp.full(m_ref.shape, NEG, F32)
    acc_ref[...] = jnp.zeros(acc_ref.shape, F32)

    def block(r0, n, causal):
        kaux = kaux_ref[pl.ds(r0, n), :]
        ones = jnp.ones((n, LANES), BF16)
        if causal:
            keep = lax.broadcasted_iota(jnp.int32, (tq, n), 0) >= lax.broadcasted_iota(jnp.int32, (tq, n), 1)
        for p in range(N_HEADS // 2):
            cols = slice(p * LANES, (p + 1) * LANES)
            k2 = jnp.concatenate([kb_ref[pl.ds(r0, n), cols], kaux], axis=1)
            v2 = jnp.concatenate([vb_ref[pl.ds(r0, n), cols], ones], axis=1)
            for h in (2 * p, 2 * p + 1):
                s = _dot_nt(q2_ref[h], k2)
                if causal:
                    s = jnp.where(keep, s, NEG)
                m_old = m_ref[h]
                m_new = jnp.maximum(m_old, jnp.max(s, axis=-1, keepdims=True))
                alpha = jnp.exp(m_old - m_new)
                m_b = m_new[:, :n] if n < LANES else jnp.concatenate([m_new] * (n // LANES), axis=1)
                pe = jnp.exp(s - m_b).astype(BF16)
                acc_ref[h] = jnp.concatenate([alpha, alpha], axis=1) * acc_ref[h] + _dot(pe, v2)
                m_ref[h] = m_new

    if n_pre:
        block(0, n_pre, False)

    def body(j, c):
        block(pl.multiple_of(n_pre + j * tq, 16), tq, False)
        return c

    lax.fori_loop(0, t, body, 0)
    block(q0, tq, True)
    for p in range(N_HEADS // 2):
        a0, a1 = acc_ref[2 * p], acc_ref[2 * p + 1]
        o_ref[0, :, p * LANES:(p + 1) * LANES] = jnp.where((lane // HEAD_DIM) == 0, a0[:, :LANES] / a0[:, LANES:],
                                                            a1[:, :LANES] / a1[:, LANES:])


def _fox(x, consts, tq, n_pre):
    B, T, _ = x["fq"].shape
    L = n_pre + T
    whole = lambda w: pl.BlockSpec((1, L, w), lambda b, t: (b, 0, 0))
    in_specs = [pl.BlockSpec((1, tq, HEADS_W), lambda b, t: (b, t, 0)), whole(HEADS_W), whole(HEADS_W), whole(LANES),
                _const_spec((PAGE, PAGE)), _const_spec((6, LANES, LANES))]
    return pl.pallas_call(
        functools.partial(_fox_kernel, T=T, n_pre=n_pre, tq=tq),
        grid=(B, T // tq), in_specs=in_specs,
        out_specs=pl.BlockSpec((1, tq, HEADS_W), lambda b, t: (b, t, 0)),
        out_shape=jax.ShapeDtypeStruct((B, T, HEADS_W), F32),
        scratch_shapes=[pltpu.VMEM((L, HEADS_W), BF16), pltpu.VMEM((L, HEADS_W), BF16),
                        pltpu.VMEM((L, LANES), BF16), pltpu.VMEM((L, LANES), BF16),
                        pltpu.VMEM((N_HEADS, tq, 2 * LANES), BF16), pltpu.VMEM((N_HEADS, tq, LANES), F32),
                        pltpu.VMEM((N_HEADS, tq, 2 * LANES), F32)],
        compiler_params=_cparams(("parallel", "arbitrary"), 48), name="fox")(
            x["fq"], x["fk"], x["fv"], x["aux"], consts["tri"], consts["place"])


def _outproj_kernel(h_ref, og_ref, op_ref, of_ref, gates_ref, wo_ref, post_ref, out_ref):
    y_in = jnp.concatenate([og_ref[0], op_ref[0], of_ref[0]], axis=1) * _silu(gates_ref[0])
    y = _dot(y_in.astype(BF16), wo_ref[...])
    out_ref[0] = h_ref[0] + y * lax.rsqrt(jnp.mean(y * y, axis=-1, keepdims=True) + EPS) * post_ref[...]


def _outproj(h, og, op, of, gates, lw, tm):
    B, T, _ = h.shape
    tile = lambda w: pl.BlockSpec((1, tm, w), lambda b, t: (b, t, 0))
    return pl.pallas_call(
        _outproj_kernel, grid=(B, T // tm),
        in_specs=[tile(D_MODEL), tile(HEADS_W), tile(POOL_W), tile(HEADS_W), tile(GATES_W),
                  _const_spec((GATES_W, D_MODEL)), _const_spec((1, D_MODEL))],
        out_specs=tile(D_MODEL), out_shape=jax.ShapeDtypeStruct((B, T, D_MODEL), F32),
        compiler_params=_cparams(("parallel", "parallel"), 48), name="outproj")(
            h, og, op, of, gates, lw["w_out"], lw["post"])


def _sample_mix_kernel(q_ref, k_ref, g_ref, v_ref, pv_ref, s_ref, hist_ref, gn_ref, gmat_ref, pw_ref, ps_ref,
                       og_ref, op_ref, sout_ref, ext_ref, *, nseq, ts, pos0):
    masks = _gla_masks(ts)
    rr = lax.broadcasted_iota(jnp.int32, (QK_W, HEADS_W), 0) // GLA_DK
    cc = lax.broadcasted_iota(jnp.int32, (QK_W, HEADS_W), 1) // HEAD_DIM
    bd = rr == cc
    ones = jnp.ones((3 * ts, LANES), BF16)
    outs, deltas = [], []
    for i in range(nseq):
        r = slice(i * ts, (i + 1) * ts)
        g = g_ref[r, :]
        v_bf = v_ref[r, :].astype(BF16)
        o_intra, qe_bf, kd_bf, _ = _gla_intra(q_ref[r, :], k_ref[r, :], g, v_bf, masks, ts)
        s2 = s_ref[i]
        s_bd = jnp.where(bd, jnp.concatenate([s2, s2, s2], axis=1), 0.0).astype(BF16)
        outs.append(o_intra + _dot(qe_bf, s_bd))
        u = jnp.where(bd, _dot_tn(kd_bf, v_bf), 0.0)
        x = jnp.concatenate([u[64 * j:64 * (j + 1), LANES * j:LANES * (j + 1)] for j in range(N_HEADS // 2)], axis=0)
        x = x + pltpu.roll(x, HEAD_DIM, axis=1)
        gh, gm, gl = _split3(g)
        gparts = jnp.concatenate([gh.astype(F32), gm.astype(F32), gl.astype(F32)], axis=0).astype(BF16)
        sout_ref[i] = s2 * jnp.exp(_dot_tn(gparts, ones)) + x
        ext_ref[0:HIST_ROWS, :] = hist_ref[i]
        deltas.append(_pool_delta(ext_ref, pv_ref[r, :], pos0, ts))
    og_ref[...] = _head_norm(jnp.concatenate(outs, axis=0), gmat_ref[...], gn_ref[...])
    op_ref[...] = _dot(jnp.concatenate(deltas, axis=0).astype(BF16), pw_ref[...]) * ps_ref[...]


def _sample_mix(x, state2, hist16, lw, consts, nseq, ts, pos0):
    n = x["q"].shape[0]
    rows = nseq * ts
    tile = lambda w: pl.BlockSpec((rows, w), lambda i: (i, 0))
    in_specs = [tile(QK_W), tile(QK_W), tile(QK_W), tile(HEADS_W), tile(POOL_W),
                pl.BlockSpec((nseq, QK_W, LANES), lambda i: (i, 0, 0)), pl.BlockSpec((nseq, HIST_ROWS, POOL_W), lambda i: (i, 0, 0)),
                _const_spec((1, HEADS_W)), _const_spec((HEADS_W, HEADS_W)), _const_spec((POOL_W, POOL_W)), _const_spec((1, POOL_W))]
    out_shape = [jax.ShapeDtypeStruct((n, HEADS_W), F32), jax.ShapeDtypeStruct((n, POOL_W), F32),
                 jax.ShapeDtypeStruct(state2.shape, F32)]
    out_specs = [tile(HEADS_W), tile(POOL_W), pl.BlockSpec((nseq, QK_W, LANES), lambda i: (i, 0, 0))]
    return pl.pallas_call(
        functools.partial(_sample_mix_kernel, nseq=nseq, ts=ts, pos0=pos0),
        grid=(n // rows,), in_specs=in_specs, out_specs=out_specs, out_shape=out_shape,
        scratch_shapes=[pltpu.VMEM((HIST_ROWS + ts, POOL_W), F32)],
        compiler_params=_cparams(("parallel",), 48), name="sample_mix")(
            x["q"], x["k"], x["g"], x["v"], x["pv"], state2, hist16, lw["gla_norm"], consts["gmat"], lw["pool_w"], lw["pool_scale"])


def _suffix_rows(x, period):
    n = x.shape[0]
    row = lax.broadcasted_iota(jnp.int32, x.shape, 0) % period
    k = 1
    while k < period:
        x = x + jnp.where(row + k < period, pltpu.roll(x, n - k, axis=0), 0.0)
        k *= 2
    return x


def _sample_fox_kernel(pt_ref, fq_ref, kn_ref, vn_ref, an_ref, sw_ref, *rest, n_pages, ts):
    lf_refs, k_refs, v_refs = rest[:n_pages], rest[n_pages:2 * n_pages], rest[2 * n_pages:3 * n_pages]
    o_ref = rest[3 * n_pages]
    n = N_HEADS * n_pages
    lp = jnp.concatenate([lf_refs[p][0, h, 0] for h in range(N_HEADS) for p in range(n_pages)], axis=0)
    hi, mid, lo = _split3(lp)
    parts = jnp.concatenate([hi.astype(F32), mid.astype(F32), lo.astype(F32)], axis=0).astype(BF16)
    r3 = _dot(parts, sw_ref[...])
    r2 = r3[0:n] + r3[n:2 * n] + r3[2 * n:3 * n]
    tot = r2[:, PAGE:]
    rfull = r2[:, :PAGE] + (_suffix_rows(tot, n_pages) - tot)
    cn = _cumsum_rows(an_ref[0])
    cparts = _split3(cn)
    lane = lax.broadcasted_iota(jnp.int32, (ts, LANES), 1)
    ti = lax.broadcasted_iota(jnp.int32, (ts, ts), 0)
    ui = lax.broadcasted_iota(jnp.int32, (ts, ts), 1)
    fq, kn, vn = fq_ref[0], kn_ref[0], vn_ref[0]
    for h in range(N_HEADS):
        hs = slice(h * HEAD_DIM, (h + 1) * HEAD_DIM)
        q_h = fq[:, hs].astype(BF16)
        cq = cn[:, h:h + 1]
        sel = jnp.where(lane == h, 1.0, 0.0).astype(BF16)
        cn_t = _dot_nt(sel, cparts[0]) + _dot_nt(sel, cparts[1]) + _dot_nt(sel, cparts[2])
        s_new = jnp.where(ti >= ui, _dot_nt(q_h, kn[:, hs].astype(BF16)) + cq - cn_t, NEG)
        s_pages = []
        for p in range(n_pages):
            kt = k_refs[p][0, 0, h].astype(BF16)
            s_pages.append(_dot(q_h, kt) + rfull[h * n_pages + p:h * n_pages + p + 1, :] + cq)
        mm = s_pages[0]
        for s in s_pages[1:]:
            mm = jnp.maximum(mm, s)
        m = jnp.maximum(jnp.max(mm, axis=-1, keepdims=True), jnp.max(s_new, axis=-1, keepdims=True))
        p_new = jnp.exp(s_new - m)
        l = jnp.sum(p_new, axis=-1, keepdims=True)
        acc = _dot(p_new.astype(BF16), vn[:, hs].astype(BF16))
        for p in range(n_pages):
            pe = jnp.exp(s_pages[p] - m)
            l = l + jnp.sum(pe, axis=-1, keepdims=True)
            acc = acc + _dot_nt(pe.astype(BF16), v_refs[p][0, 0, h].astype(BF16))
        o_ref[0, :, hs] = acc / l


def _sample_fox(layer, page_table, fq, kn, vn, an, sw, lf5, cache_kt, cache_vt):
    nb, ts, _ = fq.shape
    n_pages = page_table.shape[1]
    seq = lambda w: pl.BlockSpec((1, ts, w), lambda b, pt: (b, 0, 0))
    lf_specs = [pl.BlockSpec((1, N_HEADS, 1, 1, PAGE), lambda b, pt, p=p: (layer, 0, pt[b, p], 0, 0)) for p in range(n_pages)]
    page_specs = [pl.BlockSpec((1, 1, N_HEADS, HEAD_DIM, PAGE), lambda b, pt, p=p: (layer, pt[b, p], 0, 0, 0)) for p in range(n_pages)]
    grid_spec = pltpu.PrefetchScalarGridSpec(
        num_scalar_prefetch=1, grid=(nb,),
        in_specs=[seq(HEADS_W), seq(HEADS_W), seq(HEADS_W), seq(LANES), pl.BlockSpec((PAGE, 2 * PAGE), lambda b, pt: (0, 0))]
        + lf_specs + page_specs + page_specs,
        out_specs=pl.BlockSpec((1, ts, HEADS_W), lambda b, pt: (b, 0, 0)))
    return pl.pallas_call(
        functools.partial(_sample_fox_kernel, n_pages=n_pages, ts=ts), grid_spec=grid_spec,
        out_shape=jax.ShapeDtypeStruct((nb, ts, HEADS_W), F32),
        compiler_params=_cparams(("arbitrary",), 48), name="sample_fox")(
            page_table, fq, kn, vn, an, sw, *([lf5] * n_pages), *([cache_kt] * n_pages), *([cache_vt] * n_pages))


def _layer_weights(l, norm_pre, norm_post, w_in, gla_w_up, gla_b_gate, gla_norm, pool_w, pool_scale, fox_b_f, w_out):
    idx = [sum(SPLIT_SIZES[:i + 1]) for i in range(len(SPLIT_SIZES) - 1)]
    gq, gk, gv, glr, gg, pv, pg, fq, fk, fv, ff, fg = jnp.split(w_in[l], idx, axis=1)
    zc = lambda n: jnp.zeros((D_MODEL, n), F32)
    aux = jnp.concatenate([ff, zc(AUX_LR - N_HEADS), glr, zc(LANES - AUX_LR - GLA_LR)], axis=1)
    w = jnp.concatenate([gq, zc(COL_K - QK_W), gk, zc(COL_GV - COL_K - QK_W), gv, gg, pg, fg, pv, fq, fk, fv, aux], axis=1)
    w_up = jnp.zeros((LANES, QK_W), F32).at[AUX_LR:AUX_LR + GLA_LR].set(gla_w_up[l])
    bf = jnp.zeros((1, LANES), F32).at[0, AUX_FF:AUX_FF + N_HEADS].set(fox_b_f[l])
    pw = jnp.zeros((POOL_W, POOL_W), F32)
    for gi in range(len(POOL_WINDOWS)):
        sl = slice(gi * POOL_GROUP, (gi + 1) * POOL_GROUP)
        pw = pw.at[sl, sl].set(pool_w[l, gi])
    return {"pre": norm_pre[l][None], "post": norm_post[l][None], "w_in": w.astype(BF16), "w_up": w_up.astype(BF16),
            "b_gate": gla_b_gate[l][None], "bf": bf, "gla_norm": jnp.tile(gla_norm[l], N_HEADS)[None],
            "pool_w": pw.astype(BF16), "pool_scale": pool_scale[l][None], "w_out": w_out[l].astype(BF16)}


def _constants():
    i = jnp.arange(HEADS_W)
    gmat = jnp.where((i[:, None] // HEAD_DIM) == (i[None, :] // HEAD_DIM), 1.0 / HEAD_DIM, 0.0).astype(BF16)
    r = jnp.arange(PAGE)
    tri = (r[:, None] >= r[None, :]).astype(BF16)
    a = jnp.arange(LANES)
    place = jnp.stack([((a[:, None] < N_HEADS) & (a[None, :] == a[:, None] + off)).astype(BF16)
                       for off in (0, 6, 12, 18, 24, 30)])
    sw = jnp.concatenate([(r[:, None] > r[None, :]).astype(BF16), jnp.ones((PAGE, PAGE), BF16)], axis=1)
    return {"gmat": gmat, "tri": tri, "place": place, "sw": sw}


def _state_from_bd(st):
    B = st.shape[0]
    s = st.reshape(B, N_HEADS, HEAD_DIM, N_HEADS, GLA_DK)
    s = jnp.stack([s[:, h, :, h, :] for h in range(N_HEADS)], axis=1)
    return jnp.swapaxes(s, 2, 3)


def _sequence_layer(h, lw, consts, prefix, tm, tq, chunk):
    n_pre = 0 if prefix is None else prefix["k"].shape[0]
    x = _inproj(h, lw, tm, prefix)
    s0 = jnp.zeros((HEADS_W, QK_W), F32) if prefix is None else prefix["state"]
    hist0 = jnp.zeros((HIST_ROWS, POOL_W), F32) if prefix is None else prefix["hist"]
    og, op, st = _glapool(x, lw, consts, tm, chunk, n_pre, s0, hist0)
    of = _fox(x, consts, tq, n_pre)
    h_new = _outproj(h, og, op, of, x["gates"], lw, tm)
    return h_new, x, st


def _sample_layer(l, h, lw, consts, page_table, lf5, cache_kt, cache_vt, state2, hist16, nb, ts, tm, nseq):
    past = page_table.shape[1] * PAGE
    x = _inproj(h, lw, tm)
    flat = {n: x[n][0] for n in ("q", "k", "g", "pv")}
    flat["v"] = x["v"][0].astype(F32)
    og, op, s_new = _sample_mix(flat, state2, hist16, lw, consts, nseq, ts, past)
    per_seq = lambda a, w: a.reshape(nb, ts, w)
    of = _sample_fox(l, page_table, per_seq(x["fq"][0].astype(F32), HEADS_W), per_seq(x["fk"][0], HEADS_W),
                     per_seq(x["fv"][0], HEADS_W), per_seq(x["aux"][0], LANES), consts["sw"], lf5, cache_kt, cache_vt)
    h_new = _outproj(h, og[None], op[None], of.reshape(1, nb * ts, HEADS_W), x["gates"], lw, tm)
    return h_new, x, s_new


def _largest_tile(n, cap):
    t = min(n, cap)
    while n % t:
        t -= 8
    return t


def kernel(x_prompt, x_sample, cache_fox_k, cache_fox_v, cache_fox_logf, state_gla, state_pool, page_table,
           meta_tokens, norm_pre, norm_post, w_in, gla_w_up, gla_b_gate, gla_norm, pool_w, pool_scale,
           fox_b_f, w_out):
    B, T, _ = x_prompt.shape
    nb, ts, _ = x_sample.shape
    depth, n_phys = cache_fox_k.shape[:2]
    consts = _constants()
    cache_kt = jnp.transpose(cache_fox_k, (0, 1, 3, 4, 2))
    cache_vt = jnp.transpose(cache_fox_v, (0, 1, 3, 4, 2))
    lf5 = jnp.transpose(cache_fox_logf, (0, 3, 1, 2)).reshape(depth, N_HEADS, n_phys, 1, PAGE)
    state2 = state_gla.reshape(depth, nb, QK_W, HEAD_DIM)
    state2 = jnp.concatenate([state2, state2], axis=-1)
    hist16 = jnp.pad(state_pool, ((0, 0), (0, 0), (HIST_ROWS - POOL_HIST, 0), (0, 0)))
    tm_p, tm_s = _largest_tile(T, 256), _largest_tile(nb * ts, 512)

    h_m, h_p, h_s = meta_tokens[None], x_prompt, x_sample.reshape(1, nb * ts, D_MODEL)
    out = [[] for _ in range(10)]
    for l in range(depth):
        lw = _layer_weights(l, norm_pre, norm_post, w_in, gla_w_up, gla_b_gate, gla_norm, pool_w, pool_scale, fox_b_f, w_out)
        h_m_new, xm, st_m = _sequence_layer(h_m, lw, consts, None, N_META, N_META, N_META)
        prefix = {"k": xm["fk"][0], "v": xm["fv"][0], "aux": xm["aux"][0], "state": st_m[0], "hist": xm["pv"][0]}
        h_p, xp, st_p = _sequence_layer(h_p, lw, consts, prefix, tm_p, tm_p, GLA_CHUNK)
        h_m = h_m_new
        h_s, xs, s_new = _sample_layer(l, h_s, lw, consts, page_table, lf5, cache_kt, cache_vt,
                                       state2[l], hist16[l], nb, ts, tm_s, 8)
        L = N_META + T
        out[0].append(xp["fk"].reshape(B, L, N_HEADS, HEAD_DIM))
        out[1].append(xp["fv"].reshape(B, L, N_HEADS, HEAD_DIM))
        out[2].append(xp["aux"][:, :, :N_HEADS])
        out[3].append(_state_from_bd(st_p))
        out[4].append(xp["pv"][:, T - POOL_HIST:])
        out[5].append(xs["fk"].reshape(nb, ts, N_HEADS, HEAD_DIM))
        out[6].append(xs["fv"].reshape(nb, ts, N_HEADS, HEAD_DIM))
        out[7].append(xs["aux"][0, :, :N_HEADS].reshape(nb, ts, N_HEADS))
        out[8].append(s_new[:, :, :HEAD_DIM].reshape(nb, N_HEADS, GLA_DK, HEAD_DIM))
        out[9].append(jnp.concatenate([state_pool[l], xs["pv"][0].reshape(nb, ts, POOL_W)], axis=1)[:, ts:])
    return (h_p, h_s.reshape(nb, ts, D_MODEL)) + tuple(jnp.stack(o) for o in out)
```

```python
import functools

import jax
import jax.numpy as jnp
from jax import lax
from jax.experimental import pallas as pl
from jax.experimental.pallas import tpu as pltpu

F32 = jnp.float32
BF16 = jnp.bfloat16

D_MODEL = 1024
N_HEADS = 6
GLA_DK = 32
HEAD_DIM = 64
GLA_LR = 16
GLA_GATE_NORM = 16.0
GLA_CHUNK = 64
QK_W = N_HEADS * GLA_DK
HEADS_W = N_HEADS * HEAD_DIM
POOL_W = 256
POOL_GROUP = 64
POOL_WINDOWS = (2, 4, 8, 16)
POOL_HIST = 15
HIST_ROWS = 16
N_META = 16
PAGE = 128
EPS = 1e-6
LANES = 128
NEG = -1e30

COL_Q, COL_K, COL_GV, COL_GATES, COL_PV = 0, 256, 512, 896, 1920
COL_FQ, COL_FK, COL_FV, COL_AUX, W_IN_COLS = 2176, 2560, 2944, 3328, 3456
GATES_W = HEADS_W + POOL_W + HEADS_W
AUX_FF, AUX_LR = 0, 8
SPLIT_SIZES = (QK_W, QK_W, HEADS_W, GLA_LR, HEADS_W, POOL_W, POOL_W, HEADS_W, HEADS_W, HEADS_W, N_HEADS, HEADS_W)

AUXK_ONES = (18, 36)


def _cparams(sem, vmem_mb):
    return pltpu.CompilerParams(dimension_semantics=sem, vmem_limit_bytes=vmem_mb * 1024 * 1024)


def _const_spec(shape):
    return pl.BlockSpec(shape, lambda *_: (0,) * len(shape))


def _dot(a, b):
    return jnp.dot(a, b, preferred_element_type=F32)


def _dot_nt(a, b):
    return lax.dot_general(a, b, (((1,), (1,)), ((), ())), preferred_element_type=F32)


def _dot_tn(a, b):
    return lax.dot_general(a, b, (((0,), (0,)), ((), ())), preferred_element_type=F32)


def _log_sigmoid(x):
    return jnp.minimum(x, 0.0) - jnp.log1p(jnp.exp(-jnp.abs(x)))


def _silu(x):
    return x * (1.0 / (1.0 + jnp.exp(-x)))


def _split3(x):
    hi = x.astype(BF16)
    r = x - hi.astype(F32)
    mid = r.astype(BF16)
    lo = (r - mid.astype(F32)).astype(BF16)
    return hi, mid, lo


def _cumsum_rows(x):
    n = x.shape[0]
    row = lax.broadcasted_iota(jnp.int32, x.shape, 0)
    k = 1
    while k < n:
        x = x + jnp.where(row >= k, pltpu.roll(x, k, axis=0), 0.0)
        k *= 2
    return x


def _inproj_kernel(*refs, tm, n_pre):
    if n_pre:
        (x_ref, pre_ref, w_ref, wup_ref, bg_ref, bf_ref, kpre_ref, vpre_ref, apre_ref,
         q_ref, k_ref, g_ref, v_ref, gates_ref, pv_ref, fq_ref, fk_ref, fv_ref, aux_ref) = refs
    else:
        (x_ref, pre_ref, w_ref, wup_ref, bg_ref, bf_ref,
         q_ref, k_ref, g_ref, v_ref, gates_ref, pv_ref, fq_ref, fk_ref, fv_ref, aux_ref) = refs
    t = pl.program_id(1)
    x = x_ref[0]
    xn = x * lax.rsqrt(jnp.mean(x * x, axis=-1, keepdims=True) + EPS) * pre_ref[...]
    xb = xn.astype(BF16)

    def seg(c0, width):
        return _dot(xb, w_ref[:, c0:c0 + width])

    q_ref[0] = seg(COL_Q, QK_W) * (GLA_DK ** -0.5)
    k_ref[0] = seg(COL_K, QK_W)
    v_ref[0] = seg(COL_GV, HEADS_W).astype(BF16)
    gates_ref[0] = seg(COL_GATES, GATES_W)
    pv_ref[0] = seg(COL_PV, POOL_W)
    fq_ref[0] = (seg(COL_FQ, HEADS_W) * (HEAD_DIM ** -0.5)).astype(BF16)
    aux = seg(COL_AUX, LANES)
    g_ref[0] = _log_sigmoid(_dot(aux.astype(BF16), wup_ref[...]) + bg_ref[...]) * (1.0 / GLA_GATE_NORM)
    rows = pl.ds(pl.multiple_of(n_pre + t * tm, 8), tm)
    fk_ref[0, rows, :] = seg(COL_FK, HEADS_W)
    fv_ref[0, rows, :] = seg(COL_FV, HEADS_W)
    aux_ref[0, rows, :] = _log_sigmoid(aux + bf_ref[...])
    if n_pre:
        @pl.when(t == 0)
        def _():
            fk_ref[0, 0:n_pre, :] = kpre_ref[...]
            fv_ref[0, 0:n_pre, :] = vpre_ref[...]
            aux_ref[0, 0:n_pre, :] = apre_ref[...]


def _inproj(h, lw, tm, prefix=None):
    B, T, _ = h.shape
    n_pre = 0 if prefix is None else prefix["k"].shape[0]
    L = n_pre + T
    tile = lambda w: pl.BlockSpec((1, tm, w), lambda b, t: (b, t, 0))
    whole = lambda w: pl.BlockSpec((1, L, w), lambda b, t: (b, 0, 0))
    in_specs = [tile(D_MODEL), _const_spec((1, D_MODEL)), _const_spec((D_MODEL, W_IN_COLS)),
                _const_spec((LANES, QK_W)), _const_spec((1, QK_W)), _const_spec((1, LANES))]
    args = [h, lw["pre"], lw["w_in"], lw["w_up"], lw["b_gate"], lw["bf"]]
    if n_pre:
        in_specs += [_const_spec((n_pre, HEADS_W)), _const_spec((n_pre, HEADS_W)), _const_spec((n_pre, LANES))]
        args += [prefix["k"], prefix["v"], prefix["aux"]]
    out_shape = [jax.ShapeDtypeStruct((B, T, QK_W), F32), jax.ShapeDtypeStruct((B, T, QK_W), F32),
                 jax.ShapeDtypeStruct((B, T, QK_W), F32), jax.ShapeDtypeStruct((B, T, HEADS_W), BF16),
                 jax.ShapeDtypeStruct((B, T, GATES_W), F32), jax.ShapeDtypeStruct((B, T, POOL_W), F32),
                 jax.ShapeDtypeStruct((B, T, HEADS_W), BF16), jax.ShapeDtypeStruct((B, L, HEADS_W), F32),
                 jax.ShapeDtypeStruct((B, L, HEADS_W), F32), jax.ShapeDtypeStruct((B, L, LANES), F32)]
    out_specs = [tile(QK_W), tile(QK_W), tile(QK_W), tile(HEADS_W), tile(GATES_W), tile(POOL_W),
                 tile(HEADS_W), whole(HEADS_W), whole(HEADS_W), whole(LANES)]
    names = ("q", "k", "g", "v", "gates", "pv", "fq", "fk", "fv", "aux")
    outs = pl.pallas_call(
        functools.partial(_inproj_kernel, tm=tm, n_pre=n_pre),
        grid=(B, T // tm), in_specs=in_specs, out_specs=out_specs, out_shape=out_shape,
        compiler_params=_cparams(("parallel", "arbitrary"), 56), name="inproj")(*args)
    return dict(zip(names, outs))


def _gla_masks(chunk):
    lane_qk = lax.broadcasted_iota(jnp.int32, (1, QK_W), 1) // GLA_DK
    lane_v = lax.broadcasted_iota(jnp.int32, (1, HEADS_W), 1) // HEAD_DIM
    r = lax.broadcasted_iota(jnp.int32, (N_HEADS * chunk, chunk), 0)
    c = lax.broadcasted_iota(jnp.int32, (N_HEADS * chunk, chunk), 1)
    tril = (r % chunk) >= c
    return lane_qk, lane_v, tril


def _gla_intra(q, k, g, v_bf, masks, chunk):
    lane_qk, lane_v, tril = masks
    bcum = _cumsum_rows(g)
    qe_bf = (q * jnp.exp(bcum)).astype(BF16)
    ke_bf = (k * jnp.exp(-bcum)).astype(BF16)
    b_end = bcum[chunk - 1:chunk, :]
    kd_bf = (k * jnp.exp(b_end - bcum)).astype(BF16)
    zero = jnp.zeros_like(qe_bf)
    qe_stack = jnp.concatenate([jnp.where(lane_qk == h, qe_bf, zero) for h in range(N_HEADS)], axis=0)
    a = jnp.where(tril, _dot_nt(qe_stack, ke_bf), 0.0)
    o_full = _dot(a.astype(BF16), v_bf)
    o_intra = jnp.zeros((chunk, HEADS_W), F32)
    for h in range(N_HEADS):
        o_intra = o_intra + jnp.where(lane_v == h, o_full[h * chunk:(h + 1) * chunk, :], 0.0)
    return o_intra, qe_bf, kd_bf, jnp.exp(b_end)


def _head_norm(o, gmat_bf, gn):
    o2 = o * o
    hi = o2.astype(BF16)
    lo = (o2 - hi.astype(F32)).astype(BF16)
    ms = _dot(hi, gmat_bf) + _dot(lo, gmat_bf)
    return o * lax.rsqrt(ms + EPS) * gn


def _pool_delta(ext_ref, pv, pos0, tm):
    ext_ref[HIST_ROWS:HIST_ROWS + tm, :] = pv
    lane_g = lax.broadcasted_iota(jnp.int32, (1, POOL_W), 1) // POOL_GROUP
    pos = pos0 + lax.broadcasted_iota(jnp.int32, (tm, 1), 0)
    run = pv
    sums = jnp.zeros((tm, POOL_W), F32)
    cnt = jnp.zeros((tm, POOL_W), F32)
    for k in range(1, POOL_WINDOWS[-1]):
        run = run + ext_ref[HIST_ROWS - k:HIST_ROWS - k + tm, :]
        if k + 1 in POOL_WINDOWS:
            gi = POOL_WINDOWS.index(k + 1)
            sums = jnp.where(lane_g == gi, run, sums)
            cnt = jnp.where(lane_g == gi, jnp.minimum(pos + 1, k + 1).astype(F32), cnt)
    return sums / cnt - pv


def _glapool_kernel(q_ref, k_ref, g_ref, v_ref, pv_ref, s0_ref, hist0_ref, gn_ref, gmat_ref, pw_ref, ps_ref,
                    og_ref, op_ref, sout_ref, st_ref, ext_ref, *, tm, chunk, pos0):
    t = pl.program_id(1)

    @pl.when(t == 0)
    def _():
        st_ref[...] = s0_ref[...]
        ext_ref[0:HIST_ROWS, :] = hist0_ref[...]

    masks = _gla_masks(chunk)
    rr = lax.broadcasted_iota(jnp.int32, (HEADS_W, QK_W), 0) // HEAD_DIM
    cc = lax.broadcasted_iota(jnp.int32, (HEADS_W, QK_W), 1) // GLA_DK
    bd = rr == cc
    st = st_ref[...]
    outs = []
    for c in range(tm // chunk):
        r = slice(c * chunk, (c + 1) * chunk)
        v_bf = v_ref[0, r, :]
        o_intra, qe_bf, kd_bf, decay = _gla_intra(q_ref[0, r, :], k_ref[0, r, :], g_ref[0, r, :], v_bf, masks, chunk)
        outs.append(o_intra + _dot_nt(qe_bf, st.astype(BF16)))
        st = st * decay + jnp.where(bd, _dot_tn(v_bf, kd_bf), 0.0)
    st_ref[...] = st
    o = outs[0] if len(outs) == 1 else jnp.concatenate(outs, axis=0)
    og_ref[0] = _head_norm(o, gmat_ref[...], gn_ref[...])
    d = _pool_delta(ext_ref, pv_ref[0], pos0 + t * tm, tm)
    op_ref[0] = _dot(d.astype(BF16), pw_ref[...]) * ps_ref[...]
    ext_ref[0:HIST_ROWS, :] = ext_ref[tm:tm + HIST_ROWS, :]

    @pl.when(t == pl.num_programs(1) - 1)
    def _():
        sout_ref[0] = st


def _glapool(x, lw, consts, tm, chunk, pos0, s0, hist0):
    B, T, _ = x["q"].shape
    tile = lambda w: pl.BlockSpec((1, tm, w), lambda b, t: (b, t, 0))
    in_specs = [tile(QK_W), tile(QK_W), tile(QK_W), tile(HEADS_W), tile(POOL_W),
                _const_spec((HEADS_W, QK_W)), _const_spec((HIST_ROWS, POOL_W)), _const_spec((1, HEADS_W)),
                _const_spec((HEADS_W, HEADS_W)), _const_spec((POOL_W, POOL_W)), _const_spec((1, POOL_W))]
    out_shape = [jax.ShapeDtypeStruct((B, T, HEADS_W), F32), jax.ShapeDtypeStruct((B, T, POOL_W), F32),
                 jax.ShapeDtypeStruct((B, HEADS_W, QK_W), F32)]
    out_specs = [tile(HEADS_W), tile(POOL_W), pl.BlockSpec((1, HEADS_W, QK_W), lambda b, t: (b, 0, 0))]
    return pl.pallas_call(
        functools.partial(_glapool_kernel, tm=tm, chunk=chunk, pos0=pos0),
        grid=(B, T // tm), in_specs=in_specs, out_specs=out_specs, out_shape=out_shape,
        scratch_shapes=[pltpu.VMEM((HEADS_W, QK_W), F32), pltpu.VMEM((HIST_ROWS + tm, POOL_W), F32)],
        compiler_params=_cparams(("parallel", "arbitrary"), 48), name="glapool")(
            x["q"], x["k"], x["g"], x["v"], x["pv"], s0, hist0, lw["gla_norm"], consts["gmat"], lw["pool_w"], lw["pool_scale"])


def _fox_features(lf, carry, tri_bf, place_ref):
    r = lf.shape[0]
    lane = lax.broadcasted_iota(jnp.int32, (1, LANES), 1)
    tri = tri_bf[0:r, 0:r]
    hi, mid, lo = _split3(lf)
    f = _dot(tri, hi) + _dot(tri, mid) + _dot(tri, lo) + carry
    fh, fm, fl = _split3(f)
    ones_k = jnp.where((lane >= AUXK_ONES[0]) & (lane < AUXK_ONES[1]), 1.0, 0.0)
    ones_q = jnp.where(lane < AUXK_ONES[0], 1.0, 0.0)
    kaux = ones_k - (_dot(fh, place_ref[0]) + _dot(fm, place_ref[1]) + _dot(fl, place_ref[2]))
    qaux = ones_q + (_dot(fh, place_ref[3]) + _dot(fm, place_ref[4]) + _dot(fl, place_ref[5]))
    return kaux.astype(BF16), qaux.astype(BF16), f[r - 1:r, :]


def _fox_kernel(fq_ref, k_ref, v_ref, aux_ref, tri_ref, place_ref, o_ref,
                kb_ref, vb_ref, kaux_ref, qaux_ref, q2_ref, m_ref, acc_ref, *, T, n_pre, tq):
    t = pl.program_id(1)

    @pl.when(t == 0)
    def _():
        kb_ref[...] = k_ref[0].astype(BF16)
        vb_ref[...] = v_ref[0].astype(BF16)
        cb = min(PAGE, T)
        blocks = ([(0, n_pre)] if n_pre else []) + [(n_pre + i * cb, cb) for i in range(T // cb)]
        carry = jnp.zeros((1, LANES), F32)
        for r0, r in blocks:
            ka, qa, carry = _fox_features(aux_ref[0, r0:r0 + r, :], carry, tri_ref[...], place_ref)
            kaux_ref[r0:r0 + r, :] = ka
            qaux_ref[r0:r0 + r, :] = qa

    lane = lax.broadcasted_iota(jnp.int32, (1, LANES), 1)
    q0 = pl.multiple_of(n_pre + t * tq, 16)
    qa = qaux_ref[pl.ds(q0, tq), :]
    fq = fq_ref[0]
    zero_bf = jnp.zeros((tq, LANES), BF16)
    for h in range(N_HEADS):
        fq_p = fq[:, (h // 2) * LANES:(h // 2 + 1) * LANES]
        qmask = (lane == h) | (lane == 6 + h) | (lane == 12 + h) | (lane == 18 + h) | (lane == 24 + h) | (lane == 30 + h)
        q2_ref[h] = jnp.concatenate([jnp.where((lane // HEAD_DIM) == (h % 2), fq_p, zero_bf), jnp.where(qmask, qa, zero_bf)], axis=1)
    m_ref[...] = jnp.full(m_ref.shape, NEG, F32)
    acc_ref[...] = jnp.zeros(acc_ref.shape, F32)

    def block(r0, n, causal):
        kaux = kaux_ref[pl.ds(r0, n), :]
        one_bf = jnp.ones((n, LANES), BF16)
        if causal:
            keep = lax.broadcasted_iota(jnp.int32, (tq, n), 0) >= lax.broadcasted_iota(jnp.int32, (tq, n), 1)
        for p in range(N_HEADS // 2):
            cols = slice(p * LANES, (p + 1) * LANES)
            k2 = jnp.concatenate([kb_ref[pl.ds(r0, n), cols], kaux], axis=1)
            vp = vb_ref[pl.ds(r0, n), cols]
            for h in (2 * p, 2 * p + 1):
                s = _dot_nt(q2_ref[h], k2)
                if causal:
                    s = jnp.where(keep, s, NEG)
                m_old = m_ref[h]
                m_new = jnp.maximum(m_old, jnp.max(s, axis=-1, keepdims=True))
                m_b = m_new[:, :n] if n < LANES else jnp.concatenate([m_new] * (n // LANES), axis=1)
                pe = jnp.exp(s - m_b).astype(BF16)
                v2 = jnp.where((lane // HEAD_DIM) == (h % 2), vp, one_bf)
                acc_ref[h] = jnp.exp(m_old - m_new) * acc_ref[h] + _dot(pe, v2)
                m_ref[h] = m_new

    if n_pre:
        block(0, n_pre, False)

    def body(j, c):
        block(pl.multiple_of(n_pre + j * 2 * tq, 16), 2 * tq, False)
        return c

    lax.fori_loop(0, t // 2, body, 0)

    @pl.when(t % 2 == 1)
    def _():
        block(pl.multiple_of(n_pre + (t - 1) * tq, 16), tq, False)

    block(q0, tq, True)
    for p in range(N_HEADS // 2):
        a0, a1 = acc_ref[2 * p], acc_ref[2 * p + 1]
        o0 = a0 / pltpu.roll(a0, HEAD_DIM, axis=1)
        o1 = a1 / pltpu.roll(a1, HEAD_DIM, axis=1)
        o_ref[0, :, p * LANES:(p + 1) * LANES] = jnp.where((lane // HEAD_DIM) == 0, o0, o1)


def _fox(x, consts, tq, n_pre):
    B, T, _ = x["fq"].shape
    L = n_pre + T
    whole = lambda w: pl.BlockSpec((1, L, w), lambda b, t: (b, 0, 0))
    in_specs = [pl.BlockSpec((1, tq, HEADS_W), lambda b, t: (b, t, 0)), whole(HEADS_W), whole(HEADS_W), whole(LANES),
                _const_spec((PAGE, PAGE)), _const_spec((6, LANES, LANES))]
    return pl.pallas_call(
        functools.partial(_fox_kernel, T=T, n_pre=n_pre, tq=tq),
        grid=(B, T // tq), in_specs=in_specs,
        out_specs=pl.BlockSpec((1, tq, HEADS_W), lambda b, t: (b, t, 0)),
        out_shape=jax.ShapeDtypeStruct((B, T, HEADS_W), F32),
        scratch_shapes=[pltpu.VMEM((L, HEADS_W), BF16), pltpu.VMEM((L, HEADS_W), BF16),
                        pltpu.VMEM((L, LANES), BF16), pltpu.VMEM((L, LANES), BF16),
                        pltpu.VMEM((N_HEADS, tq, 2 * LANES), BF16), pltpu.VMEM((N_HEADS, tq, LANES), F32),
                        pltpu.VMEM((N_HEADS, tq, LANES), F32)],
        compiler_params=_cparams(("parallel", "arbitrary"), 48), name="fox")(
            x["fq"], x["fk"], x["fv"], x["aux"], consts["tri"], consts["place"])


def _outproj_kernel(h_ref, og_ref, op_ref, of_ref, gates_ref, wo_ref, post_ref, out_ref):
    y_in = jnp.concatenate([og_ref[0], op_ref[0], of_ref[0]], axis=1) * _silu(gates_ref[0])
    y = _dot(y_in.astype(BF16), wo_ref[...])
    out_ref[0] = h_ref[0] + y * lax.rsqrt(jnp.mean(y * y, axis=-1, keepdims=True) + EPS) * post_ref[...]


def _outproj(h, og, op, of, gates, lw, tm):
    B, T, _ = h.shape
    tile = lambda w: pl.BlockSpec((1, tm, w), lambda b, t: (b, t, 0))
    return pl.pallas_call(
        _outproj_kernel, grid=(B, T // tm),
        in_specs=[tile(D_MODEL), tile(HEADS_W), tile(POOL_W), tile(HEADS_W), tile(GATES_W),
                  _const_spec((GATES_W, D_MODEL)), _const_spec((1, D_MODEL))],
        out_specs=tile(D_MODEL), out_shape=jax.ShapeDtypeStruct((B, T, D_MODEL), F32),
        compiler_params=_cparams(("parallel", "parallel"), 48), name="outproj")(
            h, og, op, of, gates, lw["w_out"], lw["post"])


def _sample_mix_kernel(q_ref, k_ref, g_ref, v_ref, pv_ref, s_ref, hist_ref, gn_ref, gmat_ref, pw_ref, ps_ref,
                       og_ref, op_ref, sout_ref, ext_ref, *, nseq, ts, pos0):
    masks = _gla_masks(ts)
    rr = lax.broadcasted_iota(jnp.int32, (QK_W, HEADS_W), 0) // GLA_DK
    cc = lax.broadcasted_iota(jnp.int32, (QK_W, HEADS_W), 1) // HEAD_DIM
    bd = rr == cc
    ones = jnp.ones((3 * ts, LANES), BF16)
    outs, deltas = [], []
    for i in range(nseq):
        r = slice(i * ts, (i + 1) * ts)
        g = g_ref[r, :]
        v_bf = v_ref[r, :].astype(BF16)
        o_intra, qe_bf, kd_bf, _ = _gla_intra(q_ref[r, :], k_ref[r, :], g, v_bf, masks, ts)
        s2 = s_ref[i]
        s_bd = jnp.where(bd, jnp.concatenate([s2, s2, s2], axis=1), 0.0).astype(BF16)
        outs.append(o_intra + _dot(qe_bf, s_bd))
        u = jnp.where(bd, _dot_tn(kd_bf, v_bf), 0.0)
        x = jnp.concatenate([u[64 * j:64 * (j + 1), LANES * j:LANES * (j + 1)] for j in range(N_HEADS // 2)], axis=0)
        x = x + pltpu.roll(x, HEAD_DIM, axis=1)
        gh, gm, gl = _split3(g)
        gparts = jnp.concatenate([gh.astype(F32), gm.astype(F32), gl.astype(F32)], axis=0).astype(BF16)
        sout_ref[i] = s2 * jnp.exp(_dot_tn(gparts, ones)) + x
        ext_ref[0:HIST_ROWS, :] = hist_ref[i]
        deltas.append(_pool_delta(ext_ref, pv_ref[r, :], pos0, ts))
    og_ref[...] = _head_norm(jnp.concatenate(outs, axis=0), gmat_ref[...], gn_ref[...])
    op_ref[...] = _dot(jnp.concatenate(deltas, axis=0).astype(BF16), pw_ref[...]) * ps_ref[...]


def _sample_mix(x, state2, hist16, lw, consts, nseq, ts, pos0):
    n = x["q"].shape[0]
    rows = nseq * ts
    tile = lambda w: pl.BlockSpec((rows, w), lambda i: (i, 0))
    in_specs = [tile(QK_W), tile(QK_W), tile(QK_W), tile(HEADS_W), tile(POOL_W),
                pl.BlockSpec((nseq, QK_W, LANES), lambda i: (i, 0, 0)), pl.BlockSpec((nseq, HIST_ROWS, POOL_W), lambda i: (i, 0, 0)),
                _const_spec((1, HEADS_W)), _const_spec((HEADS_W, HEADS_W)), _const_spec((POOL_W, POOL_W)), _const_spec((1, POOL_W))]
    out_shape = [jax.ShapeDtypeStruct((n, HEADS_W), F32), jax.ShapeDtypeStruct((n, POOL_W), F32),
                 jax.ShapeDtypeStruct(state2.shape, F32)]
    out_specs = [tile(HEADS_W), tile(POOL_W), pl.BlockSpec((nseq, QK_W, LANES), lambda i: (i, 0, 0))]
    return pl.pallas_call(
        functools.partial(_sample_mix_kernel, nseq=nseq, ts=ts, pos0=pos0),
        grid=(n // rows,), in_specs=in_specs, out_specs=out_specs, out_shape=out_shape,
        scratch_shapes=[pltpu.VMEM((HIST_ROWS + ts, POOL_W), F32)],
        compiler_params=_cparams(("parallel",), 48), name="sample_mix")(
            x["q"], x["k"], x["g"], x["v"], x["pv"], state2, hist16, lw["gla_norm"], consts["gmat"], lw["pool_w"], lw["pool_scale"])


def _suffix_rows(x, period):
    n = x.shape[0]
    row = lax.broadcasted_iota(jnp.int32, x.shape, 0) % period
    k = 1
    while k < period:
        x = x + jnp.where(row + k < period, pltpu.roll(x, n - k, axis=0), 0.0)
        k *= 2
    return x


def _lfsum_kernel(lf_ref, sw_ref, o_ref):
    n = lf_ref.shape[0]
    hi, mid, lo = _split3(lf_ref[...])
    parts = jnp.concatenate([hi.astype(F32), mid.astype(F32), lo.astype(F32)], axis=0).astype(BF16)
    r3 = _dot(parts, sw_ref[...])
    o_ref[...] = r3[0:n] + r3[n:2 * n] + r3[2 * n:3 * n]


def _lfsum(lf2, sw, tr):
    n = lf2.shape[0]
    return pl.pallas_call(
        _lfsum_kernel, grid=(n // tr,),
        in_specs=[pl.BlockSpec((tr, PAGE), lambda i: (i, 0)), _const_spec((PAGE, 2 * PAGE))],
        out_specs=pl.BlockSpec((tr, 2 * PAGE), lambda i: (i, 0)), out_shape=jax.ShapeDtypeStruct((n, 2 * PAGE), F32),
        compiler_params=_cparams(("parallel",), 32), name="lfsum")(lf2, sw)


def _sample_fox_kernel(pt_ref, fq_ref, kn_ref, vn_ref, an_ref, *rest, n_pages, ts):
    rt_refs, k_refs, v_refs = rest[:n_pages], rest[n_pages:2 * n_pages], rest[2 * n_pages:3 * n_pages]
    o_ref = rest[3 * n_pages]
    r2 = jnp.concatenate([rt_refs[p][0, h, 0] for h in range(N_HEADS) for p in range(n_pages)], axis=0)
    tot = r2[:, PAGE:]
    rfull = r2[:, :PAGE] + (_suffix_rows(tot, n_pages) - tot)
    cn = _cumsum_rows(an_ref[0])
    cn2 = jnp.concatenate([cn, cn], axis=0)
    cparts = _split3(cn)
    lane = lax.broadcasted_iota(jnp.int32, (2 * ts, LANES), 1)
    first = lax.broadcasted_iota(jnp.int32, (2 * ts, LANES), 0) < ts
    own = (lane < HEAD_DIM) == first
    causal = (lax.broadcasted_iota(jnp.int32, (2 * ts, ts), 0) % ts) >= lax.broadcasted_iota(jnp.int32, (2 * ts, ts), 1)
    for pr in range(N_HEADS // 2):
        cols = slice(pr * LANES, (pr + 1) * LANES)
        fq2 = jnp.concatenate([fq_ref[0, :, cols], fq_ref[0, :, cols]], axis=0)
        q2 = jnp.where(own, fq2, 0.0).astype(BF16)
        onehot = lane == jnp.where(first, 2 * pr, 2 * pr + 1)
        cq = jnp.sum(jnp.where(onehot, cn2, 0.0), axis=-1, keepdims=True)
        sel = jnp.where(onehot, 1.0, 0.0).astype(BF16)
        cn_t = _dot_nt(sel, cparts[0]) + _dot_nt(sel, cparts[1]) + _dot_nt(sel, cparts[2])
        s_new = jnp.where(causal, _dot_nt(q2, kn_ref[0, :, cols].astype(BF16)) + cq - cn_t, NEG)
        s_pages = []
        for p in range(n_pages):
            kt = k_refs[p][0, 0, 2 * pr:2 * pr + 2].reshape(LANES, PAGE).astype(BF16)
            r0, r1 = 2 * pr * n_pages + p, (2 * pr + 1) * n_pages + p
            bias = jnp.concatenate([jnp.broadcast_to(rfull[r0:r0 + 1, :], (ts, PAGE)),
                                    jnp.broadcast_to(rfull[r1:r1 + 1, :], (ts, PAGE))], axis=0)
            s_pages.append(_dot(q2, kt) + bias + cq)
        mm = s_pages[0]
        for s in s_pages[1:]:
            mm = jnp.maximum(mm, s)
        m = jnp.maximum(jnp.max(mm, axis=-1, keepdims=True), jnp.max(s_new, axis=-1, keepdims=True))
        p_new = jnp.exp(s_new - m)
        l = jnp.sum(p_new, axis=-1, keepdims=True)
        acc = _dot(p_new.astype(BF16), vn_ref[0, :, cols].astype(BF16))
        for p in range(n_pages):
            pe = jnp.exp(s_pages[p] - m)
            l = l + jnp.sum(pe, axis=-1, keepdims=True)
            vt = v_refs[p][0, 0, 2 * pr:2 * pr + 2].reshape(LANES, PAGE).astype(BF16)
            acc = acc + _dot_nt(pe.astype(BF16), vt)
        o2 = acc / l
        o_ref[0, :, cols] = jnp.where(lane[:ts] < HEAD_DIM, o2[:ts], o2[ts:])


def _sample_fox(layer, page_table, fq, kn, vn, an, rt5, cache_kt, cache_vt):
    nb, ts, _ = fq.shape
    n_pages = page_table.shape[1]
    seq = lambda w: pl.BlockSpec((1, ts, w), lambda b, pt: (b, 0, 0))
    rt_specs = [pl.BlockSpec((1, N_HEADS, 1, 1, 2 * PAGE), lambda b, pt, p=p: (layer, 0, pt[b, p], 0, 0)) for p in range(n_pages)]
    page_specs = [pl.BlockSpec((1, 1, N_HEADS, HEAD_DIM, PAGE), lambda b, pt, p=p: (layer, pt[b, p], 0, 0, 0)) for p in range(n_pages)]
    grid_spec = pltpu.PrefetchScalarGridSpec(
        num_scalar_prefetch=1, grid=(nb,),
        in_specs=[seq(HEADS_W), seq(HEADS_W), seq(HEADS_W), seq(LANES)] + rt_specs + page_specs + page_specs,
        out_specs=pl.BlockSpec((1, ts, HEADS_W), lambda b, pt: (b, 0, 0)))
    return pl.pallas_call(
        functools.partial(_sample_fox_kernel, n_pages=n_pages, ts=ts), grid_spec=grid_spec,
        out_shape=jax.ShapeDtypeStruct((nb, ts, HEADS_W), F32),
        compiler_params=_cparams(("arbitrary",), 48), name="sample_fox")(
            page_table, fq, kn, vn, an, *([rt5] * n_pages), *([cache_kt] * n_pages), *([cache_vt] * n_pages))


def _wprep_kernel(w_ref, o_ref, *, depth):
    per_col = (D_MODEL // LANES) * depth
    for l in range(depth):
        for rt in range(D_MODEL // LANES):
            blk = w_ref[pl.ds(rt * depth + l, LANES, stride=per_col), :]
            o_ref[l, rt * LANES:(rt + 1) * LANES, :] = blk.T.astype(BF16)


def _wprep(w_in):
    depth, d, c = w_in.shape
    per_col = (d // LANES) * depth
    cp = -(-c // LANES) * LANES
    w2 = jnp.transpose(w_in, (2, 0, 1)).reshape(c, depth, d // LANES, LANES)
    w2 = jnp.transpose(w2, (0, 2, 1, 3)).reshape(c * per_col, LANES)
    w2 = jnp.pad(w2, ((0, (cp - c) * per_col), (0, 0)))
    return pl.pallas_call(
        functools.partial(_wprep_kernel, depth=depth), grid=(cp // LANES,),
        in_specs=[pl.BlockSpec((LANES * per_col, LANES), lambda j: (j, 0))],
        out_specs=pl.BlockSpec((depth, d, LANES), lambda j: (0, 0, j)),
        out_shape=jax.ShapeDtypeStruct((depth, d, cp), BF16),
        compiler_params=_cparams(("parallel",), 32), name="wprep")(w2)


def _layer_weights(l, norm_pre, norm_post, w_rm, gla_w_up, gla_b_gate, gla_norm, pool_w, pool_scale, fox_b_f, w_out):
    idx = [sum(SPLIT_SIZES[:i + 1]) for i in range(len(SPLIT_SIZES) - 1)]
    gq, gk, gv, glr, gg, pv, pg, fq, fk, fv, ff, fg = jnp.split(w_rm[l, :, :sum(SPLIT_SIZES)], idx, axis=1)
    zc = lambda n: jnp.zeros((D_MODEL, n), BF16)
    aux = jnp.concatenate([ff, zc(AUX_LR - N_HEADS), glr, zc(LANES - AUX_LR - GLA_LR)], axis=1)
    w = jnp.concatenate([gq, zc(COL_K - QK_W), gk, zc(COL_GV - COL_K - QK_W), gv, gg, pg, fg, pv, fq, fk, fv, aux], axis=1)
    w_up = jnp.zeros((LANES, QK_W), F32).at[AUX_LR:AUX_LR + GLA_LR].set(gla_w_up[l])
    bf = jnp.zeros((1, LANES), F32).at[0, AUX_FF:AUX_FF + N_HEADS].set(fox_b_f[l])
    pw = jnp.zeros((POOL_W, POOL_W), F32)
    for gi in range(len(POOL_WINDOWS)):
        sl = slice(gi * POOL_GROUP, (gi + 1) * POOL_GROUP)
        pw = pw.at[sl, sl].set(pool_w[l, gi])
    return {"pre": norm_pre[l][None], "post": norm_post[l][None], "w_in": w, "w_up": w_up.astype(BF16),
            "b_gate": gla_b_gate[l][None], "bf": bf, "gla_norm": jnp.tile(gla_norm[l], N_HEADS)[None],
            "pool_w": pw.astype(BF16), "pool_scale": pool_scale[l][None], "w_out": w_out[l].astype(BF16)}


def _constants():
    i = jnp.arange(HEADS_W)
    gmat = jnp.where((i[:, None] // HEAD_DIM) == (i[None, :] // HEAD_DIM), 1.0 / HEAD_DIM, 0.0).astype(BF16)
    r = jnp.arange(PAGE)
    tri = (r[:, None] >= r[None, :]).astype(BF16)
    a = jnp.arange(LANES)
    place = jnp.stack([((a[:, None] < N_HEADS) & (a[None, :] == a[:, None] + off)).astype(BF16)
                       for off in (0, 6, 12, 18, 24, 30)])
    sw = jnp.concatenate([(r[:, None] > r[None, :]).astype(BF16), jnp.ones((PAGE, PAGE), BF16)], axis=1)
    return {"gmat": gmat, "tri": tri, "place": place, "sw": sw}


def _state_from_bd(st):
    B = st.shape[0]
    s = st.reshape(B, N_HEADS, HEAD_DIM, N_HEADS, GLA_DK)
    s = jnp.stack([s[:, h, :, h, :] for h in range(N_HEADS)], axis=1)
    return jnp.swapaxes(s, 2, 3)


def _sequence_layer(h, lw, consts, prefix, tm, tq, chunk):
    n_pre = 0 if prefix is None else prefix["k"].shape[0]
    x = _inproj(h, lw, tm, prefix)
    s0 = jnp.zeros((HEADS_W, QK_W), F32) if prefix is None else prefix["state"]
    hist0 = jnp.zeros((HIST_ROWS, POOL_W), F32) if prefix is None else prefix["hist"]
    og, op, st = _glapool(x, lw, consts, tm, chunk, n_pre, s0, hist0)
    of = _fox(x, consts, tq, n_pre)
    h_new = _outproj(h, og, op, of, x["gates"], lw, tm)
    return h_new, x, st


def _sample_layer(l, h, lw, consts, page_table, rt5, cache_kt, cache_vt, state2, hist16, nb, ts, tm, nseq):
    past = page_table.shape[1] * PAGE
    x = _inproj(h, lw, tm)
    flat = {n: x[n][0] for n in ("q", "k", "g", "pv")}
    flat["v"] = x["v"][0].astype(F32)
    og, op, s_new = _sample_mix(flat, state2, hist16, lw, consts, nseq, ts, past)
    per_seq = lambda a, w: a.reshape(nb, ts, w)
    of = _sample_fox(l, page_table, per_seq(x["fq"][0].astype(F32), HEADS_W), per_seq(x["fk"][0], HEADS_W),
                     per_seq(x["fv"][0], HEADS_W), per_seq(x["aux"][0], LANES), rt5, cache_kt, cache_vt)
    h_new = _outproj(h, og[None], op[None], of.reshape(1, nb * ts, HEADS_W), x["gates"], lw, tm)
    return h_new, x, s_new


def _largest_tile(n, cap):
    t = min(n, cap)
    while n % t:
        t -= 8
    return t


def kernel(x_prompt, x_sample, cache_fox_k, cache_fox_v, cache_fox_logf, state_gla, state_pool, page_table,
           meta_tokens, norm_pre, norm_post, w_in, gla_w_up, gla_b_gate, gla_norm, pool_w, pool_scale,
           fox_b_f, w_out):
    B, T, _ = x_prompt.shape
    nb, ts, _ = x_sample.shape
    depth, n_phys = cache_fox_k.shape[:2]
    consts = _constants()
    cache_kt = jnp.transpose(cache_fox_k, (0, 1, 3, 4, 2))
    cache_vt = jnp.transpose(cache_fox_v, (0, 1, 3, 4, 2))
    lf2 = jnp.transpose(cache_fox_logf, (0, 3, 1, 2)).reshape(depth * N_HEADS * n_phys, PAGE)
    rt5 = _lfsum(lf2, consts["sw"], _largest_tile(depth * N_HEADS * n_phys, 512)).reshape(depth, N_HEADS, n_phys, 1, 2 * PAGE)
    state2 = state_gla.reshape(depth, nb, QK_W, HEAD_DIM)
    state2 = jnp.concatenate([state2, state2], axis=-1)
    hist16 = jnp.pad(state_pool, ((0, 0), (0, 0), (HIST_ROWS - POOL_HIST, 0), (0, 0)))
    tm_p, tm_s = _largest_tile(T, 256), _largest_tile(nb * ts, 512)
    w_rm = _wprep(w_in)

    h_m, h_p, h_s = meta_tokens[None], x_prompt, x_sample.reshape(1, nb * ts, D_MODEL)
    out = [[] for _ in range(10)]
    for l in range(depth):
        lw = _layer_weights(l, norm_pre, norm_post, w_rm, gla_w_up, gla_b_gate, gla_norm, pool_w, pool_scale, fox_b_f, w_out)
        h_m_new, xm, st_m = _sequence_layer(h_m, lw, consts, None, N_META, N_META, N_META)
        prefix = {"k": xm["fk"][0], "v": xm["fv"][0], "aux": xm["aux"][0], "state": st_m[0], "hist": xm["pv"][0]}
        h_p, xp, st_p = _sequence_layer(h_p, lw, consts, prefix, tm_p, tm_p, GLA_CHUNK)
        h_m = h_m_new
        h_s, xs, s_new = _sample_layer(l, h_s, lw, consts, page_table, rt5, cache_kt, cache_vt,
                                       state2[l], hist16[l], nb, ts, tm_s, 8)
        L = N_META + T
        out[0].append(xp["fk"].reshape(B, L, N_HEADS, HEAD_DIM))
        out[1].append(xp["fv"].reshape(B, L, N_HEADS, HEAD_DIM))
        out[2].append(xp["aux"][:, :, :N_HEADS])
        out[3].append(_state_from_bd(st_p))
        out[4].append(xp["pv"][:, T - POOL_HIST:])
        out[5].append(xs["fk"].reshape(nb, ts, N_HEADS, HEAD_DIM))
        out[6].append(xs["fv"].reshape(nb, ts, N_HEADS, HEAD_DIM))
        out[7].append(xs["aux"][0, :, :N_HEADS].reshape(nb, ts, N_HEADS))
        out[8].append(s_new[:, :, :HEAD_DIM].reshape(nb, N_HEADS, GLA_DK, HEAD_DIM))
        out[9].append(jnp.concatenate([state_pool[l], xs["pv"][0].reshape(nb, ts, POOL_W)], axis=1)[:, ts:])
    return (h_p, h_s.reshape(nb, ts, D_MODEL)) + tuple(jnp.stack(o) for o in out)
```

```python
import functools

import jax
import jax.numpy as jnp
from jax import lax
from jax.experimental import pallas as pl
from jax.experimental.pallas import tpu as pltpu

F32 = jnp.float32
BF16 = jnp.bfloat16

D_MODEL = 1024
N_HEADS = 6
GLA_DK = 32
HEAD_DIM = 64
GLA_LR = 16
GLA_GATE_NORM = 16.0
GLA_CHUNK = 64
QK_W = N_HEADS * GLA_DK
HEADS_W = N_HEADS * HEAD_DIM
POOL_W = 256
POOL_GROUP = 64
POOL_WINDOWS = (2, 4, 8, 16)
POOL_HIST = 15
HIST_ROWS = 16
N_META = 16
PAGE = 128
EPS = 1e-6
LANES = 128
NEG = -1e30

COL_Q, COL_K, COL_GV, COL_GATES, COL_PV = 0, 256, 512, 896, 1920
COL_FQ, COL_FK, COL_FV, COL_AUX, W_IN_COLS = 2176, 2560, 2944, 3328, 3456
GATES_W = HEADS_W + POOL_W + HEADS_W
AUX_FF, AUX_LR = 0, 8
SPLIT_SIZES = (QK_W, QK_W, HEADS_W, GLA_LR, HEADS_W, POOL_W, POOL_W, HEADS_W, HEADS_W, HEADS_W, N_HEADS, HEADS_W)

AUXK_ONES = (18, 36)


def _cparams(sem, vmem_mb):
    return pltpu.CompilerParams(dimension_semantics=sem, vmem_limit_bytes=vmem_mb * 1024 * 1024)


def _const_spec(shape):
    return pl.BlockSpec(shape, lambda *_: (0,) * len(shape))


def _dot(a, b):
    return jnp.dot(a, b, preferred_element_type=F32)


def _dot_nt(a, b):
    return lax.dot_general(a, b, (((1,), (1,)), ((), ())), preferred_element_type=F32)


def _dot_tn(a, b):
    return lax.dot_general(a, b, (((0,), (0,)), ((), ())), preferred_element_type=F32)


def _log_sigmoid(x):
    return jnp.minimum(x, 0.0) - jnp.log1p(jnp.exp(-jnp.abs(x)))


def _silu(x):
    return x * (1.0 / (1.0 + jnp.exp(-x)))


def _split3(x):
    hi = x.astype(BF16)
    r = x - hi.astype(F32)
    mid = r.astype(BF16)
    lo = (r - mid.astype(F32)).astype(BF16)
    return hi, mid, lo


def _cumsum_rows(x):
    n = x.shape[0]
    row = lax.broadcasted_iota(jnp.int32, x.shape, 0)
    k = 1
    while k < n:
        x = x + jnp.where(row >= k, pltpu.roll(x, k, axis=0), 0.0)
        k *= 2
    return x


def _inproj_kernel(*refs, tm, n_pre):
    if n_pre:
        (x_ref, pre_ref, w_ref, wup_ref, bg_ref, bf_ref, kpre_ref, vpre_ref, apre_ref,
         q_ref, k_ref, g_ref, v_ref, gates_ref, pv_ref, fq_ref, fk_ref, fv_ref, aux_ref) = refs
    else:
        (x_ref, pre_ref, w_ref, wup_ref, bg_ref, bf_ref,
         q_ref, k_ref, g_ref, v_ref, gates_ref, pv_ref, fq_ref, fk_ref, fv_ref, aux_ref) = refs
    t = pl.program_id(1)
    x = x_ref[0]
    xn = x * lax.rsqrt(jnp.mean(x * x, axis=-1, keepdims=True) + EPS) * pre_ref[...]
    xb = xn.astype(BF16)

    def seg(c0, width):
        return _dot(xb, w_ref[:, c0:c0 + width])

    q_ref[0] = seg(COL_Q, QK_W) * (GLA_DK ** -0.5)
    k_ref[0] = seg(COL_K, QK_W)
    v_ref[0] = seg(COL_GV, HEADS_W).astype(BF16)
    gates_ref[0] = seg(COL_GATES, GATES_W)
    pv_ref[0] = seg(COL_PV, POOL_W)
    fq_ref[0] = (seg(COL_FQ, HEADS_W) * (HEAD_DIM ** -0.5)).astype(BF16)
    aux = seg(COL_AUX, LANES)
    g_ref[0] = _log_sigmoid(_dot(aux.astype(BF16), wup_ref[...]) + bg_ref[...]) * (1.0 / GLA_GATE_NORM)
    rows = pl.ds(pl.multiple_of(n_pre + t * tm, 8), tm)
    fk_ref[0, rows, :] = seg(COL_FK, HEADS_W)
    fv_ref[0, rows, :] = seg(COL_FV, HEADS_W)
    aux_ref[0, rows, :] = _log_sigmoid(aux + bf_ref[...])
    if n_pre:
        @pl.when(t == 0)
        def _():
            fk_ref[0, 0:n_pre, :] = kpre_ref[...]
            fv_ref[0, 0:n_pre, :] = vpre_ref[...]
            aux_ref[0, 0:n_pre, :] = apre_ref[...]


def _inproj(h, lw, tm, prefix=None):
    B, T, _ = h.shape
    n_pre = 0 if prefix is None else prefix["k"].shape[0]
    L = n_pre + T
    tile = lambda w: pl.BlockSpec((1, tm, w), lambda b, t: (b, t, 0))
    whole = lambda w: pl.BlockSpec((1, L, w), lambda b, t: (b, 0, 0))
    in_specs = [tile(D_MODEL), _const_spec((1, D_MODEL)), _const_spec((D_MODEL, W_IN_COLS)),
                _const_spec((LANES, QK_W)), _const_spec((1, QK_W)), _const_spec((1, LANES))]
    args = [h, lw["pre"], lw["w_in"], lw["w_up"], lw["b_gate"], lw["bf"]]
    if n_pre:
        in_specs += [_const_spec((n_pre, HEADS_W)), _const_spec((n_pre, HEADS_W)), _const_spec((n_pre, LANES))]
        args += [prefix["k"], prefix["v"], prefix["aux"]]
    out_shape = [jax.ShapeDtypeStruct((B, T, QK_W), F32), jax.ShapeDtypeStruct((B, T, QK_W), F32),
                 jax.ShapeDtypeStruct((B, T, QK_W), F32), jax.ShapeDtypeStruct((B, T, HEADS_W), BF16),
                 jax.ShapeDtypeStruct((B, T, GATES_W), F32), jax.ShapeDtypeStruct((B, T, POOL_W), F32),
                 jax.ShapeDtypeStruct((B, T, HEADS_W), BF16), jax.ShapeDtypeStruct((B, L, HEADS_W), F32),
                 jax.ShapeDtypeStruct((B, L, HEADS_W), F32), jax.ShapeDtypeStruct((B, L, LANES), F32)]
    out_specs = [tile(QK_W), tile(QK_W), tile(QK_W), tile(HEADS_W), tile(GATES_W), tile(POOL_W),
                 tile(HEADS_W), whole(HEADS_W), whole(HEADS_W), whole(LANES)]
    names = ("q", "k", "g", "v", "gates", "pv", "fq", "fk", "fv", "aux")
    outs = pl.pallas_call(
        functools.partial(_inproj_kernel, tm=tm, n_pre=n_pre),
        grid=(B, T // tm), in_specs=in_specs, out_specs=out_specs, out_shape=out_shape,
        compiler_params=_cparams(("parallel", "arbitrary"), 56), name="inproj")(*args)
    return dict(zip(names, outs))


def _gla_masks(chunk):
    lane_qk = lax.broadcasted_iota(jnp.int32, (1, QK_W), 1) // GLA_DK
    lane_v = lax.broadcasted_iota(jnp.int32, (1, HEADS_W), 1) // HEAD_DIM
    r = lax.broadcasted_iota(jnp.int32, (N_HEADS * chunk, chunk), 0)
    c = lax.broadcasted_iota(jnp.int32, (N_HEADS * chunk, chunk), 1)
    tril = (r % chunk) >= c
    return lane_qk, lane_v, tril


def _gla_intra(q, k, g, v_bf, masks, chunk):
    lane_qk, lane_v, tril = masks
    bcum = _cumsum_rows(g)
    qe_bf = (q * jnp.exp(bcum)).astype(BF16)
    ke_bf = (k * jnp.exp(-bcum)).astype(BF16)
    b_end = bcum[chunk - 1:chunk, :]
    kd_bf = (k * jnp.exp(b_end - bcum)).astype(BF16)
    zero = jnp.zeros_like(qe_bf)
    qe_stack = jnp.concatenate([jnp.where(lane_qk == h, qe_bf, zero) for h in range(N_HEADS)], axis=0)
    a = jnp.where(tril, _dot_nt(qe_stack, ke_bf), 0.0)
    o_full = _dot(a.astype(BF16), v_bf)
    o_intra = jnp.zeros((chunk, HEADS_W), F32)
    for h in range(N_HEADS):
        o_intra = o_intra + jnp.where(lane_v == h, o_full[h * chunk:(h + 1) * chunk, :], 0.0)
    return o_intra, qe_bf, kd_bf, jnp.exp(b_end)


def _head_norm(o, gmat_bf, gn):
    o2 = o * o
    hi = o2.astype(BF16)
    lo = (o2 - hi.astype(F32)).astype(BF16)
    ms = _dot(hi, gmat_bf) + _dot(lo, gmat_bf)
    return o * lax.rsqrt(ms + EPS) * gn


def _pool_delta(ext_ref, pv, pos0, tm):
    ext_ref[HIST_ROWS:HIST_ROWS + tm, :] = pv
    lane_g = lax.broadcasted_iota(jnp.int32, (1, POOL_W), 1) // POOL_GROUP
    pos = pos0 + lax.broadcasted_iota(jnp.int32, (tm, 1), 0)
    run = pv
    sums = jnp.zeros((tm, POOL_W), F32)
    cnt = jnp.zeros((tm, POOL_W), F32)
    for k in range(1, POOL_WINDOWS[-1]):
        run = run + ext_ref[HIST_ROWS - k:HIST_ROWS - k + tm, :]
        if k + 1 in POOL_WINDOWS:
            gi = POOL_WINDOWS.index(k + 1)
            sums = jnp.where(lane_g == gi, run, sums)
            cnt = jnp.where(lane_g == gi, jnp.minimum(pos + 1, k + 1).astype(F32), cnt)
    return sums / cnt - pv


def _glapool_kernel(q_ref, k_ref, g_ref, v_ref, pv_ref, s0_ref, hist0_ref, gn_ref, gmat_ref, pw_ref, ps_ref,
                    og_ref, op_ref, sout_ref, st_ref, ext_ref, *, tm, chunk, pos0):
    t = pl.program_id(1)

    @pl.when(t == 0)
    def _():
        st_ref[...] = s0_ref[...]
        ext_ref[0:HIST_ROWS, :] = hist0_ref[...]

    masks = _gla_masks(chunk)
    rr = lax.broadcasted_iota(jnp.int32, (HEADS_W, QK_W), 0) // HEAD_DIM
    cc = lax.broadcasted_iota(jnp.int32, (HEADS_W, QK_W), 1) // GLA_DK
    bd = rr == cc
    st = st_ref[...]
    outs = []
    for c in range(tm // chunk):
        r = slice(c * chunk, (c + 1) * chunk)
        v_bf = v_ref[0, r, :]
        o_intra, qe_bf, kd_bf, decay = _gla_intra(q_ref[0, r, :], k_ref[0, r, :], g_ref[0, r, :], v_bf, masks, chunk)
        outs.append(o_intra + _dot_nt(qe_bf, st.astype(BF16)))
        st = st * decay + jnp.where(bd, _dot_tn(v_bf, kd_bf), 0.0)
    st_ref[...] = st
    o = outs[0] if len(outs) == 1 else jnp.concatenate(outs, axis=0)
    og_ref[0] = _head_norm(o, gmat_ref[...], gn_ref[...])
    d = _pool_delta(ext_ref, pv_ref[0], pos0 + t * tm, tm)
    op_ref[0] = _dot(d.astype(BF16), pw_ref[...]) * ps_ref[...]
    ext_ref[0:HIST_ROWS, :] = ext_ref[tm:tm + HIST_ROWS, :]

    @pl.when(t == pl.num_programs(1) - 1)
    def _():
        sout_ref[0] = st


def _glapool(x, lw, consts, tm, chunk, pos0, s0, hist0):
    B, T, _ = x["q"].shape
    tile = lambda w: pl.BlockSpec((1, tm, w), lambda b, t: (b, t, 0))
    in_specs = [tile(QK_W), tile(QK_W), tile(QK_W), tile(HEADS_W), tile(POOL_W),
                _const_spec((HEADS_W, QK_W)), _const_spec((HIST_ROWS, POOL_W)), _const_spec((1, HEADS_W)),
                _const_spec((HEADS_W, HEADS_W)), _const_spec((POOL_W, POOL_W)), _const_spec((1, POOL_W))]
    out_shape = [jax.ShapeDtypeStruct((B, T, HEADS_W), F32), jax.ShapeDtypeStruct((B, T, POOL_W), F32),
                 jax.ShapeDtypeStruct((B, HEADS_W, QK_W), F32)]
    out_specs = [tile(HEADS_W), tile(POOL_W), pl.BlockSpec((1, HEADS_W, QK_W), lambda b, t: (b, 0, 0))]
    return pl.pallas_call(
        functools.partial(_glapool_kernel, tm=tm, chunk=chunk, pos0=pos0),
        grid=(B, T // tm), in_specs=in_specs, out_specs=out_specs, out_shape=out_shape,
        scratch_shapes=[pltpu.VMEM((HEADS_W, QK_W), F32), pltpu.VMEM((HIST_ROWS + tm, POOL_W), F32)],
        compiler_params=_cparams(("parallel", "arbitrary"), 48), name="glapool")(
            x["q"], x["k"], x["g"], x["v"], x["pv"], s0, hist0, lw["gla_norm"], consts["gmat"], lw["pool_w"], lw["pool_scale"])


def _fox_features(lf, carry, tri_bf, place_ref):
    r = lf.shape[0]
    lane = lax.broadcasted_iota(jnp.int32, (1, LANES), 1)
    tri = tri_bf[0:r, 0:r]
    hi, mid, lo = _split3(lf)
    f = _dot(tri, hi) + _dot(tri, mid) + _dot(tri, lo) + carry
    fh, fm, fl = _split3(f)
    ones_k = jnp.where((lane >= AUXK_ONES[0]) & (lane < AUXK_ONES[1]), 1.0, 0.0)
    ones_q = jnp.where(lane < AUXK_ONES[0], 1.0, 0.0)
    kaux = ones_k - (_dot(fh, place_ref[0]) + _dot(fm, place_ref[1]) + _dot(fl, place_ref[2]))
    qaux = ones_q + (_dot(fh, place_ref[3]) + _dot(fm, place_ref[4]) + _dot(fl, place_ref[5]))
    return kaux.astype(BF16), qaux.astype(BF16), f[r - 1:r, :]


def _fox_kernel(fq_ref, k_ref, v_ref, aux_ref, tri_ref, place_ref, o_ref,
                kb_ref, vb_ref, kaux_ref, qaux_ref, q2_ref, m_ref, acc_ref, sa_ref, sb_ref, *, T, n_pre, tq):
    t = pl.program_id(1)
    L = n_pre + T

    @pl.when(t == 0)
    def _():
        kb_ref[...] = k_ref[0].astype(BF16)
        vb_ref[...] = v_ref[0].astype(BF16)
        cb = min(PAGE, T)
        blocks = ([(0, n_pre)] if n_pre else []) + [(n_pre + i * cb, cb) for i in range(T // cb)]
        carry = jnp.zeros((1, LANES), F32)
        for r0, r in blocks:
            ka, qa, carry = _fox_features(aux_ref[0, r0:r0 + r, :], carry, tri_ref[...], place_ref)
            kaux_ref[r0:r0 + r, :] = ka
            qaux_ref[r0:r0 + r, :] = qa

    lane = lax.broadcasted_iota(jnp.int32, (1, LANES), 1)
    q0 = pl.multiple_of(n_pre + t * tq, 16)
    qa = qaux_ref[pl.ds(q0, tq), :]
    fq = fq_ref[0]
    zero_bf = jnp.zeros((tq, LANES), BF16)
    for h in range(N_HEADS):
        fq_p = fq[:, (h // 2) * LANES:(h // 2 + 1) * LANES]
        qmask = (lane == h) | (lane == 6 + h) | (lane == 12 + h) | (lane == 18 + h) | (lane == 24 + h) | (lane == 30 + h)
        q2_ref[h] = jnp.concatenate([jnp.where((lane // HEAD_DIM) == (h % 2), fq_p, zero_bf), jnp.where(qmask, qa, zero_bf)], axis=1)
    m_ref[...] = jnp.full(m_ref.shape, NEG, F32)
    acc_ref[...] = jnp.zeros(acc_ref.shape, F32)

    def logits(j, n, s_out):
        r0 = 0 if j is None else pl.multiple_of(n_pre + j * tq, 16)
        kaux = kaux_ref[pl.ds(r0, n), :]
        for p in range(N_HEADS // 2):
            k2 = jnp.concatenate([kb_ref[pl.ds(r0, n), p * LANES:(p + 1) * LANES], kaux], axis=1)
            for h in (2 * p, 2 * p + 1):
                s_out[h] = _dot_nt(q2_ref[h], k2)

    def update(j, n, s_in, causal):
        r0 = 0 if j is None else pl.multiple_of(n_pre + j * tq, 16)
        ones = jnp.ones((n, LANES), BF16)
        if causal:
            keep = lax.broadcasted_iota(jnp.int32, (tq, n), 0) >= lax.broadcasted_iota(jnp.int32, (tq, n), 1)
        for p in range(N_HEADS // 2):
            v2 = jnp.concatenate([vb_ref[pl.ds(r0, n), p * LANES:(p + 1) * LANES], ones], axis=1)
            for h in (2 * p, 2 * p + 1):
                def read(z):
                    s = s_in[h] if j is None else s_in[h + jnp.minimum(z, 0)]
                    return jnp.where(keep, s, NEG) if causal else s
                m_old = m_ref[h]
                m_new = jnp.maximum(m_old, jnp.max(read(t), axis=-1, keepdims=True))
                alpha = jnp.exp(m_old - m_new)
                m_b = m_new[:, :n] if n < LANES else jnp.concatenate([m_new] * (n // LANES), axis=1)
                pe = jnp.exp(read(pl.program_id(0)) - m_b).astype(BF16)
                acc_ref[h] = jnp.concatenate([alpha, alpha], axis=1) * acc_ref[h] + _dot(pe, v2)
                m_ref[h] = m_new

    if n_pre:
        pre = {}
        logits(None, n_pre, pre)
        update(None, n_pre, pre, False)

    logits(0, tq, sa_ref)

    def body(jj, c):
        j = 2 * jj
        logits(j + 1, tq, sb_ref)
        update(j, tq, sa_ref, False)
        logits(j + 2, tq, sa_ref)
        update(j + 1, tq, sb_ref, False)
        return c

    lax.fori_loop(0, t // 2, body, 0)

    @pl.when(t % 2 == 0)
    def _():
        update(t, tq, sa_ref, True)

    @pl.when(t % 2 == 1)
    def _():
        logits(t, tq, sb_ref)
        update(t - 1, tq, sa_ref, False)
        update(t, tq, sb_ref, True)

    for p in range(N_HEADS // 2):
        a0, a1 = acc_ref[2 * p], acc_ref[2 * p + 1]
        o_ref[0, :, p * LANES:(p + 1) * LANES] = jnp.where((lane // HEAD_DIM) == 0, a0[:, :LANES] / a0[:, LANES:],
                                                            a1[:, :LANES] / a1[:, LANES:])


def _fox(x, consts, tq, n_pre):
    B, T, _ = x["fq"].shape
    L = n_pre + T
    whole = lambda w: pl.BlockSpec((1, L, w), lambda b, t: (b, 0, 0))
    in_specs = [pl.BlockSpec((1, tq, HEADS_W), lambda b, t: (b, t, 0)), whole(HEADS_W), whole(HEADS_W), whole(LANES),
                _const_spec((PAGE, PAGE)), _const_spec((6, LANES, LANES))]
    return pl.pallas_call(
        functools.partial(_fox_kernel, T=T, n_pre=n_pre, tq=tq),
        grid=(B, T // tq), in_specs=in_specs,
        out_specs=pl.BlockSpec((1, tq, HEADS_W), lambda b, t: (b, t, 0)),
        out_shape=jax.ShapeDtypeStruct((B, T, HEADS_W), F32),
        scratch_shapes=[pltpu.VMEM((L, HEADS_W), BF16), pltpu.VMEM((L, HEADS_W), BF16),
                        pltpu.VMEM((L, LANES), BF16), pltpu.VMEM((L, LANES), BF16),
                        pltpu.VMEM((N_HEADS, tq, 2 * LANES), BF16), pltpu.VMEM((N_HEADS, tq, LANES), F32),
                        pltpu.VMEM((N_HEADS, tq, 2 * LANES), F32),
                        pltpu.VMEM((N_HEADS, tq, tq), F32), pltpu.VMEM((N_HEADS, tq, tq), F32)],
        compiler_params=_cparams(("parallel", "arbitrary"), 48), name="fox")(
            x["fq"], x["fk"], x["fv"], x["aux"], consts["tri"], consts["place"])


def _outproj_kernel(h_ref, og_ref, op_ref, of_ref, gates_ref, wo_ref, post_ref, out_ref):
    y_in = jnp.concatenate([og_ref[0], op_ref[0], of_ref[0]], axis=1) * _silu(gates_ref[0])
    y = _dot(y_in.astype(BF16), wo_ref[...])
    out_ref[0] = h_ref[0] + y * lax.rsqrt(jnp.mean(y * y, axis=-1, keepdims=True) + EPS) * post_ref[...]


def _outproj(h, og, op, of, gates, lw, tm):
    B, T, _ = h.shape
    tile = lambda w: pl.BlockSpec((1, tm, w), lambda b, t: (b, t, 0))
    return pl.pallas_call(
        _outproj_kernel, grid=(B, T // tm),
        in_specs=[tile(D_MODEL), tile(HEADS_W), tile(POOL_W), tile(HEADS_W), tile(GATES_W),
                  _const_spec((GATES_W, D_MODEL)), _const_spec((1, D_MODEL))],
        out_specs=tile(D_MODEL), out_shape=jax.ShapeDtypeStruct((B, T, D_MODEL), F32),
        compiler_params=_cparams(("parallel", "parallel"), 48), name="outproj")(
            h, og, op, of, gates, lw["w_out"], lw["post"])


def _sample_mix_kernel(q_ref, k_ref, g_ref, v_ref, pv_ref, s_ref, hist_ref, gn_ref, gmat_ref, pw_ref, ps_ref,
                       og_ref, op_ref, sout_ref, ext_ref, *, nseq, ts, pos0):
    masks = _gla_masks(ts)
    rr = lax.broadcasted_iota(jnp.int32, (QK_W, HEADS_W), 0) // GLA_DK
    cc = lax.broadcasted_iota(jnp.int32, (QK_W, HEADS_W), 1) // HEAD_DIM
    bd = rr == cc
    ones = jnp.ones((3 * ts, LANES), BF16)
    outs, deltas = [], []
    for i in range(nseq):
        r = slice(i * ts, (i + 1) * ts)
        g = g_ref[r, :]
        v_bf = v_ref[r, :].astype(BF16)
        o_intra, qe_bf, kd_bf, _ = _gla_intra(q_ref[r, :], k_ref[r, :], g, v_bf, masks, ts)
        s2 = s_ref[i]
        s_bd = jnp.where(bd, jnp.concatenate([s2, s2, s2], axis=1), 0.0).astype(BF16)
        outs.append(o_intra + _dot(qe_bf, s_bd))
        u = jnp.where(bd, _dot_tn(kd_bf, v_bf), 0.0)
        x = jnp.concatenate([u[64 * j:64 * (j + 1), LANES * j:LANES * (j + 1)] for j in range(N_HEADS // 2)], axis=0)
        x = x + pltpu.roll(x, HEAD_DIM, axis=1)
        gh, gm, gl = _split3(g)
        gparts = jnp.concatenate([gh.astype(F32), gm.astype(F32), gl.astype(F32)], axis=0).astype(BF16)
        sout_ref[i] = s2 * jnp.exp(_dot_tn(gparts, ones)) + x
        ext_ref[0:HIST_ROWS, :] = hist_ref[i]
        deltas.append(_pool_delta(ext_ref, pv_ref[r, :], pos0, ts))
    og_ref[...] = _head_norm(jnp.concatenate(outs, axis=0), gmat_ref[...], gn_ref[...])
    op_ref[...] = _dot(jnp.concatenate(deltas, axis=0).astype(BF16), pw_ref[...]) * ps_ref[...]


def _sample_mix(x, state2, hist16, lw, consts, nseq, ts, pos0):
    n = x["q"].shape[0]
    rows = nseq * ts
    tile = lambda w: pl.BlockSpec((rows, w), lambda i: (i, 0))
    in_specs = [tile(QK_W), tile(QK_W), tile(QK_W), tile(HEADS_W), tile(POOL_W),
                pl.BlockSpec((nseq, QK_W, LANES), lambda i: (i, 0, 0)), pl.BlockSpec((nseq, HIST_ROWS, POOL_W), lambda i: (i, 0, 0)),
                _const_spec((1, HEADS_W)), _const_spec((HEADS_W, HEADS_W)), _const_spec((POOL_W, POOL_W)), _const_spec((1, POOL_W))]
    out_shape = [jax.ShapeDtypeStruct((n, HEADS_W), F32), jax.ShapeDtypeStruct((n, POOL_W), F32),
                 jax.ShapeDtypeStruct(state2.shape, F32)]
    out_specs = [tile(HEADS_W), tile(POOL_W), pl.BlockSpec((nseq, QK_W, LANES), lambda i: (i, 0, 0))]
    return pl.pallas_call(
        functools.partial(_sample_mix_kernel, nseq=nseq, ts=ts, pos0=pos0),
        grid=(n // rows,), in_specs=in_specs, out_specs=out_specs, out_shape=out_shape,
        scratch_shapes=[pltpu.VMEM((HIST_ROWS + ts, POOL_W), F32)],
        compiler_params=_cparams(("parallel",), 48), name="sample_mix")(
            x["q"], x["k"], x["g"], x["v"], x["pv"], state2, hist16, lw["gla_norm"], consts["gmat"], lw["pool_w"], lw["pool_scale"])


def _suffix_rows(x, period):
    n = x.shape[0]
    row = lax.broadcasted_iota(jnp.int32, x.shape, 0) % period
    k = 1
    while k < period:
        x = x + jnp.where(row + k < period, pltpu.roll(x, n - k, axis=0), 0.0)
        k *= 2
    return x


def _lfsum_kernel(lf_ref, sw_ref, o_ref):
    n = lf_ref.shape[0]
    hi, mid, lo = _split3(lf_ref[...])
    parts = jnp.concatenate([hi.astype(F32), mid.astype(F32), lo.astype(F32)], axis=0).astype(BF16)
    r3 = _dot(parts, sw_ref[...])
    o_ref[...] = r3[0:n] + r3[n:2 * n] + r3[2 * n:3 * n]


def _lfsum(lf2, sw, tr):
    n = lf2.shape[0]
    return pl.pallas_call(
        _lfsum_kernel, grid=(n // tr,),
        in_specs=[pl.BlockSpec((tr, PAGE), lambda i: (i, 0)), _const_spec((PAGE, 2 * PAGE))],
        out_specs=pl.BlockSpec((tr, 2 * PAGE), lambda i: (i, 0)), out_shape=jax.ShapeDtypeStruct((n, 2 * PAGE), F32),
        compiler_params=_cparams(("parallel",), 32), name="lfsum")(lf2, sw)


def _sample_fox_kernel(pt_ref, fq_ref, kn_ref, vn_ref, an_ref, rt_ref, *rest, n_pages, ts, n_phys):
    k_refs, v_refs = rest[:n_pages], rest[n_pages:2 * n_pages]
    o_ref = rest[2 * n_pages]
    b = pl.program_id(0)
    r2 = jnp.concatenate([rt_ref[pl.ds(h * n_phys + pt_ref[b, p], 1), :]
                          for h in range(N_HEADS) for p in range(n_pages)], axis=0)
    tot = r2[:, PAGE:]
    rfull = r2[:, :PAGE] + (_suffix_rows(tot, n_pages) - tot)
    cn = _cumsum_rows(an_ref[0])
    cn2 = jnp.concatenate([cn, cn], axis=0)
    cparts = _split3(cn)
    lane = lax.broadcasted_iota(jnp.int32, (2 * ts, LANES), 1)
    first = lax.broadcasted_iota(jnp.int32, (2 * ts, LANES), 0) < ts
    own = (lane < HEAD_DIM) == first
    causal = (lax.broadcasted_iota(jnp.int32, (2 * ts, ts), 0) % ts) >= lax.broadcasted_iota(jnp.int32, (2 * ts, ts), 1)
    for pr in range(N_HEADS // 2):
        cols = slice(pr * LANES, (pr + 1) * LANES)
        fq2 = jnp.concatenate([fq_ref[0, :, cols], fq_ref[0, :, cols]], axis=0)
        q2 = jnp.where(own, fq2, 0.0).astype(BF16)
        onehot = lane == jnp.where(first, 2 * pr, 2 * pr + 1)
        cq = jnp.sum(jnp.where(onehot, cn2, 0.0), axis=-1, keepdims=True)
        sel = jnp.where(onehot, 1.0, 0.0).astype(BF16)
        cn_t = _dot_nt(sel, cparts[0]) + _dot_nt(sel, cparts[1]) + _dot_nt(sel, cparts[2])
        s_new = jnp.where(causal, _dot_nt(q2, kn_ref[0, :, cols].astype(BF16)) + cq - cn_t, NEG)
        s_pages = []
        for p in range(n_pages):
            kt = k_refs[p][0, 0, 2 * pr:2 * pr + 2].reshape(LANES, PAGE).astype(BF16)
            r0, r1 = 2 * pr * n_pages + p, (2 * pr + 1) * n_pages + p
            bias = jnp.concatenate([jnp.broadcast_to(rfull[r0:r0 + 1, :], (ts, PAGE)),
                                    jnp.broadcast_to(rfull[r1:r1 + 1, :], (ts, PAGE))], axis=0)
            s_pages.append(_dot(q2, kt) + bias + cq)
        mm = s_pages[0]
        for s in s_pages[1:]:
            mm = jnp.maximum(mm, s)
        m = jnp.maximum(jnp.max(mm, axis=-1, keepdims=True), jnp.max(s_new, axis=-1, keepdims=True))
        p_new = jnp.exp(s_new - m)
        l = jnp.sum(p_new, axis=-1, keepdims=True)
        acc = _dot(p_new.astype(BF16), vn_ref[0, :, cols].astype(BF16))
        for p in range(n_pages):
            pe = jnp.exp(s_pages[p] - m)
            l = l + jnp.sum(pe, axis=-1, keepdims=True)
            vt = v_refs[p][0, 0, 2 * pr:2 * pr + 2].reshape(LANES, PAGE).astype(BF16)
            acc = acc + _dot_nt(pe.astype(BF16), vt)
        o2 = acc / l
        o_ref[0, :, cols] = jnp.where(lane[:ts] < HEAD_DIM, o2[:ts], o2[ts:])


def _sample_fox(layer, page_table, fq, kn, vn, an, rt2, cache_kt, cache_vt):
    nb, ts, _ = fq.shape
    n_pages = page_table.shape[1]
    n_phys = cache_kt.shape[1]
    seq = lambda w: pl.BlockSpec((1, ts, w), lambda b, pt: (b, 0, 0))
    rt_spec = pl.BlockSpec((N_HEADS * n_phys, 2 * PAGE), lambda b, pt: (layer, 0), pipeline_mode=pl.Buffered(1))
    page_specs = [pl.BlockSpec((1, 1, N_HEADS, HEAD_DIM, PAGE), lambda b, pt, p=p: (layer, pt[b, p], 0, 0, 0)) for p in range(n_pages)]
    grid_spec = pltpu.PrefetchScalarGridSpec(
        num_scalar_prefetch=1, grid=(nb,),
        in_specs=[seq(HEADS_W), seq(HEADS_W), seq(HEADS_W), seq(LANES), rt_spec] + page_specs + page_specs,
        out_specs=pl.BlockSpec((1, ts, HEADS_W), lambda b, pt: (b, 0, 0)))
    return pl.pallas_call(
        functools.partial(_sample_fox_kernel, n_pages=n_pages, ts=ts, n_phys=n_phys), grid_spec=grid_spec,
        out_shape=jax.ShapeDtypeStruct((nb, ts, HEADS_W), F32),
        compiler_params=_cparams(("arbitrary",), 56), name="sample_fox")(
            page_table, fq, kn, vn, an, rt2, *([cache_kt] * n_pages), *([cache_vt] * n_pages))


def _wprep_kernel(w_ref, o_ref, *, depth):
    per_col = (D_MODEL // LANES) * depth
    for l in range(depth):
        for rt in range(D_MODEL // LANES):
            blk = w_ref[pl.ds(rt * depth + l, LANES, stride=per_col), :]
            o_ref[l, rt * LANES:(rt + 1) * LANES, :] = blk.T.astype(BF16)


def _wprep(w_in):
    depth, d, c = w_in.shape
    per_col = (d // LANES) * depth
    cp = -(-c // LANES) * LANES
    w2 = jnp.transpose(w_in, (2, 0, 1)).reshape(c, depth, d // LANES, LANES)
    w2 = jnp.transpose(w2, (0, 2, 1, 3)).reshape(c * per_col, LANES)
    w2 = jnp.pad(w2, ((0, (cp - c) * per_col), (0, 0)))
    return pl.pallas_call(
        functools.partial(_wprep_kernel, depth=depth), grid=(cp // LANES,),
        in_specs=[pl.BlockSpec((LANES * per_col, LANES), lambda j: (j, 0))],
        out_specs=pl.BlockSpec((depth, d, LANES), lambda j: (0, 0, j)),
        out_shape=jax.ShapeDtypeStruct((depth, d, cp), BF16),
        compiler_params=_cparams(("parallel",), 32), name="wprep")(w2)


def _layer_weights(l, norm_pre, norm_post, w_rm, gla_w_up, gla_b_gate, gla_norm, pool_w, pool_scale, fox_b_f, w_out):
    idx = [sum(SPLIT_SIZES[:i + 1]) for i in range(len(SPLIT_SIZES) - 1)]
    gq, gk, gv, glr, gg, pv, pg, fq, fk, fv, ff, fg = jnp.split(w_rm[l, :, :sum(SPLIT_SIZES)], idx, axis=1)
    zc = lambda n: jnp.zeros((D_MODEL, n), BF16)
    aux = jnp.concatenate([ff, zc(AUX_LR - N_HEADS), glr, zc(LANES - AUX_LR - GLA_LR)], axis=1)
    w = jnp.concatenate([gq, zc(COL_K - QK_W), gk, zc(COL_GV - COL_K - QK_W), gv, gg, pg, fg, pv, fq, fk, fv, aux], axis=1)
    w_up = jnp.zeros((LANES, QK_W), F32).at[AUX_LR:AUX_LR + GLA_LR].set(gla_w_up[l])
    bf = jnp.zeros((1, LANES), F32).at[0, AUX_FF:AUX_FF + N_HEADS].set(fox_b_f[l])
    pw = jnp.zeros((POOL_W, POOL_W), F32)
    for gi in range(len(POOL_WINDOWS)):
        sl = slice(gi * POOL_GROUP, (gi + 1) * POOL_GROUP)
        pw = pw.at[sl, sl].set(pool_w[l, gi])
    return {"pre": norm_pre[l][None], "post": norm_post[l][None], "w_in": w, "w_up": w_up.astype(BF16),
            "b_gate": gla_b_gate[l][None], "bf": bf, "gla_norm": jnp.tile(gla_norm[l], N_HEADS)[None],
            "pool_w": pw.astype(BF16), "pool_scale": pool_scale[l][None], "w_out": w_out[l].astype(BF16)}


def _constants():
    i = jnp.arange(HEADS_W)
    gmat = jnp.where((i[:, None] // HEAD_DIM) == (i[None, :] // HEAD_DIM), 1.0 / HEAD_DIM, 0.0).astype(BF16)
    r = jnp.arange(PAGE)
    tri = (r[:, None] >= r[None, :]).astype(BF16)
    a = jnp.arange(LANES)
    place = jnp.stack([((a[:, None] < N_HEADS) & (a[None, :] == a[:, None] + off)).astype(BF16)
                       for off in (0, 6, 12, 18, 24, 30)])
    sw = jnp.concatenate([(r[:, None] > r[None, :]).astype(BF16), jnp.ones((PAGE, PAGE), BF16)], axis=1)
    return {"gmat": gmat, "tri": tri, "place": place, "sw": sw}


def _state_from_bd(st):
    B = st.shape[0]
    s = st.reshape(B, N_HEADS, HEAD_DIM, N_HEADS, GLA_DK)
    s = jnp.stack([s[:, h, :, h, :] for h in range(N_HEADS)], axis=1)
    return jnp.swapaxes(s, 2, 3)


def _sequence_layer(h, lw, consts, prefix, tm, tq, chunk):
    n_pre = 0 if prefix is None else prefix["k"].shape[0]
    x = _inproj(h, lw, tm, prefix)
    s0 = jnp.zeros((HEADS_W, QK_W), F32) if prefix is None else prefix["state"]
    hist0 = jnp.zeros((HIST_ROWS, POOL_W), F32) if prefix is None else prefix["hist"]
    og, op, st = _glapool(x, lw, consts, tm, chunk, n_pre, s0, hist0)
    of = _fox(x, consts, tq, n_pre)
    h_new = _outproj(h, og, op, of, x["gates"], lw, tm)
    return h_new, x, st


def _sample_layer(l, h, lw, consts, page_table, rt5, cache_kt, cache_vt, state2, hist16, nb, ts, tm, nseq):
    past = page_table.shape[1] * PAGE
    x = _inproj(h, lw, tm)
    flat = {n: x[n][0] for n in ("q", "k", "g", "pv")}
    flat["v"] = x["v"][0].astype(F32)
    og, op, s_new = _sample_mix(flat, state2, hist16, lw, consts, nseq, ts, past)
    per_seq = lambda a, w: a.reshape(nb, ts, w)
    of = _sample_fox(l, page_table, per_seq(x["fq"][0].astype(F32), HEADS_W), per_seq(x["fk"][0], HEADS_W),
                     per_seq(x["fv"][0], HEADS_W), per_seq(x["aux"][0], LANES), rt5, cache_kt, cache_vt)
    h_new = _outproj(h, og[None], op[None], of.reshape(1, nb * ts, HEADS_W), x["gates"], lw, tm)
    return h_new, x, s_new


def _largest_tile(n, cap):
    t = min(n, cap)
    while n % t:
        t -= 8
    return t


def kernel(x_prompt, x_sample, cache_fox_k, cache_fox_v, cache_fox_logf, state_gla, state_pool, page_table,
           meta_tokens, norm_pre, norm_post, w_in, gla_w_up, gla_b_gate, gla_norm, pool_w, pool_scale,
           fox_b_f, w_out):
    B, T, _ = x_prompt.shape
    nb, ts, _ = x_sample.shape
    depth, n_phys = cache_fox_k.shape[:2]
    consts = _constants()
    cache_kt = jnp.transpose(cache_fox_k, (0, 1, 3, 4, 2))
    cache_vt = jnp.transpose(cache_fox_v, (0, 1, 3, 4, 2))
    lf2 = jnp.transpose(cache_fox_logf, (0, 3, 1, 2)).reshape(depth * N_HEADS * n_phys, PAGE)
    rt5 = _lfsum(lf2, consts["sw"], _largest_tile(depth * N_HEADS * n_phys, 512))
    state2 = state_gla.reshape(depth, nb, QK_W, HEAD_DIM)
    state2 = jnp.concatenate([state2, state2], axis=-1)
    hist16 = jnp.pad(state_pool, ((0, 0), (0, 0), (HIST_ROWS - POOL_HIST, 0), (0, 0)))
    tm_p, tm_s = _largest_tile(T, 256), _largest_tile(nb * ts, 512)
    w_rm = _wprep(w_in)

    h_m, h_p, h_s = meta_tokens[None], x_prompt, x_sample.reshape(1, nb * ts, D_MODEL)
    out = [[] for _ in range(10)]
    for l in range(depth):
        lw = _layer_weights(l, norm_pre, norm_post, w_rm, gla_w_up, gla_b_gate, gla_norm, pool_w, pool_scale, fox_b_f, w_out)
        h_m_new, xm, st_m = _sequence_layer(h_m, lw, consts, None, N_META, N_META, N_META)
        prefix = {"k": xm["fk"][0], "v": xm["fv"][0], "aux": xm["aux"][0], "state": st_m[0], "hist": xm["pv"][0]}
        h_p, xp, st_p = _sequence_layer(h_p, lw, consts, prefix, tm_p, tm_p, GLA_CHUNK)
        h_m = h_m_new
        h_s, xs, s_new = _sample_layer(l, h_s, lw, consts, page_table, rt5, cache_kt, cache_vt,
                                       state2[l], hist16[l], nb, ts, tm_s, 8)
        L = N_META + T
        out[0].append(xp["fk"].reshape(B, L, N_HEADS, HEAD_DIM))
        out[1].append(xp["fv"].reshape(B, L, N_HEADS, HEAD_DIM))
        out[2].append(xp["aux"][:, :, :N_HEADS])
        out[3].append(_state_from_bd(st_p))
        out[4].append(xp["pv"][:, T - POOL_HIST:])
        out[5].append(xs["fk"].reshape(nb, ts, N_HEADS, HEAD_DIM))
        out[6].append(xs["fv"].reshape(nb, ts, N_HEADS, HEAD_DIM))
        out[7].append(xs["aux"][0, :, :N_HEADS].reshape(nb, ts, N_HEADS))
        out[8].append(s_new[:, :, :HEAD_DIM].reshape(nb, N_HEADS, GLA_DK, HEAD_DIM))
        out[9].append(jnp.concatenate([state_pool[l], xs["pv"][0].reshape(nb, ts, POOL_W)], axis=1)[:, ts:])
    return (h_p, h_s.reshape(nb, ts, D_MODEL)) + tuple(jnp.stack(o) for o in out)
```

```python
import functools

import jax
import jax.numpy as jnp
import numpy as np
from jax import lax
from jax.experimental import pallas as pl
from jax.experimental.pallas import tpu as pltpu

F32 = jnp.float32
BF16 = jnp.bfloat16

D_MODEL = 1024
N_HEADS = 6
GLA_DK = 32
HEAD_DIM = 64
GLA_LR = 16
GLA_GATE_NORM = 16.0
GLA_CHUNK = 64
QK_W = N_HEADS * GLA_DK
HEADS_W = N_HEADS * HEAD_DIM
POOL_W = 256
POOL_GROUP = 64
POOL_WINDOWS = (2, 4, 8, 16)
POOL_HIST = 15
HIST_ROWS = 16
N_META = 16
PAGE = 128
EPS = 1e-6
LANES = 128
NEG = -1e30

COL_Q, COL_K, COL_GV, COL_GATES, COL_PV = 0, 256, 512, 896, 1920
COL_FQ, COL_FK, COL_FV, COL_AUX, W_IN_COLS = 2176, 2560, 2944, 3328, 3456
GATES_W = HEADS_W + POOL_W + HEADS_W
AUX_FF, AUX_LR = 0, 8
SPLIT_SIZES = (QK_W, QK_W, HEADS_W, GLA_LR, HEADS_W, POOL_W, POOL_W, HEADS_W, HEADS_W, HEADS_W, N_HEADS, HEADS_W)

AUXK_ONES = (18, 36)


def _cparams(sem, vmem_mb):
    return pltpu.CompilerParams(dimension_semantics=sem, vmem_limit_bytes=vmem_mb * 1024 * 1024)


def _const_spec(shape):
    return pl.BlockSpec(shape, lambda *_: (0,) * len(shape))


def _dot(a, b):
    return jnp.dot(a, b, preferred_element_type=F32)


def _dot_nt(a, b):
    return lax.dot_general(a, b, (((1,), (1,)), ((), ())), preferred_element_type=F32)


def _dot_tn(a, b):
    return lax.dot_general(a, b, (((0,), (0,)), ((), ())), preferred_element_type=F32)


def _log_sigmoid(x):
    return jnp.minimum(x, 0.0) - jnp.log1p(jnp.exp(-jnp.abs(x)))


def _silu(x):
    return x * (1.0 / (1.0 + jnp.exp(-x)))


def _split3(x):
    hi = x.astype(BF16)
    r = x - hi.astype(F32)
    mid = r.astype(BF16)
    lo = (r - mid.astype(F32)).astype(BF16)
    return hi, mid, lo


def _cumsum_rows(x):
    n = x.shape[0]
    row = lax.broadcasted_iota(jnp.int32, x.shape, 0)
    k = 1
    while k < n:
        x = x + jnp.where(row >= k, pltpu.roll(x, k, axis=0), 0.0)
        k *= 2
    return x


def _inproj_kernel(*refs, tm, n_pre):
    if n_pre:
        (x_ref, pre_ref, w_ref, wup_ref, bg_ref, bf_ref, kpre_ref, vpre_ref, apre_ref,
         q_ref, k_ref, g_ref, v_ref, gates_ref, pv_ref, fq_ref, fk_ref, fv_ref, aux_ref) = refs
    else:
        (x_ref, pre_ref, w_ref, wup_ref, bg_ref, bf_ref,
         q_ref, k_ref, g_ref, v_ref, gates_ref, pv_ref, fq_ref, fk_ref, fv_ref, aux_ref) = refs
    t = pl.program_id(1)
    x = x_ref[0]
    xn = x * lax.rsqrt(jnp.mean(x * x, axis=-1, keepdims=True) + EPS) * pre_ref[...]
    xb = xn.astype(BF16)

    def seg(c0, width):
        return _dot(xb, w_ref[:, c0:c0 + width])

    q_ref[0] = seg(COL_Q, QK_W) * (GLA_DK ** -0.5)
    k_ref[0] = seg(COL_K, QK_W)
    v_ref[0] = seg(COL_GV, HEADS_W).astype(BF16)
    gates_ref[0] = seg(COL_GATES, GATES_W)
    pv_ref[0] = seg(COL_PV, POOL_W)
    fq_ref[0] = (seg(COL_FQ, HEADS_W) * (HEAD_DIM ** -0.5)).astype(BF16)
    aux = seg(COL_AUX, LANES)
    g_ref[0] = _log_sigmoid(_dot(aux.astype(BF16), wup_ref[...]) + bg_ref[...]) * (1.0 / GLA_GATE_NORM)
    rows = pl.ds(pl.multiple_of(n_pre + t * tm, 8), tm)
    fk_ref[0, rows, :] = seg(COL_FK, HEADS_W)
    fv_ref[0, rows, :] = seg(COL_FV, HEADS_W)
    aux_ref[0, rows, :] = _log_sigmoid(aux + bf_ref[...])
    if n_pre:
        @pl.when(t == 0)
        def _():
            fk_ref[0, 0:n_pre, :] = kpre_ref[...]
            fv_ref[0, 0:n_pre, :] = vpre_ref[...]
            aux_ref[0, 0:n_pre, :] = apre_ref[...]


def _inproj(h, lw, tm, prefix=None):
    B, T, _ = h.shape
    n_pre = 0 if prefix is None else prefix["k"].shape[0]
    L = n_pre + T
    tile = lambda w: pl.BlockSpec((1, tm, w), lambda b, t: (b, t, 0))
    whole = lambda w: pl.BlockSpec((1, L, w), lambda b, t: (b, 0, 0))
    in_specs = [tile(D_MODEL), _const_spec((1, D_MODEL)), _const_spec((D_MODEL, W_IN_COLS)),
                _const_spec((LANES, QK_W)), _const_spec((1, QK_W)), _const_spec((1, LANES))]
    args = [h, lw["pre"], lw["w_in"], lw["w_up"], lw["b_gate"], lw["bf"]]
    if n_pre:
        in_specs += [_const_spec((n_pre, HEADS_W)), _const_spec((n_pre, HEADS_W)), _const_spec((n_pre, LANES))]
        args += [prefix["k"], prefix["v"], prefix["aux"]]
    out_shape = [jax.ShapeDtypeStruct((B, T, QK_W), F32), jax.ShapeDtypeStruct((B, T, QK_W), F32),
                 jax.ShapeDtypeStruct((B, T, QK_W), F32), jax.ShapeDtypeStruct((B, T, HEADS_W), BF16),
                 jax.ShapeDtypeStruct((B, T, GATES_W), F32), jax.ShapeDtypeStruct((B, T, POOL_W), F32),
                 jax.ShapeDtypeStruct((B, T, HEADS_W), BF16), jax.ShapeDtypeStruct((B, L, HEADS_W), F32),
                 jax.ShapeDtypeStruct((B, L, HEADS_W), F32), jax.ShapeDtypeStruct((B, L, LANES), F32)]
    out_specs = [tile(QK_W), tile(QK_W), tile(QK_W), tile(HEADS_W), tile(GATES_W), tile(POOL_W),
                 tile(HEADS_W), whole(HEADS_W), whole(HEADS_W), whole(LANES)]
    names = ("q", "k", "g", "v", "gates", "pv", "fq", "fk", "fv", "aux")
    outs = pl.pallas_call(
        functools.partial(_inproj_kernel, tm=tm, n_pre=n_pre),
        grid=(B, T // tm), in_specs=in_specs, out_specs=out_specs, out_shape=out_shape,
        compiler_params=_cparams(("parallel", "arbitrary"), 56), name="inproj")(*args)
    return dict(zip(names, outs))


def _gla_masks(chunk):
    lane_qk = lax.broadcasted_iota(jnp.int32, (1, QK_W), 1) // GLA_DK
    lane_v = lax.broadcasted_iota(jnp.int32, (1, HEADS_W), 1) // HEAD_DIM
    r = lax.broadcasted_iota(jnp.int32, (N_HEADS * chunk, chunk), 0)
    c = lax.broadcasted_iota(jnp.int32, (N_HEADS * chunk, chunk), 1)
    tril = (r % chunk) >= c
    return lane_qk, lane_v, tril


def _gla_intra(q, k, g, v_bf, masks, chunk):
    lane_qk, lane_v, tril = masks
    bcum = _cumsum_rows(g)
    qe_bf = (q * jnp.exp(bcum)).astype(BF16)
    ke_bf = (k * jnp.exp(-bcum)).astype(BF16)
    b_end = bcum[chunk - 1:chunk, :]
    kd_bf = (k * jnp.exp(b_end - bcum)).astype(BF16)
    zero = jnp.zeros_like(qe_bf)
    qe_stack = jnp.concatenate([jnp.where(lane_qk == h, qe_bf, zero) for h in range(N_HEADS)], axis=0)
    a = jnp.where(tril, _dot_nt(qe_stack, ke_bf), 0.0)
    o_full = _dot(a.astype(BF16), v_bf)
    o_intra = jnp.zeros((chunk, HEADS_W), F32)
    for h in range(N_HEADS):
        o_intra = o_intra + jnp.where(lane_v == h, o_full[h * chunk:(h + 1) * chunk, :], 0.0)
    return o_intra, qe_bf, kd_bf, jnp.exp(b_end)


def _head_norm(o, gmat_bf, gn):
    o2 = o * o
    hi = o2.astype(BF16)
    lo = (o2 - hi.astype(F32)).astype(BF16)
    ms = _dot(hi, gmat_bf) + _dot(lo, gmat_bf)
    return o * lax.rsqrt(ms + EPS) * gn


def _pool_delta(ext_ref, pv, pos0, tm):
    ext_ref[HIST_ROWS:HIST_ROWS + tm, :] = pv
    lane_g = lax.broadcasted_iota(jnp.int32, (1, POOL_W), 1) // POOL_GROUP
    pos = pos0 + lax.broadcasted_iota(jnp.int32, (tm, 1), 0)
    run = pv
    sums = jnp.zeros((tm, POOL_W), F32)
    cnt = jnp.zeros((tm, POOL_W), F32)
    for k in range(1, POOL_WINDOWS[-1]):
        run = run + ext_ref[HIST_ROWS - k:HIST_ROWS - k + tm, :]
        if k + 1 in POOL_WINDOWS:
            gi = POOL_WINDOWS.index(k + 1)
            sums = jnp.where(lane_g == gi, run, sums)
            cnt = jnp.where(lane_g == gi, jnp.minimum(pos + 1, k + 1).astype(F32), cnt)
    return sums / cnt - pv


def _glapool_kernel(q_ref, k_ref, g_ref, v_ref, pv_ref, s0_ref, hist0_ref, gn_ref, gmat_ref, pw_ref, ps_ref,
                    og_ref, op_ref, sout_ref, st_ref, ext_ref, *, tm, chunk, pos0):
    t = pl.program_id(1)

    @pl.when(t == 0)
    def _():
        st_ref[...] = s0_ref[...]
        ext_ref[0:HIST_ROWS, :] = hist0_ref[...]

    masks = _gla_masks(chunk)
    rr = lax.broadcasted_iota(jnp.int32, (HEADS_W, QK_W), 0) // HEAD_DIM
    cc = lax.broadcasted_iota(jnp.int32, (HEADS_W, QK_W), 1) // GLA_DK
    bd = rr == cc
    st = st_ref[...]
    outs = []
    for c in range(tm // chunk):
        r = slice(c * chunk, (c + 1) * chunk)
        v_bf = v_ref[0, r, :]
        o_intra, qe_bf, kd_bf, decay = _gla_intra(q_ref[0, r, :], k_ref[0, r, :], g_ref[0, r, :], v_bf, masks, chunk)
        outs.append(o_intra + _dot_nt(qe_bf, st.astype(BF16)))
        st = st * decay + jnp.where(bd, _dot_tn(v_bf, kd_bf), 0.0)
    st_ref[...] = st
    o = outs[0] if len(outs) == 1 else jnp.concatenate(outs, axis=0)
    og_ref[0] = _head_norm(o, gmat_ref[...], gn_ref[...])
    d = _pool_delta(ext_ref, pv_ref[0], pos0 + t * tm, tm)
    op_ref[0] = _dot(d.astype(BF16), pw_ref[...]) * ps_ref[...]
    ext_ref[0:HIST_ROWS, :] = ext_ref[tm:tm + HIST_ROWS, :]

    @pl.when(t == pl.num_programs(1) - 1)
    def _():
        sout_ref[0] = st


def _glapool(x, lw, consts, tm, chunk, pos0, s0, hist0):
    B, T, _ = x["q"].shape
    tile = lambda w: pl.BlockSpec((1, tm, w), lambda b, t: (b, t, 0))
    in_specs = [tile(QK_W), tile(QK_W), tile(QK_W), tile(HEADS_W), tile(POOL_W),
                _const_spec((HEADS_W, QK_W)), _const_spec((HIST_ROWS, POOL_W)), _const_spec((1, HEADS_W)),
                _const_spec((HEADS_W, HEADS_W)), _const_spec((POOL_W, POOL_W)), _const_spec((1, POOL_W))]
    out_shape = [jax.ShapeDtypeStruct((B, T, HEADS_W), F32), jax.ShapeDtypeStruct((B, T, POOL_W), F32),
                 jax.ShapeDtypeStruct((B, HEADS_W, QK_W), F32)]
    out_specs = [tile(HEADS_W), tile(POOL_W), pl.BlockSpec((1, HEADS_W, QK_W), lambda b, t: (b, 0, 0))]
    return pl.pallas_call(
        functools.partial(_glapool_kernel, tm=tm, chunk=chunk, pos0=pos0),
        grid=(B, T // tm), in_specs=in_specs, out_specs=out_specs, out_shape=out_shape,
        scratch_shapes=[pltpu.VMEM((HEADS_W, QK_W), F32), pltpu.VMEM((HIST_ROWS + tm, POOL_W), F32)],
        compiler_params=_cparams(("parallel", "arbitrary"), 48), name="glapool")(
            x["q"], x["k"], x["g"], x["v"], x["pv"], s0, hist0, lw["gla_norm"], consts["gmat"], lw["pool_w"], lw["pool_scale"])


def _fox_features(lf, carry, tri_bf, place_ref):
    r = lf.shape[0]
    lane = lax.broadcasted_iota(jnp.int32, (1, LANES), 1)
    tri = tri_bf[0:r, 0:r]
    hi, mid, lo = _split3(lf)
    f = _dot(tri, hi) + _dot(tri, mid) + _dot(tri, lo) + carry
    fh, fm, fl = _split3(f)
    ones_k = jnp.where((lane >= AUXK_ONES[0]) & (lane < AUXK_ONES[1]), 1.0, 0.0)
    ones_q = jnp.where(lane < AUXK_ONES[0], 1.0, 0.0)
    kaux = ones_k - (_dot(fh, place_ref[0]) + _dot(fm, place_ref[1]) + _dot(fl, place_ref[2]))
    qaux = ones_q + (_dot(fh, place_ref[3]) + _dot(fm, place_ref[4]) + _dot(fl, place_ref[5]))
    return kaux.astype(BF16), qaux.astype(BF16), f[r - 1:r, :]


def _fox_kernel(fq_ref, k_ref, v_ref, aux_ref, tri_ref, place_ref, o_ref,
                kb_ref, vb_ref, kaux_ref, qaux_ref, q2_ref, m_ref, acc_ref, sa_ref, sb_ref, *, T, n_pre, tq):
    t = pl.program_id(1)
    L = n_pre + T

    @pl.when(t == 0)
    def _():
        kb_ref[...] = k_ref[0].astype(BF16)
        vb_ref[...] = v_ref[0].astype(BF16)
        cb = min(PAGE, T)
        blocks = ([(0, n_pre)] if n_pre else []) + [(n_pre + i * cb, cb) for i in range(T // cb)]
        carry = jnp.zeros((1, LANES), F32)
        for r0, r in blocks:
            ka, qa, carry = _fox_features(aux_ref[0, r0:r0 + r, :], carry, tri_ref[...], place_ref)
            kaux_ref[r0:r0 + r, :] = ka
            qaux_ref[r0:r0 + r, :] = qa

    lane = lax.broadcasted_iota(jnp.int32, (1, LANES), 1)
    q0 = pl.multiple_of(n_pre + t * tq, 16)
    qa = qaux_ref[pl.ds(q0, tq), :]
    fq = fq_ref[0]
    zero_bf = jnp.zeros((tq, LANES), BF16)
    for h in range(N_HEADS):
        fq_p = fq[:, (h // 2) * LANES:(h // 2 + 1) * LANES]
        qmask = (lane == h) | (lane == 6 + h) | (lane == 12 + h) | (lane == 18 + h) | (lane == 24 + h) | (lane == 30 + h)
        q2_ref[h] = jnp.concatenate([jnp.where((lane // HEAD_DIM) == (h % 2), fq_p, zero_bf), jnp.where(qmask, qa, zero_bf)], axis=1)
    m_ref[...] = jnp.full(m_ref.shape, NEG, F32)
    acc_ref[...] = jnp.zeros(acc_ref.shape, F32)

    def logits(j, n, s_out):
        r0 = 0 if j is None else pl.multiple_of(n_pre + j * tq, 16)
        kaux = kaux_ref[pl.ds(r0, n), :]
        for p in range(N_HEADS // 2):
            k2 = jnp.concatenate([kb_ref[pl.ds(r0, n), p * LANES:(p + 1) * LANES], kaux], axis=1)
            for h in (2 * p, 2 * p + 1):
                s_out[h] = _dot_nt(q2_ref[h], k2)

    def update(j, n, s_in, causal):
        r0 = 0 if j is None else pl.multiple_of(n_pre + j * tq, 16)
        ones = jnp.ones((n, LANES), BF16)
        if causal:
            keep = lax.broadcasted_iota(jnp.int32, (tq, n), 0) >= lax.broadcasted_iota(jnp.int32, (tq, n), 1)
        for p in range(N_HEADS // 2):
            v2 = jnp.concatenate([vb_ref[pl.ds(r0, n), p * LANES:(p + 1) * LANES], ones], axis=1)
            for h in (2 * p, 2 * p + 1):
                def read(z):
                    s = s_in[h] if j is None else s_in[h + jnp.minimum(z, 0)]
                    return jnp.where(keep, s, NEG) if causal else s
                m_old = m_ref[h]
                m_new = jnp.maximum(m_old, jnp.max(read(t), axis=-1, keepdims=True))
                alpha = jnp.exp(m_old - m_new)
                m_b = m_new[:, :n] if n < LANES else jnp.concatenate([m_new] * (n // LANES), axis=1)
                pe = jnp.exp(read(pl.program_id(0)) - m_b).astype(BF16)
                acc_ref[h] = jnp.concatenate([alpha, alpha], axis=1) * acc_ref[h] + _dot(pe, v2)
                m_ref[h] = m_new

    if n_pre:
        pre = {}
        logits(None, n_pre, pre)
        update(None, n_pre, pre, False)

    logits(0, tq, sa_ref)

    def body(jj, c):
        j = 2 * jj
        logits(j + 1, tq, sb_ref)
        update(j, tq, sa_ref, False)
        logits(j + 2, tq, sa_ref)
        update(j + 1, tq, sb_ref, False)
        return c

    lax.fori_loop(0, t // 2, body, 0)

    @pl.when(t % 2 == 0)
    def _():
        update(t, tq, sa_ref, True)

    @pl.when(t % 2 == 1)
    def _():
        logits(t, tq, sb_ref)
        update(t - 1, tq, sa_ref, False)
        update(t, tq, sb_ref, True)

    for p in range(N_HEADS // 2):
        a0, a1 = acc_ref[2 * p], acc_ref[2 * p + 1]
        o_ref[0, :, p * LANES:(p + 1) * LANES] = jnp.where((lane // HEAD_DIM) == 0, a0[:, :LANES] / a0[:, LANES:],
                                                            a1[:, :LANES] / a1[:, LANES:])


def _fox(x, consts, tq, n_pre):
    B, T, _ = x["fq"].shape
    L = n_pre + T
    whole = lambda w: pl.BlockSpec((1, L, w), lambda b, t: (b, 0, 0))
    in_specs = [pl.BlockSpec((1, tq, HEADS_W), lambda b, t: (b, t, 0)), whole(HEADS_W), whole(HEADS_W), whole(LANES),
                _const_spec((PAGE, PAGE)), _const_spec((6, LANES, LANES))]
    return pl.pallas_call(
        functools.partial(_fox_kernel, T=T, n_pre=n_pre, tq=tq),
        grid=(B, T // tq), in_specs=in_specs,
        out_specs=pl.BlockSpec((1, tq, HEADS_W), lambda b, t: (b, t, 0)),
        out_shape=jax.ShapeDtypeStruct((B, T, HEADS_W), F32),
        scratch_shapes=[pltpu.VMEM((L, HEADS_W), BF16), pltpu.VMEM((L, HEADS_W), BF16),
                        pltpu.VMEM((L, LANES), BF16), pltpu.VMEM((L, LANES), BF16),
                        pltpu.VMEM((N_HEADS, tq, 2 * LANES), BF16), pltpu.VMEM((N_HEADS, tq, LANES), F32),
                        pltpu.VMEM((N_HEADS, tq, 2 * LANES), F32),
                        pltpu.VMEM((N_HEADS, tq, tq), F32), pltpu.VMEM((N_HEADS, tq, tq), F32)],
        compiler_params=_cparams(("parallel", "arbitrary"), 48), name="fox")(
            x["fq"], x["fk"], x["fv"], x["aux"], consts["tri"], consts["place"])


def _outproj_kernel(h_ref, og_ref, op_ref, of_ref, gates_ref, wo_ref, post_ref, out_ref):
    y_in = jnp.concatenate([og_ref[0], op_ref[0], of_ref[0]], axis=1) * _silu(gates_ref[0])
    y = _dot(y_in.astype(BF16), wo_ref[...])
    out_ref[0] = h_ref[0] + y * lax.rsqrt(jnp.mean(y * y, axis=-1, keepdims=True) + EPS) * post_ref[...]


def _outproj(h, og, op, of, gates, lw, tm):
    B, T, _ = h.shape
    tile = lambda w: pl.BlockSpec((1, tm, w), lambda b, t: (b, t, 0))
    return pl.pallas_call(
        _outproj_kernel, grid=(B, T // tm),
        in_specs=[tile(D_MODEL), tile(HEADS_W), tile(POOL_W), tile(HEADS_W), tile(GATES_W),
                  _const_spec((GATES_W, D_MODEL)), _const_spec((1, D_MODEL))],
        out_specs=tile(D_MODEL), out_shape=jax.ShapeDtypeStruct((B, T, D_MODEL), F32),
        compiler_params=_cparams(("parallel", "parallel"), 48), name="outproj")(
            h, og, op, of, gates, lw["w_out"], lw["post"])


def _sample_mix_kernel(q_ref, k_ref, g_ref, v_ref, pv_ref, s_ref, hist_ref, gn_ref, gmat_ref, pw_ref, ps_ref,
                       og_ref, op_ref, sout_ref, ext_ref, *, nseq, ts, pos0):
    masks = _gla_masks(ts)
    rr = lax.broadcasted_iota(jnp.int32, (QK_W, HEADS_W), 0) // GLA_DK
    cc = lax.broadcasted_iota(jnp.int32, (QK_W, HEADS_W), 1) // HEAD_DIM
    bd = rr == cc
    ones = jnp.ones((3 * ts, LANES), BF16)
    outs, deltas = [], []
    for i in range(nseq):
        r = slice(i * ts, (i + 1) * ts)
        g = g_ref[r, :]
        v_bf = v_ref[r, :].astype(BF16)
        o_intra, qe_bf, kd_bf, _ = _gla_intra(q_ref[r, :], k_ref[r, :], g, v_bf, masks, ts)
        s2 = s_ref[i]
        s_bd = jnp.where(bd, jnp.concatenate([s2, s2, s2], axis=1), 0.0).astype(BF16)
        outs.append(o_intra + _dot(qe_bf, s_bd))
        u = jnp.where(bd, _dot_tn(kd_bf, v_bf), 0.0)
        x = jnp.concatenate([u[64 * j:64 * (j + 1), LANES * j:LANES * (j + 1)] for j in range(N_HEADS // 2)], axis=0)
        x = x + pltpu.roll(x, HEAD_DIM, axis=1)
        gh, gm, gl = _split3(g)
        gparts = jnp.concatenate([gh.astype(F32), gm.astype(F32), gl.astype(F32)], axis=0).astype(BF16)
        sout_ref[i] = s2 * jnp.exp(_dot_tn(gparts, ones)) + x
        ext_ref[0:HIST_ROWS, :] = hist_ref[i]
        deltas.append(_pool_delta(ext_ref, pv_ref[r, :], pos0, ts))
    og_ref[...] = _head_norm(jnp.concatenate(outs, axis=0), gmat_ref[...], gn_ref[...])
    op_ref[...] = _dot(jnp.concatenate(deltas, axis=0).astype(BF16), pw_ref[...]) * ps_ref[...]


def _sample_mix(x, state2, hist16, lw, consts, nseq, ts, pos0):
    n = x["q"].shape[0]
    rows = nseq * ts
    tile = lambda w: pl.BlockSpec((rows, w), lambda i: (i, 0))
    in_specs = [tile(QK_W), tile(QK_W), tile(QK_W), tile(HEADS_W), tile(POOL_W),
                pl.BlockSpec((nseq, QK_W, LANES), lambda i: (i, 0, 0)), pl.BlockSpec((nseq, HIST_ROWS, POOL_W), lambda i: (i, 0, 0)),
                _const_spec((1, HEADS_W)), _const_spec((HEADS_W, HEADS_W)), _const_spec((POOL_W, POOL_W)), _const_spec((1, POOL_W))]
    out_shape = [jax.ShapeDtypeStruct((n, HEADS_W), F32), jax.ShapeDtypeStruct((n, POOL_W), F32),
                 jax.ShapeDtypeStruct(state2.shape, F32)]
    out_specs = [tile(HEADS_W), tile(POOL_W), pl.BlockSpec((nseq, QK_W, LANES), lambda i: (i, 0, 0))]
    return pl.pallas_call(
        functools.partial(_sample_mix_kernel, nseq=nseq, ts=ts, pos0=pos0),
        grid=(n // rows,), in_specs=in_specs, out_specs=out_specs, out_shape=out_shape,
        scratch_shapes=[pltpu.VMEM((HIST_ROWS + ts, POOL_W), F32)],
        compiler_params=_cparams(("parallel",), 48), name="sample_mix")(
            x["q"], x["k"], x["g"], x["v"], x["pv"], state2, hist16, lw["gla_norm"], consts["gmat"], lw["pool_w"], lw["pool_scale"])


def _suffix_rows(x, period):
    n = x.shape[0]
    row = lax.broadcasted_iota(jnp.int32, x.shape, 0) % period
    k = 1
    while k < period:
        x = x + jnp.where(row + k < period, pltpu.roll(x, n - k, axis=0), 0.0)
        k *= 2
    return x


def _lfsum_kernel(lf_ref, sw_ref, o_ref):
    n = lf_ref.shape[0]
    hi, mid, lo = _split3(lf_ref[...])
    parts = jnp.concatenate([hi.astype(F32), mid.astype(F32), lo.astype(F32)], axis=0).astype(BF16)
    r3 = _dot(parts, sw_ref[...])
    o_ref[...] = r3[0:n] + r3[n:2 * n] + r3[2 * n:3 * n]


def _lfsum(lf2, sw, tr):
    n = lf2.shape[0]
    return pl.pallas_call(
        _lfsum_kernel, grid=(n // tr,),
        in_specs=[pl.BlockSpec((tr, PAGE), lambda i: (i, 0)), _const_spec((PAGE, 2 * PAGE))],
        out_specs=pl.BlockSpec((tr, 2 * PAGE), lambda i: (i, 0)), out_shape=jax.ShapeDtypeStruct((n, 2 * PAGE), F32),
        compiler_params=_cparams(("parallel",), 32), name="lfsum")(lf2, sw)


def _sample_fox_kernel(pt_ref, fq_ref, kn_ref, vn_ref, an_ref, rt_ref, ck_ref, cv_ref, o_ref,
                       kbuf, vbuf, ksem, vsem, *, n_pages, ts, n_phys, layer):
    b = pl.program_id(0)
    nb = pl.num_programs(0)
    slot = b % 2
    nxt = jnp.minimum(b + 1, nb - 1)

    def k_copy(seq, s, p):
        return pltpu.make_async_copy(ck_ref.at[layer, pt_ref[seq, p]], kbuf.at[s, p], ksem.at[s])

    def v_copy(seq, s, p):
        return pltpu.make_async_copy(cv_ref.at[layer, pt_ref[seq, p]], vbuf.at[s, p], vsem.at[s])

    @pl.when(b == 0)
    def _():
        for p in range(n_pages):
            k_copy(0, 0, p).start()
            v_copy(0, 0, p).start()

    for p in range(n_pages):
        k_copy(b, slot, p).wait()
    r2 = jnp.concatenate([rt_ref[pl.ds(h * n_phys + pt_ref[b, p], 1), :]
                          for h in range(N_HEADS) for p in range(n_pages)], axis=0)
    tot = r2[:, PAGE:]
    rfull = r2[:, :PAGE] + (_suffix_rows(tot, n_pages) - tot)
    cn = _cumsum_rows(an_ref[0])
    cn2 = jnp.concatenate([cn, cn], axis=0)
    cparts = _split3(cn)
    lane = lax.broadcasted_iota(jnp.int32, (2 * ts, LANES), 1)
    first = lax.broadcasted_iota(jnp.int32, (2 * ts, LANES), 0) < ts
    own = (lane < HEAD_DIM) == first
    causal = (lax.broadcasted_iota(jnp.int32, (2 * ts, ts), 0) % ts) >= lax.broadcasted_iota(jnp.int32, (2 * ts, ts), 1)
    for pr in range(N_HEADS // 2):
        cols = slice(pr * LANES, (pr + 1) * LANES)
        fq2 = jnp.concatenate([fq_ref[0, :, cols], fq_ref[0, :, cols]], axis=0)
        q2 = jnp.where(own, fq2, 0.0).astype(BF16)
        onehot = lane == jnp.where(first, 2 * pr, 2 * pr + 1)
        cq = jnp.sum(jnp.where(onehot, cn2, 0.0), axis=-1, keepdims=True)
        sel = jnp.where(onehot, 1.0, 0.0).astype(BF16)
        cn_t = _dot_nt(sel, cparts[0]) + _dot_nt(sel, cparts[1]) + _dot_nt(sel, cparts[2])
        s_new = jnp.where(causal, _dot_nt(q2, kn_ref[0, :, cols].astype(BF16)) + cq - cn_t, NEG)
        s_pages = []
        for p in range(n_pages):
            if pr == 0:
                k_copy(nxt, 1 - slot, p).start()
            kt = kbuf[slot, p, 2 * pr:2 * pr + 2].reshape(LANES, PAGE).astype(BF16)
            r0, r1 = 2 * pr * n_pages + p, (2 * pr + 1) * n_pages + p
            bias = jnp.concatenate([jnp.broadcast_to(rfull[r0:r0 + 1, :], (ts, PAGE)),
                                    jnp.broadcast_to(rfull[r1:r1 + 1, :], (ts, PAGE))], axis=0)
            s_pages.append(_dot(q2, kt) + bias + cq)
        mm = s_pages[0]
        for s in s_pages[1:]:
            mm = jnp.maximum(mm, s)
        m = jnp.maximum(jnp.max(mm, axis=-1, keepdims=True), jnp.max(s_new, axis=-1, keepdims=True))
        p_new = jnp.exp(s_new - m)
        l = jnp.sum(p_new, axis=-1, keepdims=True)
        acc = _dot(p_new.astype(BF16), vn_ref[0, :, cols].astype(BF16))
        if pr == 0:
            for p in range(n_pages):
                v_copy(b, slot, p).wait()
        for p in range(n_pages):
            if pr == 0:
                v_copy(nxt, 1 - slot, p).start()
            pe = jnp.exp(s_pages[p] - m)
            l = l + jnp.sum(pe, axis=-1, keepdims=True)
            vt = vbuf[slot, p, 2 * pr:2 * pr + 2].reshape(LANES, PAGE).astype(BF16)
            acc = acc + _dot_nt(pe.astype(BF16), vt)
        o2 = acc / l
        o_ref[0, :, cols] = jnp.where(lane[:ts] < HEAD_DIM, o2[:ts], o2[ts:])

    @pl.when(b == nb - 1)
    def _():
        for p in range(n_pages):
            k_copy(nxt, 1 - slot, p).wait()
            v_copy(nxt, 1 - slot, p).wait()


def _sample_fox(layer, page_table, fq, kn, vn, an, rt2, cache_kt, cache_vt):
    nb, ts, _ = fq.shape
    n_pages = page_table.shape[1]
    n_phys = cache_kt.shape[1]
    seq = lambda w: pl.BlockSpec((1, ts, w), lambda b, pt: (b, 0, 0))
    rt_spec = pl.BlockSpec((N_HEADS * n_phys, 2 * PAGE), lambda b, pt: (layer, 0), pipeline_mode=pl.Buffered(1))
    hbm = pl.BlockSpec(memory_space=pl.ANY)
    page_buf = pltpu.VMEM((2, n_pages, N_HEADS, HEAD_DIM, PAGE), F32)
    grid_spec = pltpu.PrefetchScalarGridSpec(
        num_scalar_prefetch=1, grid=(nb,),
        in_specs=[seq(HEADS_W), seq(HEADS_W), seq(HEADS_W), seq(LANES), rt_spec, hbm, hbm],
        out_specs=pl.BlockSpec((1, ts, HEADS_W), lambda b, pt: (b, 0, 0)),
        scratch_shapes=[page_buf, page_buf, pltpu.SemaphoreType.DMA((2,)), pltpu.SemaphoreType.DMA((2,))])
    return pl.pallas_call(
        functools.partial(_sample_fox_kernel, n_pages=n_pages, ts=ts, n_phys=n_phys, layer=layer), grid_spec=grid_spec,
        out_shape=jax.ShapeDtypeStruct((nb, ts, HEADS_W), F32),
        compiler_params=_cparams(("arbitrary",), 56), name="sample_fox")(
            page_table, fq, kn, vn, an, rt2, cache_kt, cache_vt)


def _wprep_kernel(w_ref, o_ref, *, depth):
    per_col = (D_MODEL // LANES) * depth
    for l in range(depth):
        for rt in range(D_MODEL // LANES):
            blk = w_ref[pl.ds(rt * depth + l, LANES, stride=per_col), :]
            o_ref[l, rt * LANES:(rt + 1) * LANES, :] = blk.T.astype(BF16)


def _wprep(w_in):
    depth, d, c = w_in.shape
    per_col = (d // LANES) * depth
    cp = -(-c // LANES) * LANES
    w2 = jnp.transpose(w_in, (2, 0, 1)).reshape(c, depth, d // LANES, LANES)
    w2 = jnp.transpose(w2, (0, 2, 1, 3)).reshape(c * per_col, LANES)
    w2 = jnp.pad(w2, ((0, (cp - c) * per_col), (0, 0)))
    return pl.pallas_call(
        functools.partial(_wprep_kernel, depth=depth), grid=(cp // LANES,),
        in_specs=[pl.BlockSpec((LANES * per_col, LANES), lambda j: (j, 0))],
        out_specs=pl.BlockSpec((depth, d, LANES), lambda j: (0, 0, j)),
        out_shape=jax.ShapeDtypeStruct((depth, d, cp), BF16),
        compiler_params=_cparams(("parallel",), 32), name="wprep")(w2)


def _layer_weights(l, norm_pre, norm_post, w_rm, gla_w_up, gla_b_gate, gla_norm, pool_w, pool_scale, fox_b_f, w_out):
    idx = [sum(SPLIT_SIZES[:i + 1]) for i in range(len(SPLIT_SIZES) - 1)]
    gq, gk, gv, glr, gg, pv, pg, fq, fk, fv, ff, fg = jnp.split(w_rm[l, :, :sum(SPLIT_SIZES)], idx, axis=1)
    zc = lambda n: jnp.zeros((D_MODEL, n), BF16)
    aux = jnp.concatenate([ff, zc(AUX_LR - N_HEADS), glr, zc(LANES - AUX_LR - GLA_LR)], axis=1)
    w = jnp.concatenate([gq, zc(COL_K - QK_W), gk, zc(COL_GV - COL_K - QK_W), gv, gg, pg, fg, pv, fq, fk, fv, aux], axis=1)
    w_up = jnp.zeros((LANES, QK_W), F32).at[AUX_LR:AUX_LR + GLA_LR].set(gla_w_up[l])
    bf = jnp.zeros((1, LANES), F32).at[0, AUX_FF:AUX_FF + N_HEADS].set(fox_b_f[l])
    pw = jnp.zeros((POOL_W, POOL_W), F32)
    for gi in range(len(POOL_WINDOWS)):
        sl = slice(gi * POOL_GROUP, (gi + 1) * POOL_GROUP)
        pw = pw.at[sl, sl].set(pool_w[l, gi])
    return {"pre": norm_pre[l][None], "post": norm_post[l][None], "w_in": w, "w_up": w_up.astype(BF16),
            "b_gate": gla_b_gate[l][None], "bf": bf, "gla_norm": jnp.tile(gla_norm[l], N_HEADS)[None],
            "pool_w": pw.astype(BF16), "pool_scale": pool_scale[l][None], "w_out": w_out[l].astype(BF16)}


def _constants():
    i = np.arange(HEADS_W)
    gmat = np.where((i[:, None] // HEAD_DIM) == (i[None, :] // HEAD_DIM), 1.0 / HEAD_DIM, 0.0)
    r = np.arange(PAGE)
    tri = r[:, None] >= r[None, :]
    a = np.arange(LANES)
    place = np.stack([(a[:, None] < N_HEADS) & (a[None, :] == a[:, None] + off) for off in (0, 6, 12, 18, 24, 30)])
    sw = np.concatenate([r[:, None] > r[None, :], np.ones((PAGE, PAGE), bool)], axis=1)
    bf = lambda x: jnp.asarray(x.astype(np.float32), dtype=BF16)
    return {"gmat": bf(gmat), "tri": bf(tri), "place": bf(place), "sw": bf(sw)}


def _state_from_bd(st):
    B = st.shape[0]
    s = st.reshape(B, N_HEADS, HEAD_DIM, N_HEADS, GLA_DK)
    s = jnp.stack([s[:, h, :, h, :] for h in range(N_HEADS)], axis=1)
    return jnp.swapaxes(s, 2, 3)


def _sequence_layer(h, lw, consts, prefix, tm, tq, chunk):
    n_pre = 0 if prefix is None else prefix["k"].shape[0]
    x = _inproj(h, lw, tm, prefix)
    s0 = jnp.zeros((HEADS_W, QK_W), F32) if prefix is None else prefix["state"]
    hist0 = jnp.zeros((HIST_ROWS, POOL_W), F32) if prefix is None else prefix["hist"]
    og, op, st = _glapool(x, lw, consts, tm, chunk, n_pre, s0, hist0)
    of = _fox(x, consts, tq, n_pre)
    h_new = _outproj(h, og, op, of, x["gates"], lw, tm)
    return h_new, x, st


def _sample_layer(l, h, lw, consts, page_table, rt5, cache_kt, cache_vt, state2, hist16, nb, ts, tm, nseq):
    past = page_table.shape[1] * PAGE
    x = _inproj(h, lw, tm)
    flat = {n: x[n][0] for n in ("q", "k", "g", "pv")}
    flat["v"] = x["v"][0].astype(F32)
    og, op, s_new = _sample_mix(flat, state2, hist16, lw, consts, nseq, ts, past)
    per_seq = lambda a, w: a.reshape(nb, ts, w)
    of = _sample_fox(l, page_table, per_seq(x["fq"][0].astype(F32), HEADS_W), per_seq(x["fk"][0], HEADS_W),
                     per_seq(x["fv"][0], HEADS_W), per_seq(x["aux"][0], LANES), rt5, cache_kt, cache_vt)
    h_new = _outproj(h, og[None], op[None], of.reshape(1, nb * ts, HEADS_W), x["gates"], lw, tm)
    return h_new, x, s_new


def _largest_tile(n, cap):
    t = min(n, cap)
    while n % t:
        t -= 8
    return t


def kernel(x_prompt, x_sample, cache_fox_k, cache_fox_v, cache_fox_logf, state_gla, state_pool, page_table,
           meta_tokens, norm_pre, norm_post, w_in, gla_w_up, gla_b_gate, gla_norm, pool_w, pool_scale,
           fox_b_f, w_out):
    B, T, _ = x_prompt.shape
    nb, ts, _ = x_sample.shape
    depth, n_phys = cache_fox_k.shape[:2]
    consts = _constants()
    cache_kt = jnp.transpose(cache_fox_k, (0, 1, 3, 4, 2))
    cache_vt = jnp.transpose(cache_fox_v, (0, 1, 3, 4, 2))
    lf2 = jnp.transpose(cache_fox_logf, (0, 3, 1, 2)).reshape(depth * N_HEADS * n_phys, PAGE)
    rt5 = _lfsum(lf2, consts["sw"], _largest_tile(depth * N_HEADS * n_phys, 512))
    state2 = state_gla.reshape(depth, nb, QK_W, HEAD_DIM)
    state2 = jnp.concatenate([state2, state2], axis=-1)
    hist16 = jnp.pad(state_pool, ((0, 0), (0, 0), (HIST_ROWS - POOL_HIST, 0), (0, 0)))
    tm_p, tm_s = _largest_tile(T, 256), _largest_tile(nb * ts, 512)
    w_rm = _wprep(w_in)

    h_m, h_p, h_s = meta_tokens[None], x_prompt, x_sample.reshape(1, nb * ts, D_MODEL)
    out = [[] for _ in range(10)]
    for l in range(depth):
        lw = _layer_weights(l, norm_pre, norm_post, w_rm, gla_w_up, gla_b_gate, gla_norm, pool_w, pool_scale, fox_b_f, w_out)
        h_m_new, xm, st_m = _sequence_layer(h_m, lw, consts, None, N_META, N_META, N_META)
        prefix = {"k": xm["fk"][0], "v": xm["fv"][0], "aux": xm["aux"][0], "state": st_m[0], "hist": xm["pv"][0]}
        h_p, xp, st_p = _sequence_layer(h_p, lw, consts, prefix, tm_p, tm_p, GLA_CHUNK)
        h_m = h_m_new
        h_s, xs, s_new = _sample_layer(l, h_s, lw, consts, page_table, rt5, cache_kt, cache_vt,
                                       state2[l], hist16[l], nb, ts, tm_s, 8)
        L = N_META + T
        out[0].append(xp["fk"].reshape(B, L, N_HEADS, HEAD_DIM))
        out[1].append(xp["fv"].reshape(B, L, N_HEADS, HEAD_DIM))
        out[2].append(xp["aux"][:, :, :N_HEADS])
        out[3].append(_state_from_bd(st_p))
        out[4].append(xp["pv"][:, T - POOL_HIST:])
        out[5].append(xs["fk"].reshape(nb, ts, N_HEADS, HEAD_DIM))
        out[6].append(xs["fv"].reshape(nb, ts, N_HEADS, HEAD_DIM))
        out[7].append(xs["aux"][0, :, :N_HEADS].reshape(nb, ts, N_HEADS))
        out[8].append(s_new[:, :, :HEAD_DIM].reshape(nb, N_HEADS, GLA_DK, HEAD_DIM))
        out[9].append(jnp.concatenate([state_pool[l], xs["pv"][0].reshape(nb, ts, POOL_W)], axis=1)[:, ts:])
    return (h_p, h_s.reshape(nb, ts, D_MODEL)) + tuple(jnp.stack(o) for o in out)
```

```python
import functools

import jax
import jax.numpy as jnp
import numpy as np
from jax import lax
from jax.experimental import pallas as pl
from jax.experimental.pallas import tpu as pltpu

F32 = jnp.float32
BF16 = jnp.bfloat16

D_MODEL = 1024
N_HEADS = 6
GLA_DK = 32
HEAD_DIM = 64
GLA_LR = 16
GLA_GATE_NORM = 16.0
GLA_CHUNK = 64
QK_W = N_HEADS * GLA_DK
HEADS_W = N_HEADS * HEAD_DIM
POOL_W = 256
POOL_GROUP = 64
POOL_WINDOWS = (2, 4, 8, 16)
POOL_HIST = 15
HIST_ROWS = 16
N_META = 16
PAGE = 128
EPS = 1e-6
LANES = 128
NEG = -1e30

COL_Q, COL_K, COL_GV, COL_GATES, COL_PV = 0, 256, 512, 896, 1920
COL_FQ, COL_FK, COL_FV, COL_AUX, W_IN_COLS = 2176, 2560, 2944, 3328, 3456
GATES_W = HEADS_W + POOL_W + HEADS_W
AUX_FF, AUX_LR = 0, 8
SPLIT_SIZES = (QK_W, QK_W, HEADS_W, GLA_LR, HEADS_W, POOL_W, POOL_W, HEADS_W, HEADS_W, HEADS_W, N_HEADS, HEADS_W)

AUXK_ONES = (18, 36)


def _cparams(sem, vmem_mb):
    return pltpu.CompilerParams(dimension_semantics=sem, vmem_limit_bytes=vmem_mb * 1024 * 1024)


def _const_spec(shape):
    return pl.BlockSpec(shape, lambda *_: (0,) * len(shape))


def _dot(a, b):
    return jnp.dot(a, b, preferred_element_type=F32)


def _dot_nt(a, b):
    return lax.dot_general(a, b, (((1,), (1,)), ((), ())), preferred_element_type=F32)


def _dot_tn(a, b):
    return lax.dot_general(a, b, (((0,), (0,)), ((), ())), preferred_element_type=F32)


def _log_sigmoid(x):
    return jnp.minimum(x, 0.0) - jnp.log1p(jnp.exp(-jnp.abs(x)))


def _silu(x):
    return x * (1.0 / (1.0 + jnp.exp(-x)))


def _split3(x):
    hi = x.astype(BF16)
    r = x - hi.astype(F32)
    mid = r.astype(BF16)
    lo = (r - mid.astype(F32)).astype(BF16)
    return hi, mid, lo


def _cumsum_rows(x):
    n = x.shape[0]
    row = lax.broadcasted_iota(jnp.int32, x.shape, 0)
    k = 1
    while k < n:
        x = x + jnp.where(row >= k, pltpu.roll(x, k, axis=0), 0.0)
        k *= 2
    return x


def _inproj_kernel(*refs, tm, n_pre):
    if n_pre:
        (x_ref, pre_ref, w_ref, wup_ref, bg_ref, bf_ref, kpre_ref, vpre_ref, apre_ref,
         q_ref, k_ref, g_ref, v_ref, gates_ref, pv_ref, fq_ref, fk_ref, fv_ref, aux_ref) = refs
    else:
        (x_ref, pre_ref, w_ref, wup_ref, bg_ref, bf_ref,
         q_ref, k_ref, g_ref, v_ref, gates_ref, pv_ref, fq_ref, fk_ref, fv_ref, aux_ref) = refs
    t = pl.program_id(1)
    x = x_ref[0]
    xn = x * lax.rsqrt(jnp.mean(x * x, axis=-1, keepdims=True) + EPS) * pre_ref[...]
    xb = xn.astype(BF16)

    def seg(c0, width):
        return _dot(xb, w_ref[:, c0:c0 + width])

    q_ref[0] = seg(COL_Q, QK_W) * (GLA_DK ** -0.5)
    k_ref[0] = seg(COL_K, QK_W)
    v_ref[0] = seg(COL_GV, HEADS_W).astype(BF16)
    gates_ref[0] = seg(COL_GATES, GATES_W)
    pv_ref[0] = seg(COL_PV, POOL_W)
    fq_ref[0] = (seg(COL_FQ, HEADS_W) * (HEAD_DIM ** -0.5)).astype(BF16)
    aux = seg(COL_AUX, LANES)
    g_ref[0] = _log_sigmoid(_dot(aux.astype(BF16), wup_ref[...]) + bg_ref[...]) * (1.0 / GLA_GATE_NORM)
    rows = pl.ds(pl.multiple_of(n_pre + t * tm, 8), tm)
    fk_ref[0, rows, :] = seg(COL_FK, HEADS_W)
    fv_ref[0, rows, :] = seg(COL_FV, HEADS_W)
    aux_ref[0, rows, :] = _log_sigmoid(aux + bf_ref[...])
    if n_pre:
        @pl.when(t == 0)
        def _():
            fk_ref[0, 0:n_pre, :] = kpre_ref[...]
            fv_ref[0, 0:n_pre, :] = vpre_ref[...]
            aux_ref[0, 0:n_pre, :] = apre_ref[...]


def _inproj(h, lw, tm, prefix=None):
    B, T, _ = h.shape
    n_pre = 0 if prefix is None else prefix["k"].shape[0]
    L = n_pre + T
    tile = lambda w: pl.BlockSpec((1, tm, w), lambda b, t: (b, t, 0))
    whole = lambda w: pl.BlockSpec((1, L, w), lambda b, t: (b, 0, 0))
    in_specs = [tile(D_MODEL), _const_spec((1, D_MODEL)), _const_spec((D_MODEL, W_IN_COLS)),
                _const_spec((LANES, QK_W)), _const_spec((1, QK_W)), _const_spec((1, LANES))]
    args = [h, lw["pre"], lw["w_in"], lw["w_up"], lw["b_gate"], lw["bf"]]
    if n_pre:
        in_specs += [_const_spec((n_pre, HEADS_W)), _const_spec((n_pre, HEADS_W)), _const_spec((n_pre, LANES))]
        args += [prefix["k"], prefix["v"], prefix["aux"]]
    out_shape = [jax.ShapeDtypeStruct((B, T, QK_W), F32), jax.ShapeDtypeStruct((B, T, QK_W), F32),
                 jax.ShapeDtypeStruct((B, T, QK_W), F32), jax.ShapeDtypeStruct((B, T, HEADS_W), BF16),
                 jax.ShapeDtypeStruct((B, T, GATES_W), F32), jax.ShapeDtypeStruct((B, T, POOL_W), F32),
                 jax.ShapeDtypeStruct((B, T, HEADS_W), BF16), jax.ShapeDtypeStruct((B, L, HEADS_W), F32),
                 jax.ShapeDtypeStruct((B, L, HEADS_W), F32), jax.ShapeDtypeStruct((B, L, LANES), F32)]
    out_specs = [tile(QK_W), tile(QK_W), tile(QK_W), tile(HEADS_W), tile(GATES_W), tile(POOL_W),
                 tile(HEADS_W), whole(HEADS_W), whole(HEADS_W), whole(LANES)]
    names = ("q", "k", "g", "v", "gates", "pv", "fq", "fk", "fv", "aux")
    outs = pl.pallas_call(
        functools.partial(_inproj_kernel, tm=tm, n_pre=n_pre),
        grid=(B, T // tm), in_specs=in_specs, out_specs=out_specs, out_shape=out_shape,
        compiler_params=_cparams(("parallel", "arbitrary"), 56), name="inproj")(*args)
    return dict(zip(names, outs))


def _gla_masks(chunk):
    lane_qk = lax.broadcasted_iota(jnp.int32, (1, QK_W), 1) // GLA_DK
    lane_v = lax.broadcasted_iota(jnp.int32, (1, HEADS_W), 1) // HEAD_DIM
    r = lax.broadcasted_iota(jnp.int32, (N_HEADS * chunk, chunk), 0)
    c = lax.broadcasted_iota(jnp.int32, (N_HEADS * chunk, chunk), 1)
    tril = (r % chunk) >= c
    return lane_qk, lane_v, tril


def _gla_intra(q, k, g, v_bf, masks, chunk):
    lane_qk, lane_v, tril = masks
    bcum = _cumsum_rows(g)
    qe_bf = (q * jnp.exp(bcum)).astype(BF16)
    ke_bf = (k * jnp.exp(-bcum)).astype(BF16)
    b_end = bcum[chunk - 1:chunk, :]
    kd_bf = (k * jnp.exp(b_end - bcum)).astype(BF16)
    zero = jnp.zeros_like(qe_bf)
    qe_stack = jnp.concatenate([jnp.where(lane_qk == h, qe_bf, zero) for h in range(N_HEADS)], axis=0)
    a = jnp.where(tril, _dot_nt(qe_stack, ke_bf), 0.0)
    o_full = _dot(a.astype(BF16), v_bf)
    o_intra = jnp.zeros((chunk, HEADS_W), F32)
    for h in range(N_HEADS):
        o_intra = o_intra + jnp.where(lane_v == h, o_full[h * chunk:(h + 1) * chunk, :], 0.0)
    return o_intra, qe_bf, kd_bf, jnp.exp(b_end)


def _head_norm(o, gmat_bf, gn):
    o2 = o * o
    hi = o2.astype(BF16)
    lo = (o2 - hi.astype(F32)).astype(BF16)
    ms = _dot(hi, gmat_bf) + _dot(lo, gmat_bf)
    return o * lax.rsqrt(ms + EPS) * gn


def _pool_delta(ext_ref, pv, pos0, tm):
    ext_ref[HIST_ROWS:HIST_ROWS + tm, :] = pv
    lane_g = lax.broadcasted_iota(jnp.int32, (1, POOL_W), 1) // POOL_GROUP
    pos = pos0 + lax.broadcasted_iota(jnp.int32, (tm, 1), 0)
    run = pv
    sums = jnp.zeros((tm, POOL_W), F32)
    cnt = jnp.zeros((tm, POOL_W), F32)
    for k in range(1, POOL_WINDOWS[-1]):
        run = run + ext_ref[HIST_ROWS - k:HIST_ROWS - k + tm, :]
        if k + 1 in POOL_WINDOWS:
            gi = POOL_WINDOWS.index(k + 1)
            sums = jnp.where(lane_g == gi, run, sums)
            cnt = jnp.where(lane_g == gi, jnp.minimum(pos + 1, k + 1).astype(F32), cnt)
    return sums / cnt - pv


def _glapool_kernel(q_ref, k_ref, g_ref, v_ref, pv_ref, s0_ref, hist0_ref, gn_ref, gmat_ref, pw_ref, ps_ref,
                    og_ref, op_ref, sout_ref, st_ref, ext_ref, *, tm, chunk, pos0):
    t = pl.program_id(1)

    @pl.when(t == 0)
    def _():
        st_ref[...] = s0_ref[...]
        ext_ref[0:HIST_ROWS, :] = hist0_ref[...]

    masks = _gla_masks(chunk)
    rr = lax.broadcasted_iota(jnp.int32, (HEADS_W, QK_W), 0) // HEAD_DIM
    cc = lax.broadcasted_iota(jnp.int32, (HEADS_W, QK_W), 1) // GLA_DK
    bd = rr == cc
    st = st_ref[...]
    outs = []
    for c in range(tm // chunk):
        r = slice(c * chunk, (c + 1) * chunk)
        v_bf = v_ref[0, r, :]
        o_intra, qe_bf, kd_bf, decay = _gla_intra(q_ref[0, r, :], k_ref[0, r, :], g_ref[0, r, :], v_bf, masks, chunk)
        outs.append(o_intra + _dot_nt(qe_bf, st.astype(BF16)))
        st = st * decay + jnp.where(bd, _dot_tn(v_bf, kd_bf), 0.0)
    st_ref[...] = st
    o = outs[0] if len(outs) == 1 else jnp.concatenate(outs, axis=0)
    og_ref[0] = _head_norm(o, gmat_ref[...], gn_ref[...])
    d = _pool_delta(ext_ref, pv_ref[0], pos0 + t * tm, tm)
    op_ref[0] = _dot(d.astype(BF16), pw_ref[...]) * ps_ref[...]
    ext_ref[0:HIST_ROWS, :] = ext_ref[tm:tm + HIST_ROWS, :]

    @pl.when(t == pl.num_programs(1) - 1)
    def _():
        sout_ref[0] = st


def _glapool(x, lw, consts, tm, chunk, pos0, s0, hist0):
    B, T, _ = x["q"].shape
    tile = lambda w: pl.BlockSpec((1, tm, w), lambda b, t: (b, t, 0))
    in_specs = [tile(QK_W), tile(QK_W), tile(QK_W), tile(HEADS_W), tile(POOL_W),
                _const_spec((HEADS_W, QK_W)), _const_spec((HIST_ROWS, POOL_W)), _const_spec((1, HEADS_W)),
                _const_spec((HEADS_W, HEADS_W)), _const_spec((POOL_W, POOL_W)), _const_spec((1, POOL_W))]
    out_shape = [jax.ShapeDtypeStruct((B, T, HEADS_W), F32), jax.ShapeDtypeStruct((B, T, POOL_W), F32),
                 jax.ShapeDtypeStruct((B, HEADS_W, QK_W), F32)]
    out_specs = [tile(HEADS_W), tile(POOL_W), pl.BlockSpec((1, HEADS_W, QK_W), lambda b, t: (b, 0, 0))]
    return pl.pallas_call(
        functools.partial(_glapool_kernel, tm=tm, chunk=chunk, pos0=pos0),
        grid=(B, T // tm), in_specs=in_specs, out_specs=out_specs, out_shape=out_shape,
        scratch_shapes=[pltpu.VMEM((HEADS_W, QK_W), F32), pltpu.VMEM((HIST_ROWS + tm, POOL_W), F32)],
        compiler_params=_cparams(("parallel", "arbitrary"), 48), name="glapool")(
            x["q"], x["k"], x["g"], x["v"], x["pv"], s0, hist0, lw["gla_norm"], consts["gmat"], lw["pool_w"], lw["pool_scale"])


def _fox_features(lf, carry, tri_bf, place_ref):
    r = lf.shape[0]
    lane = lax.broadcasted_iota(jnp.int32, (1, LANES), 1)
    tri = tri_bf[0:r, 0:r]
    hi, mid, lo = _split3(lf)
    f = _dot(tri, hi) + _dot(tri, mid) + _dot(tri, lo) + carry
    fh, fm, fl = _split3(f)
    ones_k = jnp.where((lane >= AUXK_ONES[0]) & (lane < AUXK_ONES[1]), 1.0, 0.0)
    ones_q = jnp.where(lane < AUXK_ONES[0], 1.0, 0.0)
    kaux = ones_k - (_dot(fh, place_ref[0]) + _dot(fm, place_ref[1]) + _dot(fl, place_ref[2]))
    qaux = ones_q + (_dot(fh, place_ref[3]) + _dot(fm, place_ref[4]) + _dot(fl, place_ref[5]))
    return kaux.astype(BF16), qaux.astype(BF16), f[r - 1:r, :]


def _fox_kernel(fq_ref, k_ref, v_ref, aux_ref, tri_ref, place_ref, h_ref, og_ref, op_ref, gates_ref, wo_ref, post_ref,
                out_ref, kb_ref, vb_ref, kaux_ref, qaux_ref, q2_ref, m_ref, acc_ref, sa_ref, sb_ref, *, T, n_pre, tq):
    t = pl.program_id(1)

    @pl.when(t == 0)
    def _():
        kb_ref[...] = k_ref[0].astype(BF16)
        vb_ref[...] = v_ref[0].astype(BF16)
        cb = min(PAGE, T)
        blocks = ([(0, n_pre)] if n_pre else []) + [(n_pre + i * cb, cb) for i in range(T // cb)]
        carry = jnp.zeros((1, LANES), F32)
        for r0, r in blocks:
            ka, qa, carry = _fox_features(aux_ref[0, r0:r0 + r, :], carry, tri_ref[...], place_ref)
            kaux_ref[r0:r0 + r, :] = ka
            qaux_ref[r0:r0 + r, :] = qa

    lane = lax.broadcasted_iota(jnp.int32, (1, LANES), 1)
    q0 = pl.multiple_of(n_pre + t * tq, 16)
    qa = qaux_ref[pl.ds(q0, tq), :]
    fq = fq_ref[0]
    zero_bf = jnp.zeros((tq, LANES), BF16)
    for h in range(N_HEADS):
        fq_p = fq[:, (h // 2) * LANES:(h // 2 + 1) * LANES]
        qmask = (lane == h) | (lane == 6 + h) | (lane == 12 + h) | (lane == 18 + h) | (lane == 24 + h) | (lane == 30 + h)
        q2_ref[h] = jnp.concatenate([jnp.where((lane // HEAD_DIM) == (h % 2), fq_p, zero_bf), jnp.where(qmask, qa, zero_bf)], axis=1)
    m_ref[...] = jnp.full(m_ref.shape, NEG, F32)
    acc_ref[...] = jnp.zeros(acc_ref.shape, F32)

    def logits(j, n, s_out):
        r0 = 0 if j is None else pl.multiple_of(n_pre + j * tq, 16)
        kaux = kaux_ref[pl.ds(r0, n), :]
        for p in range(N_HEADS // 2):
            k2 = jnp.concatenate([kb_ref[pl.ds(r0, n), p * LANES:(p + 1) * LANES], kaux], axis=1)
            for h in (2 * p, 2 * p + 1):
                s_out[h] = _dot_nt(q2_ref[h], k2)

    def update(j, n, s_in, causal):
        r0 = 0 if j is None else pl.multiple_of(n_pre + j * tq, 16)
        ones = jnp.ones((n, LANES), BF16)
        if causal:
            keep = lax.broadcasted_iota(jnp.int32, (tq, n), 0) >= lax.broadcasted_iota(jnp.int32, (tq, n), 1)
        for p in range(N_HEADS // 2):
            vp = vb_ref[pl.ds(r0, n), p * LANES:(p + 1) * LANES]
            for h in (2 * p, 2 * p + 1):
                v2 = jnp.where((lane // HEAD_DIM) == (h % 2), vp, ones)
                def read(z):
                    s = s_in[h] if j is None else s_in[h + jnp.minimum(z, 0)]
                    return jnp.where(keep, s, NEG) if causal else s
                m_old = m_ref[h]
                m_new = jnp.maximum(m_old, jnp.max(read(t), axis=-1, keepdims=True))
                alpha = jnp.exp(m_old - m_new)
                m_b = m_new[:, :n] if n < LANES else jnp.concatenate([m_new] * (n // LANES), axis=1)
                pe = jnp.exp(read(pl.program_id(0)) - m_b).astype(BF16)
                acc_ref[h] = alpha * acc_ref[h] + _dot(pe, v2)
                m_ref[h] = m_new

    if n_pre:
        pre = {}
        logits(None, n_pre, pre)
        update(None, n_pre, pre, False)

    logits(0, tq, sa_ref)

    def body(jj, c):
        j = 2 * jj
        logits(j + 1, tq, sb_ref)
        update(j, tq, sa_ref, False)
        logits(j + 2, tq, sa_ref)
        update(j + 1, tq, sb_ref, False)
        return c

    lax.fori_loop(0, t // 2, body, 0)

    @pl.when(t % 2 == 0)
    def _():
        update(t, tq, sa_ref, True)

    @pl.when(t % 2 == 1)
    def _():
        logits(t, tq, sb_ref)
        update(t - 1, tq, sa_ref, False)
        update(t, tq, sb_ref, True)

    o_fox = []
    for p in range(N_HEADS // 2):
        a0, a1 = acc_ref[2 * p], acc_ref[2 * p + 1]
        o_fox.append(jnp.where((lane // HEAD_DIM) == 0, a0 / pltpu.roll(a0, HEAD_DIM, axis=1),
                               a1 / pltpu.roll(a1, HEAD_DIM, axis=1)))
    y_in = jnp.concatenate([og_ref[0], op_ref[0]] + o_fox, axis=1) * _silu(gates_ref[0])
    y = _dot(y_in.astype(BF16), wo_ref[...])
    out_ref[0] = h_ref[0] + y * lax.rsqrt(jnp.mean(y * y, axis=-1, keepdims=True) + EPS) * post_ref[...]


def _fox_out(x, h, og, op, lw, consts, tq, n_pre):
    B, T, _ = x["fq"].shape
    L = n_pre + T
    whole = lambda w: pl.BlockSpec((1, L, w), lambda b, t: (b, 0, 0))
    tile = lambda w: pl.BlockSpec((1, tq, w), lambda b, t: (b, t, 0))
    in_specs = [tile(HEADS_W), whole(HEADS_W), whole(HEADS_W), whole(LANES),
                _const_spec((PAGE, PAGE)), _const_spec((6, LANES, LANES)),
                tile(D_MODEL), tile(HEADS_W), tile(POOL_W), tile(GATES_W), _const_spec((GATES_W, D_MODEL)), _const_spec((1, D_MODEL))]
    return pl.pallas_call(
        functools.partial(_fox_kernel, T=T, n_pre=n_pre, tq=tq),
        grid=(B, T // tq), in_specs=in_specs,
        out_specs=tile(D_MODEL),
        out_shape=jax.ShapeDtypeStruct((B, T, D_MODEL), F32),
        scratch_shapes=[pltpu.VMEM((L, HEADS_W), BF16), pltpu.VMEM((L, HEADS_W), BF16),
                        pltpu.VMEM((L, LANES), BF16), pltpu.VMEM((L, LANES), BF16),
                        pltpu.VMEM((N_HEADS, tq, 2 * LANES), BF16), pltpu.VMEM((N_HEADS, tq, LANES), F32),
                        pltpu.VMEM((N_HEADS, tq, LANES), F32),
                        pltpu.VMEM((N_HEADS, tq, tq), F32), pltpu.VMEM((N_HEADS, tq, tq), F32)],
        compiler_params=_cparams(("parallel", "arbitrary"), 48), name="fox")(
            x["fq"], x["fk"], x["fv"], x["aux"], consts["tri"], consts["place"],
            h, og, op, x["gates"], lw["w_out"], lw["post"])


def _outproj_kernel(h_ref, og_ref, op_ref, of_ref, gates_ref, wo_ref, post_ref, out_ref):
    y_in = jnp.concatenate([og_ref[0], op_ref[0], of_ref[0]], axis=1) * _silu(gates_ref[0])
    y = _dot(y_in.astype(BF16), wo_ref[...])
    out_ref[0] = h_ref[0] + y * lax.rsqrt(jnp.mean(y * y, axis=-1, keepdims=True) + EPS) * post_ref[...]


def _outproj(h, og, op, of, gates, lw, tm):
    B, T, _ = h.shape
    tile = lambda w: pl.BlockSpec((1, tm, w), lambda b, t: (b, t, 0))
    return pl.pallas_call(
        _outproj_kernel, grid=(B, T // tm),
        in_specs=[tile(D_MODEL), tile(HEADS_W), tile(POOL_W), tile(HEADS_W), tile(GATES_W),
                  _const_spec((GATES_W, D_MODEL)), _const_spec((1, D_MODEL))],
        out_specs=tile(D_MODEL), out_shape=jax.ShapeDtypeStruct((B, T, D_MODEL), F32),
        compiler_params=_cparams(("parallel", "parallel"), 48), name="outproj")(
            h, og, op, of, gates, lw["w_out"], lw["post"])


def _sample_mix_kernel(q_ref, k_ref, g_ref, v_ref, pv_ref, s_ref, hist_ref, gn_ref, gmat_ref, pw_ref, ps_ref,
                       og_ref, op_ref, sout_ref, ext_ref, *, nseq, ts, pos0):
    masks = _gla_masks(ts)
    rr = lax.broadcasted_iota(jnp.int32, (QK_W, HEADS_W), 0) // GLA_DK
    cc = lax.broadcasted_iota(jnp.int32, (QK_W, HEADS_W), 1) // HEAD_DIM
    bd = rr == cc
    ones = jnp.ones((3 * ts, LANES), BF16)
    outs, deltas = [], []
    for i in range(nseq):
        r = slice(i * ts, (i + 1) * ts)
        g = g_ref[r, :]
        v_bf = v_ref[r, :].astype(BF16)
        o_intra, qe_bf, kd_bf, _ = _gla_intra(q_ref[r, :], k_ref[r, :], g, v_bf, masks, ts)
        s2 = s_ref[i]
        s_bd = jnp.where(bd, jnp.concatenate([s2, s2, s2], axis=1), 0.0).astype(BF16)
        outs.append(o_intra + _dot(qe_bf, s_bd))
        u = jnp.where(bd, _dot_tn(kd_bf, v_bf), 0.0)
        x = jnp.concatenate([u[64 * j:64 * (j + 1), LANES * j:LANES * (j + 1)] for j in range(N_HEADS // 2)], axis=0)
        x = x + pltpu.roll(x, HEAD_DIM, axis=1)
        gh, gm, gl = _split3(g)
        gparts = jnp.concatenate([gh.astype(F32), gm.astype(F32), gl.astype(F32)], axis=0).astype(BF16)
        sout_ref[i] = s2 * jnp.exp(_dot_tn(gparts, ones)) + x
        ext_ref[0:HIST_ROWS, :] = hist_ref[i]
        deltas.append(_pool_delta(ext_ref, pv_ref[r, :], pos0, ts))
    og_ref[...] = _head_norm(jnp.concatenate(outs, axis=0), gmat_ref[...], gn_ref[...])
    op_ref[...] = _dot(jnp.concatenate(deltas, axis=0).astype(BF16), pw_ref[...]) * ps_ref[...]


def _sample_mix(x, state2, hist16, lw, consts, nseq, ts, pos0):
    n = x["q"].shape[0]
    rows = nseq * ts
    tile = lambda w: pl.BlockSpec((rows, w), lambda i: (i, 0))
    in_specs = [tile(QK_W), tile(QK_W), tile(QK_W), tile(HEADS_W), tile(POOL_W),
                pl.BlockSpec((nseq, QK_W, LANES), lambda i: (i, 0, 0)), pl.BlockSpec((nseq, HIST_ROWS, POOL_W), lambda i: (i, 0, 0)),
                _const_spec((1, HEADS_W)), _const_spec((HEADS_W, HEADS_W)), _const_spec((POOL_W, POOL_W)), _const_spec((1, POOL_W))]
    out_shape = [jax.ShapeDtypeStruct((n, HEADS_W), F32), jax.ShapeDtypeStruct((n, POOL_W), F32),
                 jax.ShapeDtypeStruct(state2.shape, F32)]
    out_specs = [tile(HEADS_W), tile(POOL_W), pl.BlockSpec((nseq, QK_W, LANES), lambda i: (i, 0, 0))]
    return pl.pallas_call(
        functools.partial(_sample_mix_kernel, nseq=nseq, ts=ts, pos0=pos0),
        grid=(n // rows,), in_specs=in_specs, out_specs=out_specs, out_shape=out_shape,
        scratch_shapes=[pltpu.VMEM((HIST_ROWS + ts, POOL_W), F32)],
        compiler_params=_cparams(("parallel",), 48), name="sample_mix")(
            x["q"], x["k"], x["g"], x["v"], x["pv"], state2, hist16, lw["gla_norm"], consts["gmat"], lw["pool_w"], lw["pool_scale"])


def _suffix_rows(x, period):
    n = x.shape[0]
    row = lax.broadcasted_iota(jnp.int32, x.shape, 0) % period
    k = 1
    while k < period:
        x = x + jnp.where(row + k < period, pltpu.roll(x, n - k, axis=0), 0.0)
        k *= 2
    return x


def _lfsum_kernel(lf_ref, sw_ref, o_ref):
    n = lf_ref.shape[0]
    hi, mid, lo = _split3(lf_ref[...])
    parts = jnp.concatenate([hi.astype(F32), mid.astype(F32), lo.astype(F32)], axis=0).astype(BF16)
    r3 = _dot(parts, sw_ref[...])
    o_ref[...] = r3[0:n] + r3[n:2 * n] + r3[2 * n:3 * n]


def _lfsum(lf2, sw, tr):
    n = lf2.shape[0]
    return pl.pallas_call(
        _lfsum_kernel, grid=(n // tr,),
        in_specs=[pl.BlockSpec((tr, PAGE), lambda i: (i, 0)), _const_spec((PAGE, 2 * PAGE))],
        out_specs=pl.BlockSpec((tr, 2 * PAGE), lambda i: (i, 0)), out_shape=jax.ShapeDtypeStruct((n, 2 * PAGE), F32),
        compiler_params=_cparams(("parallel",), 32), name="lfsum")(lf2, sw)


def _sample_fox_kernel(pt_ref, fq_ref, kn_ref, vn_ref, an_ref, rt_ref, ck_ref, cv_ref, o_ref,
                       kbuf, vbuf, ksem, vsem, *, n_pages, ts, n_phys, layer):
    b = pl.program_id(0)
    nb = pl.num_programs(0)
    slot = b % 2
    nxt = jnp.minimum(b + 1, nb - 1)

    def k_copy(seq, s, p):
        return pltpu.make_async_copy(ck_ref.at[layer, pt_ref[seq, p]], kbuf.at[s, p], ksem.at[s])

    def v_copy(seq, s, p):
        return pltpu.make_async_copy(cv_ref.at[layer, pt_ref[seq, p]], vbuf.at[s, p], vsem.at[s])

    @pl.when(b == 0)
    def _():
        for p in range(n_pages):
            k_copy(0, 0, p).start()
            v_copy(0, 0, p).start()

    for p in range(n_pages):
        k_copy(b, slot, p).wait()
    r2 = jnp.concatenate([rt_ref[pl.ds(h * n_phys + pt_ref[b, p], 1), :]
                          for h in range(N_HEADS) for p in range(n_pages)], axis=0)
    tot = r2[:, PAGE:]
    rfull = r2[:, :PAGE] + (_suffix_rows(tot, n_pages) - tot)
    cn = _cumsum_rows(an_ref[0])
    cn2 = jnp.concatenate([cn, cn], axis=0)
    cparts = _split3(cn)
    lane = lax.broadcasted_iota(jnp.int32, (2 * ts, LANES), 1)
    first = lax.broadcasted_iota(jnp.int32, (2 * ts, LANES), 0) < ts
    own = (lane < HEAD_DIM) == first
    causal = (lax.broadcasted_iota(jnp.int32, (2 * ts, ts), 0) % ts) >= lax.broadcasted_iota(jnp.int32, (2 * ts, ts), 1)
    for pr in range(N_HEADS // 2):
        cols = slice(pr * LANES, (pr + 1) * LANES)
        fq2 = jnp.concatenate([fq_ref[0, :, cols], fq_ref[0, :, cols]], axis=0)
        q2 = jnp.where(own, fq2, 0.0).astype(BF16)
        onehot = lane == jnp.where(first, 2 * pr, 2 * pr + 1)
        cq = jnp.sum(jnp.where(onehot, cn2, 0.0), axis=-1, keepdims=True)
        sel = jnp.where(onehot, 1.0, 0.0).astype(BF16)
        cn_t = _dot_nt(sel, cparts[0]) + _dot_nt(sel, cparts[1]) + _dot_nt(sel, cparts[2])
        s_new = jnp.where(causal, _dot_nt(q2, kn_ref[0, :, cols].astype(BF16)) + cq - cn_t, NEG)
        s_pages = []
        for p in range(n_pages):
            if pr == 0:
                k_copy(nxt, 1 - slot, p).start()
            kt = kbuf[slot, p, 2 * pr:2 * pr + 2].reshape(LANES, PAGE).astype(BF16)
            r0, r1 = 2 * pr * n_pages + p, (2 * pr + 1) * n_pages + p
            bias = jnp.concatenate([jnp.broadcast_to(rfull[r0:r0 + 1, :], (ts, PAGE)),
                                    jnp.broadcast_to(rfull[r1:r1 + 1, :], (ts, PAGE))], axis=0)
            s_pages.append(_dot(q2, kt) + bias + cq)
        mm = s_pages[0]
        for s in s_pages[1:]:
            mm = jnp.maximum(mm, s)
        m = jnp.maximum(jnp.max(mm, axis=-1, keepdims=True), jnp.max(s_new, axis=-1, keepdims=True))
        p_new = jnp.exp(s_new - m)
        l = jnp.sum(p_new, axis=-1, keepdims=True)
        acc = _dot(p_new.astype(BF16), vn_ref[0, :, cols].astype(BF16))
        if pr == 0:
            for p in range(n_pages):
                v_copy(b, slot, p).wait()
        for p in range(n_pages):
            if pr == 0:
                v_copy(nxt, 1 - slot, p).start()
            pe = jnp.exp(s_pages[p] - m)
            l = l + jnp.sum(pe, axis=-1, keepdims=True)
            vt = vbuf[slot, p, 2 * pr:2 * pr + 2].reshape(LANES, PAGE).astype(BF16)
            acc = acc + _dot_nt(pe.astype(BF16), vt)
        o2 = acc / l
        o_ref[0, :, cols] = jnp.where(lane[:ts] < HEAD_DIM, o2[:ts], o2[ts:])

    @pl.when(b == nb - 1)
    def _():
        for p in range(n_pages):
            k_copy(nxt, 1 - slot, p).wait()
            v_copy(nxt, 1 - slot, p).wait()


def _sample_fox(layer, page_table, fq, kn, vn, an, rt2, cache_kt, cache_vt):
    nb, ts, _ = fq.shape
    n_pages = page_table.shape[1]
    n_phys = cache_kt.shape[1]
    seq = lambda w: pl.BlockSpec((1, ts, w), lambda b, pt: (b, 0, 0))
    rt_spec = pl.BlockSpec((N_HEADS * n_phys, 2 * PAGE), lambda b, pt: (layer, 0), pipeline_mode=pl.Buffered(1))
    hbm = pl.BlockSpec(memory_space=pl.ANY)
    page_buf = pltpu.VMEM((2, n_pages, N_HEADS, HEAD_DIM, PAGE), F32)
    grid_spec = pltpu.PrefetchScalarGridSpec(
        num_scalar_prefetch=1, grid=(nb,),
        in_specs=[seq(HEADS_W), seq(HEADS_W), seq(HEADS_W), seq(LANES), rt_spec, hbm, hbm],
        out_specs=pl.BlockSpec((1, ts, HEADS_W), lambda b, pt: (b, 0, 0)),
        scratch_shapes=[page_buf, page_buf, pltpu.SemaphoreType.DMA((2,)), pltpu.SemaphoreType.DMA((2,))])
    return pl.pallas_call(
        functools.partial(_sample_fox_kernel, n_pages=n_pages, ts=ts, n_phys=n_phys, layer=layer), grid_spec=grid_spec,
        out_shape=jax.ShapeDtypeStruct((nb, ts, HEADS_W), F32),
        compiler_params=_cparams(("arbitrary",), 56), name="sample_fox")(
            page_table, fq, kn, vn, an, rt2, cache_kt, cache_vt)


def _wprep_kernel(w_ref, o_ref, *, depth):
    per_col = (D_MODEL // LANES) * depth
    for l in range(depth):
        for rt in range(D_MODEL // LANES):
            blk = w_ref[pl.ds(rt * depth + l, LANES, stride=per_col), :]
            o_ref[l, rt * LANES:(rt + 1) * LANES, :] = blk.T.astype(BF16)


def _wprep(w_in):
    depth, d, c = w_in.shape
    per_col = (d // LANES) * depth
    cp = -(-c // LANES) * LANES
    w2 = jnp.transpose(w_in, (2, 0, 1)).reshape(c, depth, d // LANES, LANES)
    w2 = jnp.transpose(w2, (0, 2, 1, 3)).reshape(c * per_col, LANES)
    w2 = jnp.pad(w2, ((0, (cp - c) * per_col), (0, 0)))
    return pl.pallas_call(
        functools.partial(_wprep_kernel, depth=depth), grid=(cp // LANES,),
        in_specs=[pl.BlockSpec((LANES * per_col, LANES), lambda j: (j, 0))],
        out_specs=pl.BlockSpec((depth, d, LANES), lambda j: (0, 0, j)),
        out_shape=jax.ShapeDtypeStruct((depth, d, cp), BF16),
        compiler_params=_cparams(("parallel",), 32), name="wprep")(w2)


def _layer_weights(l, norm_pre, norm_post, w_rm, gla_w_up, gla_b_gate, gla_norm, pool_w, pool_scale, fox_b_f, w_out):
    idx = [sum(SPLIT_SIZES[:i + 1]) for i in range(len(SPLIT_SIZES) - 1)]
    gq, gk, gv, glr, gg, pv, pg, fq, fk, fv, ff, fg = jnp.split(w_rm[l, :, :sum(SPLIT_SIZES)], idx, axis=1)
    zc = lambda n: jnp.zeros((D_MODEL, n), BF16)
    aux = jnp.concatenate([ff, zc(AUX_LR - N_HEADS), glr, zc(LANES - AUX_LR - GLA_LR)], axis=1)
    w = jnp.concatenate([gq, zc(COL_K - QK_W), gk, zc(COL_GV - COL_K - QK_W), gv, gg, pg, fg, pv, fq, fk, fv, aux], axis=1)
    w_up = jnp.zeros((LANES, QK_W), F32).at[AUX_LR:AUX_LR + GLA_LR].set(gla_w_up[l])
    bf = jnp.zeros((1, LANES), F32).at[0, AUX_FF:AUX_FF + N_HEADS].set(fox_b_f[l])
    pw = jnp.zeros((POOL_W, POOL_W), F32)
    for gi in range(len(POOL_WINDOWS)):
        sl = slice(gi * POOL_GROUP, (gi + 1) * POOL_GROUP)
        pw = pw.at[sl, sl].set(pool_w[l, gi])
    return {"pre": norm_pre[l][None], "post": norm_post[l][None], "w_in": w, "w_up": w_up.astype(BF16),
            "b_gate": gla_b_gate[l][None], "bf": bf, "gla_norm": jnp.tile(gla_norm[l], N_HEADS)[None],
            "pool_w": pw.astype(BF16), "pool_scale": pool_scale[l][None], "w_out": w_out[l].astype(BF16)}


def _constants():
    i = np.arange(HEADS_W)
    gmat = np.where((i[:, None] // HEAD_DIM) == (i[None, :] // HEAD_DIM), 1.0 / HEAD_DIM, 0.0)
    r = np.arange(PAGE)
    tri = r[:, None] >= r[None, :]
    a = np.arange(LANES)
    place = np.stack([(a[:, None] < N_HEADS) & (a[None, :] == a[:, None] + off) for off in (0, 6, 12, 18, 24, 30)])
    sw = np.concatenate([r[:, None] > r[None, :], np.ones((PAGE, PAGE), bool)], axis=1)
    bf = lambda x: jnp.asarray(x.astype(np.float32), dtype=BF16)
    return {"gmat": bf(gmat), "tri": bf(tri), "place": bf(place), "sw": bf(sw)}


def _state_from_bd(st):
    B = st.shape[0]
    s = st.reshape(B, N_HEADS, HEAD_DIM, N_HEADS, GLA_DK)
    s = jnp.stack([s[:, h, :, h, :] for h in range(N_HEADS)], axis=1)
    return jnp.swapaxes(s, 2, 3)


def _sequence_layer(h, lw, consts, prefix, tm, tq, chunk):
    n_pre = 0 if prefix is None else prefix["k"].shape[0]
    x = _inproj(h, lw, tm, prefix)
    s0 = jnp.zeros((HEADS_W, QK_W), F32) if prefix is None else prefix["state"]
    hist0 = jnp.zeros((HIST_ROWS, POOL_W), F32) if prefix is None else prefix["hist"]
    og, op, st = _glapool(x, lw, consts, tm, chunk, n_pre, s0, hist0)
    h_new = _fox_out(x, h, og, op, lw, consts, tq, n_pre)
    return h_new, x, st


def _sample_layer(l, h, lw, consts, page_table, rt5, cache_kt, cache_vt, state2, hist16, nb, ts, tm, nseq):
    past = page_table.shape[1] * PAGE
    x = _inproj(h, lw, tm)
    flat = {n: x[n][0] for n in ("q", "k", "g", "pv")}
    flat["v"] = x["v"][0].astype(F32)
    og, op, s_new = _sample_mix(flat, state2, hist16, lw, consts, nseq, ts, past)
    per_seq = lambda a, w: a.reshape(nb, ts, w)
    of = _sample_fox(l, page_table, per_seq(x["fq"][0].astype(F32), HEADS_W), per_seq(x["fk"][0], HEADS_W),
                     per_seq(x["fv"][0], HEADS_W), per_seq(x["aux"][0], LANES), rt5, cache_kt, cache_vt)
    h_new = _outproj(h, og[None], op[None], of.reshape(1, nb * ts, HEADS_W), x["gates"], lw, tm)
    return h_new, x, s_new


def _largest_tile(n, cap):
    t = min(n, cap)
    while n % t:
        t -= 8
    return t


def kernel(x_prompt, x_sample, cache_fox_k, cache_fox_v, cache_fox_logf, state_gla, state_pool, page_table,
           meta_tokens, norm_pre, norm_post, w_in, gla_w_up, gla_b_gate, gla_norm, pool_w, pool_scale,
           fox_b_f, w_out):
    B, T, _ = x_prompt.shape
    nb, ts, _ = x_sample.shape
    depth, n_phys = cache_fox_k.shape[:2]
    consts = _constants()
    cache_kt = jnp.transpose(cache_fox_k, (0, 1, 3, 4, 2))
    cache_vt = jnp.transpose(cache_fox_v, (0, 1, 3, 4, 2))
    lf2 = jnp.transpose(cache_fox_logf, (0, 3, 1, 2)).reshape(depth * N_HEADS * n_phys, PAGE)
    rt5 = _lfsum(lf2, consts["sw"], _largest_tile(depth * N_HEADS * n_phys, 512))
    state2 = state_gla.reshape(depth, nb, QK_W, HEAD_DIM)
    state2 = jnp.concatenate([state2, state2], axis=-1)
    hist16 = jnp.pad(state_pool, ((0, 0), (0, 0), (HIST_ROWS - POOL_HIST, 0), (0, 0)))
    tm_p, tm_s = _largest_tile(T, 256), _largest_tile(nb * ts, 512)
    w_rm = _wprep(w_in)

    h_m, h_p, h_s = meta_tokens[None], x_prompt, x_sample.reshape(1, nb * ts, D_MODEL)
    out = [[] for _ in range(10)]
    for l in range(depth):
        lw = _layer_weights(l, norm_pre, norm_post, w_rm, gla_w_up, gla_b_gate, gla_norm, pool_w, pool_scale, fox_b_f, w_out)
        h_m_new, xm, st_m = _sequence_layer(h_m, lw, consts, None, N_META, N_META, N_META)
        prefix = {"k": xm["fk"][0], "v": xm["fv"][0], "aux": xm["aux"][0], "state": st_m[0], "hist": xm["pv"][0]}
        h_p, xp, st_p = _sequence_layer(h_p, lw, consts, prefix, tm_p, tm_p, GLA_CHUNK)
        h_m = h_m_new
        h_s, xs, s_new = _sample_layer(l, h_s, lw, consts, page_table, rt5, cache_kt, cache_vt,
                                       state2[l], hist16[l], nb, ts, tm_s, 8)
        L = N_META + T
        out[0].append(xp["fk"].reshape(B, L, N_HEADS, HEAD_DIM))
        out[1].append(xp["fv"].reshape(B, L, N_HEADS, HEAD_DIM))
        out[2].append(xp["aux"][:, :, :N_HEADS])
        out[3].append(_state_from_bd(st_p))
        out[4].append(xp["pv"][:, T - POOL_HIST:])
        out[5].append(xs["fk"].reshape(nb, ts, N_HEADS, HEAD_DIM))
        out[6].append(xs["fv"].reshape(nb, ts, N_HEADS, HEAD_DIM))
        out[7].append(xs["aux"][0, :, :N_HEADS].reshape(nb, ts, N_HEADS))
        out[8].append(s_new[:, :, :HEAD_DIM].reshape(nb, N_HEADS, GLA_DK, HEAD_DIM))
        out[9].append(jnp.concatenate([state_pool[l], xs["pv"][0].reshape(nb, ts, POOL_W)], axis=1)[:, ts:])
    return (h_p, h_s.reshape(nb, ts, D_MODEL)) + tuple(jnp.stack(o) for o in out)
```

```python
import functools

import jax
import jax.numpy as jnp
import numpy as np
from jax import lax
from jax.experimental import pallas as pl
from jax.experimental.pallas import tpu as pltpu

F32 = jnp.float32
BF16 = jnp.bfloat16

D_MODEL = 1024
N_HEADS = 6
GLA_DK = 32
HEAD_DIM = 64
GLA_LR = 16
GLA_GATE_NORM = 16.0
GLA_CHUNK = 64
QK_W = N_HEADS * GLA_DK
HEADS_W = N_HEADS * HEAD_DIM
POOL_W = 256
POOL_GROUP = 64
POOL_WINDOWS = (2, 4, 8, 16)
POOL_HIST = 15
HIST_ROWS = 16
N_META = 16
PAGE = 128
EPS = 1e-6
LANES = 128
NEG = -1e30

COL_Q, COL_K, COL_GV, COL_GATES, COL_PV = 0, 256, 512, 896, 1920
COL_FQ, COL_FK, COL_FV, COL_AUX, W_IN_COLS = 2176, 2560, 2944, 3328, 3456
GATES_W = HEADS_W + POOL_W + HEADS_W
AUX_FF, AUX_LR = 0, 8
SPLIT_SIZES = (QK_W, QK_W, HEADS_W, GLA_LR, HEADS_W, POOL_W, POOL_W, HEADS_W, HEADS_W, HEADS_W, N_HEADS, HEADS_W)

AUXK_ONES = (18, 36)


def _cparams(sem, vmem_mb):
    return pltpu.CompilerParams(dimension_semantics=sem, vmem_limit_bytes=vmem_mb * 1024 * 1024)


def _const_spec(shape):
    return pl.BlockSpec(shape, lambda *_: (0,) * len(shape))


def _dot(a, b):
    return jnp.dot(a, b, preferred_element_type=F32)


def _dot_nt(a, b):
    return lax.dot_general(a, b, (((1,), (1,)), ((), ())), preferred_element_type=F32)


def _dot_tn(a, b):
    return lax.dot_general(a, b, (((0,), (0,)), ((), ())), preferred_element_type=F32)


def _log_sigmoid(x):
    return jnp.minimum(x, 0.0) - jnp.log1p(jnp.exp(-jnp.abs(x)))


def _silu(x):
    return x * (1.0 / (1.0 + jnp.exp(-x)))


def _split3(x):
    hi = x.astype(BF16)
    r = x - hi.astype(F32)
    mid = r.astype(BF16)
    lo = (r - mid.astype(F32)).astype(BF16)
    return hi, mid, lo


def _cumsum_rows(x):
    n = x.shape[0]
    row = lax.broadcasted_iota(jnp.int32, x.shape, 0)
    k = 1
    while k < n:
        x = x + jnp.where(row >= k, pltpu.roll(x, k, axis=0), 0.0)
        k *= 2
    return x


def _inproj_kernel(x_ref, pre_ref, w_ref, wup_ref, bg_ref, bf_ref,
                   q_ref, k_ref, g_ref, v_ref, gates_ref, pv_ref, fq_ref, fk_ref, fv_ref, aux_ref):
    x = x_ref[0]
    xn = x * lax.rsqrt(jnp.mean(x * x, axis=-1, keepdims=True) + EPS) * pre_ref[...]
    xb = xn.astype(BF16)

    def seg(c0, width):
        return _dot(xb, w_ref[:, c0:c0 + width])

    q_ref[0] = seg(COL_Q, QK_W) * (GLA_DK ** -0.5)
    k_ref[0] = seg(COL_K, QK_W)
    v_ref[0] = seg(COL_GV, HEADS_W).astype(BF16)
    gates_ref[0] = seg(COL_GATES, GATES_W)
    pv_ref[0] = seg(COL_PV, POOL_W)
    fq_ref[0] = (seg(COL_FQ, HEADS_W) * (HEAD_DIM ** -0.5)).astype(BF16)
    aux = seg(COL_AUX, LANES)
    g_ref[0] = _log_sigmoid(_dot(aux.astype(BF16), wup_ref[...]) + bg_ref[...]) * (1.0 / GLA_GATE_NORM)
    fk_ref[0] = seg(COL_FK, HEADS_W)
    fv_ref[0] = seg(COL_FV, HEADS_W)
    aux_ref[0] = _log_sigmoid(aux + bf_ref[...])


def _inproj(h, lw, tm):
    B, T, _ = h.shape
    tile = lambda w: pl.BlockSpec((1, tm, w), lambda b, t: (b, t, 0))
    in_specs = [tile(D_MODEL), _const_spec((1, D_MODEL)), _const_spec((D_MODEL, W_IN_COLS)),
                _const_spec((LANES, QK_W)), _const_spec((1, QK_W)), _const_spec((1, LANES))]
    widths = (("q", QK_W, F32), ("k", QK_W, F32), ("g", QK_W, F32), ("v", HEADS_W, BF16), ("gates", GATES_W, F32),
              ("pv", POOL_W, F32), ("fq", HEADS_W, BF16), ("fk", HEADS_W, F32), ("fv", HEADS_W, F32), ("aux", LANES, F32))
    outs = pl.pallas_call(
        _inproj_kernel, grid=(B, T // tm), in_specs=in_specs,
        out_specs=[tile(w) for _, w, _ in widths],
        out_shape=[jax.ShapeDtypeStruct((B, T, w), dt) for _, w, dt in widths],
        compiler_params=_cparams(("parallel", "parallel"), 56), name="inproj")(
            h, lw["pre"], lw["w_in"], lw["w_up"], lw["b_gate"], lw["bf"])
    return dict(zip([n for n, _, _ in widths], outs))


def _gla_masks(chunk):
    n = N_HEADS * chunk
    bd_k = (lax.broadcasted_iota(jnp.int32, (n, QK_W), 0) // chunk) == (lax.broadcasted_iota(jnp.int32, (n, QK_W), 1) // GLA_DK)
    bd_v = (lax.broadcasted_iota(jnp.int32, (n, HEADS_W), 0) // chunk) == (lax.broadcasted_iota(jnp.int32, (n, HEADS_W), 1) // HEAD_DIM)
    tril = (lax.broadcasted_iota(jnp.int32, (chunk, n), 1) % chunk) <= lax.broadcasted_iota(jnp.int32, (chunk, n), 0)
    return bd_k, bd_v, tril


def _gla_intra(q, k, g, v_bf, masks, chunk):
    bd_k, bd_v, tril = masks
    bcum = _cumsum_rows(g)
    qe_bf = (q * jnp.exp(bcum)).astype(BF16)
    ke = k * jnp.exp(-bcum)
    b_end = bcum[chunk - 1:chunk, :]
    kd_bf = (k * jnp.exp(b_end - bcum)).astype(BF16)
    if chunk % 16 == 0:
        ke_rows = jnp.where(bd_k, jnp.concatenate([ke.astype(BF16)] * N_HEADS, axis=0), jnp.zeros((), BF16))
        v_rows = jnp.where(bd_v, jnp.concatenate([v_bf] * N_HEADS, axis=0), jnp.zeros((), BF16))
    else:
        ke_rows = jnp.where(bd_k, jnp.concatenate([ke] * N_HEADS, axis=0), 0.0).astype(BF16)
        v_rows = jnp.where(bd_v, jnp.concatenate([v_bf.astype(F32)] * N_HEADS, axis=0), 0.0).astype(BF16)
    a = jnp.where(tril, _dot_nt(qe_bf, ke_rows), 0.0)
    o_intra = _dot(a.astype(BF16), v_rows)
    return o_intra, qe_bf, kd_bf, jnp.exp(b_end)


def _head_norm(o, gmat_bf, gn):
    o2 = o * o
    hi = o2.astype(BF16)
    lo = (o2 - hi.astype(F32)).astype(BF16)
    ms = _dot(hi, gmat_bf) + _dot(lo, gmat_bf)
    return o * lax.rsqrt(ms + EPS) * gn


def _pool_delta(hist, pv, pos0, tm):
    assert POOL_WINDOWS == (2, 4, 8, 16)
    ext = jnp.concatenate([hist, pv], axis=0)
    lane_g = lax.broadcasted_iota(jnp.int32, (1, POOL_W), 1) // POOL_GROUP
    pos = pos0 + lax.broadcasted_iota(jnp.int32, (tm, 1), 0)
    run = ext
    sums = jnp.zeros((tm, POOL_W), F32)
    cnt = jnp.zeros((tm, POOL_W), F32)
    for gi, w in enumerate(POOL_WINDOWS):
        run = run + pltpu.roll(run, w // 2, axis=0)
        sums = jnp.where(lane_g == gi, run[HIST_ROWS:, :], sums)
        cnt = jnp.where(lane_g == gi, jnp.minimum(pos + 1, w).astype(F32), cnt)
    return sums / cnt - pv, ext[tm:tm + HIST_ROWS, :]


def _seqmix_kernel(*refs, tm, chunk, n_pre):
    if n_pre:
        (x_ref, pre_ref, w_ref, wup_ref, bg_ref, bf_ref, kpre_ref, vpre_ref, apre_ref, s0_ref, hist0_ref, gn_ref, gmat_ref,
         pw_ref, ps_ref, gates_ref, fq_ref, fk_ref, fv_ref, aux_ref, og_ref, op_ref, sout_ref, hout_ref, st_ref, ext_ref) = refs
    else:
        (x_ref, pre_ref, w_ref, wup_ref, bg_ref, bf_ref, s0_ref, hist0_ref, gn_ref, gmat_ref,
         pw_ref, ps_ref, gates_ref, fq_ref, fk_ref, fv_ref, aux_ref, og_ref, op_ref, sout_ref, hout_ref, st_ref, ext_ref) = refs
    t = pl.program_id(1)

    @pl.when(t == 0)
    def _():
        st_ref[...] = s0_ref[...]
        ext_ref[...] = hist0_ref[...]
        if n_pre:
            fk_ref[0, 0:n_pre, :] = kpre_ref[...]
            fv_ref[0, 0:n_pre, :] = vpre_ref[...]
            aux_ref[0, 0:n_pre, :] = apre_ref[...]

    x = x_ref[0]
    xn = x * lax.rsqrt(jnp.mean(x * x, axis=-1, keepdims=True) + EPS) * pre_ref[...]
    xb = xn.astype(BF16)

    def seg(c0, width):
        return _dot(xb, w_ref[:, c0:c0 + width])

    gates_ref[0] = seg(COL_GATES, GATES_W)
    fq_ref[0] = (seg(COL_FQ, HEADS_W) * (HEAD_DIM ** -0.5)).astype(BF16)
    aux = seg(COL_AUX, LANES)
    rows = pl.ds(pl.multiple_of(n_pre + t * tm, 8), tm)
    fk_ref[0, rows, :] = seg(COL_FK, HEADS_W)
    fv_ref[0, rows, :] = seg(COL_FV, HEADS_W)
    aux_ref[0, rows, :] = _log_sigmoid(aux + bf_ref[...])
    q = seg(COL_Q, QK_W) * (GLA_DK ** -0.5)
    k = seg(COL_K, QK_W)
    v_bf = seg(COL_GV, HEADS_W).astype(BF16)
    g = _log_sigmoid(_dot(aux.astype(BF16), wup_ref[...]) + bg_ref[...]) * (1.0 / GLA_GATE_NORM)

    masks = _gla_masks(chunk)
    bd = (lax.broadcasted_iota(jnp.int32, (HEADS_W, QK_W), 0) // HEAD_DIM) == (lax.broadcasted_iota(jnp.int32, (HEADS_W, QK_W), 1) // GLA_DK)
    st = st_ref[...]
    outs = []
    for c in range(tm // chunk):
        r = slice(c * chunk, (c + 1) * chunk)
        o_intra, qe_bf, kd_bf, decay = _gla_intra(q[r], k[r], g[r], v_bf[r], masks, chunk)
        outs.append(o_intra + _dot_nt(qe_bf, st.astype(BF16)))
        st = st * decay + jnp.where(bd, _dot_tn(v_bf[r], kd_bf), 0.0)
    st_ref[...] = st
    o = outs[0] if len(outs) == 1 else jnp.concatenate(outs, axis=0)
    og_ref[0] = _head_norm(o, gmat_ref[...], gn_ref[...])
    d, ext_ref[...] = _pool_delta(ext_ref[...], seg(COL_PV, POOL_W), n_pre + t * tm, tm)
    op_ref[0] = _dot(d.astype(BF16), pw_ref[...]) * ps_ref[...]

    @pl.when(t == pl.num_programs(1) - 1)
    def _():
        sout_ref[0] = st
        hout_ref[0] = ext_ref[...]


def _seqmix(h, lw, consts, prefix, tm, chunk):
    B, T, _ = h.shape
    n_pre = 0 if prefix is None else prefix["k"].shape[0]
    L = n_pre + T
    tile = lambda w: pl.BlockSpec((1, tm, w), lambda b, t: (b, t, 0))
    whole = lambda w: pl.BlockSpec((1, L, w), lambda b, t: (b, 0, 0))
    per_seq = lambda r, w: pl.BlockSpec((1, r, w), lambda b, t: (b, 0, 0))
    in_specs = [tile(D_MODEL), _const_spec((1, D_MODEL)), _const_spec((D_MODEL, W_IN_COLS)),
                _const_spec((LANES, QK_W)), _const_spec((1, QK_W)), _const_spec((1, LANES))]
    args = [h, lw["pre"], lw["w_in"], lw["w_up"], lw["b_gate"], lw["bf"]]
    if n_pre:
        in_specs += [_const_spec((n_pre, HEADS_W)), _const_spec((n_pre, HEADS_W)), _const_spec((n_pre, LANES))]
        args += [prefix["k"], prefix["v"], prefix["aux"]]
    in_specs += [_const_spec((HEADS_W, QK_W)), _const_spec((HIST_ROWS, POOL_W)), _const_spec((1, HEADS_W)),
                 _const_spec((HEADS_W, HEADS_W)), _const_spec((POOL_W, POOL_W)), _const_spec((1, POOL_W))]
    s0 = jnp.zeros((HEADS_W, QK_W), F32) if prefix is None else prefix["state"]
    hist0 = jnp.zeros((HIST_ROWS, POOL_W), F32) if prefix is None else prefix["hist"]
    args += [s0, hist0, lw["gla_norm"], consts["gmat"], lw["pool_w"], lw["pool_scale"]]
    out_shape = [jax.ShapeDtypeStruct((B, T, GATES_W), F32), jax.ShapeDtypeStruct((B, T, HEADS_W), BF16),
                 jax.ShapeDtypeStruct((B, L, HEADS_W), F32), jax.ShapeDtypeStruct((B, L, HEADS_W), F32),
                 jax.ShapeDtypeStruct((B, L, LANES), F32), jax.ShapeDtypeStruct((B, T, HEADS_W), F32),
                 jax.ShapeDtypeStruct((B, T, POOL_W), F32), jax.ShapeDtypeStruct((B, HEADS_W, QK_W), F32),
                 jax.ShapeDtypeStruct((B, HIST_ROWS, POOL_W), F32)]
    out_specs = [tile(GATES_W), tile(HEADS_W), whole(HEADS_W), whole(HEADS_W), whole(LANES), tile(HEADS_W), tile(POOL_W),
                 per_seq(HEADS_W, QK_W), per_seq(HIST_ROWS, POOL_W)]
    names = ("gates", "fq", "fk", "fv", "aux", "og", "op", "state", "hist")
    outs = pl.pallas_call(
        functools.partial(_seqmix_kernel, tm=tm, chunk=chunk, n_pre=n_pre),
        grid=(B, T // tm), in_specs=in_specs, out_specs=out_specs, out_shape=out_shape,
        scratch_shapes=[pltpu.VMEM((HEADS_W, QK_W), F32), pltpu.VMEM((HIST_ROWS, POOL_W), F32)],
        compiler_params=_cparams(("parallel", "arbitrary"), 56), name="seqmix")(*args)
    return dict(zip(names, outs))


def _fox_features(lf, carry, tri_bf, place_ref):
    r = lf.shape[0]
    lane = lax.broadcasted_iota(jnp.int32, (1, LANES), 1)
    tri = tri_bf[0:r, 0:r]
    hi, mid, lo = _split3(lf)
    f = _dot(tri, hi) + _dot(tri, mid) + _dot(tri, lo) + carry
    fh, fm, fl = _split3(f)
    ones_k = jnp.where((lane >= AUXK_ONES[0]) & (lane < AUXK_ONES[1]), 1.0, 0.0)
    ones_q = jnp.where(lane < AUXK_ONES[0], 1.0, 0.0)
    kaux = ones_k - (_dot(fh, place_ref[0]) + _dot(fm, place_ref[1]) + _dot(fl, place_ref[2]))
    qaux = ones_q + (_dot(fh, place_ref[3]) + _dot(fm, place_ref[4]) + _dot(fl, place_ref[5]))
    return kaux.astype(BF16), qaux.astype(BF16), f[r - 1:r, :]


def _fox_kernel(fq_ref, k_ref, v_ref, aux_ref, tri_ref, place_ref, h_ref, og_ref, op_ref, gates_ref, wo_ref, post_ref,
                out_ref, kb_ref, vb_ref, kaux_ref, qaux_ref, q2_ref, m_ref, acc_ref, sa_ref, sb_ref, *, T, n_pre, tq):
    t = pl.program_id(1)

    @pl.when(t == 0)
    def _():
        kb_ref[...] = k_ref[0].astype(BF16)
        vb_ref[...] = v_ref[0].astype(BF16)
        cb = min(PAGE, T)
        blocks = ([(0, n_pre)] if n_pre else []) + [(n_pre + i * cb, cb) for i in range(T // cb)]
        carry = jnp.zeros((1, LANES), F32)
        for r0, r in blocks:
            ka, qa, carry = _fox_features(aux_ref[0, r0:r0 + r, :], carry, tri_ref[...], place_ref)
            kaux_ref[r0:r0 + r, :] = ka
            qaux_ref[r0:r0 + r, :] = qa

    lane = lax.broadcasted_iota(jnp.int32, (1, LANES), 1)
    q0 = pl.multiple_of(n_pre + t * tq, 16)
    qa = qaux_ref[pl.ds(q0, tq), :]
    fq = fq_ref[0]
    zero_bf = jnp.zeros((tq, LANES), BF16)
    for h in range(N_HEADS):
        fq_p = fq[:, (h // 2) * LANES:(h // 2 + 1) * LANES]
        qmask = (lane == h) | (lane == 6 + h) | (lane == 12 + h) | (lane == 18 + h) | (lane == 24 + h) | (lane == 30 + h)
        q2_ref[h] = jnp.concatenate([jnp.where((lane // HEAD_DIM) == (h % 2), fq_p, zero_bf), jnp.where(qmask, qa, zero_bf)], axis=1)
    m_ref[...] = jnp.full(m_ref.shape, NEG, F32)
    acc_ref[...] = jnp.zeros(acc_ref.shape, F32)

    def logits(j, n, s_out):
        r0 = 0 if j is None else pl.multiple_of(n_pre + j * tq, 16)
        kaux = kaux_ref[pl.ds(r0, n), :]
        for p in range(N_HEADS // 2):
            k2 = jnp.concatenate([kb_ref[pl.ds(r0, n), p * LANES:(p + 1) * LANES], kaux], axis=1)
            for h in (2 * p, 2 * p + 1):
                s_out[h] = _dot_nt(q2_ref[h], k2)

    def update(j, n, s_in, causal):
        r0 = 0 if j is None else pl.multiple_of(n_pre + j * tq, 16)
        ones = jnp.ones((n, LANES), BF16)
        if causal:
            keep = lax.broadcasted_iota(jnp.int32, (tq, n), 0) >= lax.broadcasted_iota(jnp.int32, (tq, n), 1)
        for p in range(N_HEADS // 2):
            vp = vb_ref[pl.ds(r0, n), p * LANES:(p + 1) * LANES]
            for h in (2 * p, 2 * p + 1):
                v2 = jnp.where((lane // HEAD_DIM) == (h % 2), vp, ones)
                def read(z):
                    s = s_in[h] if j is None else s_in[h + jnp.minimum(z, 0)]
                    return jnp.where(keep, s, NEG) if causal else s
                m_old = m_ref[h]
                m_new = jnp.maximum(m_old, jnp.max(read(t), axis=-1, keepdims=True))
                alpha = jnp.exp(m_old - m_new)
                m_b = m_new[:, :n] if n < LANES else jnp.concatenate([m_new] * (n // LANES), axis=1)
                pe = jnp.exp(read(pl.program_id(0)) - m_b).astype(BF16)
                acc_ref[h] = alpha * acc_ref[h] + _dot(pe, v2)
                m_ref[h] = m_new

    if n_pre:
        pre = {}
        logits(None, n_pre, pre)
        update(None, n_pre, pre, False)

    logits(0, tq, sa_ref)

    def body(jj, c):
        j = 2 * jj
        logits(j + 1, tq, sb_ref)
        update(j, tq, sa_ref, False)
        logits(j + 2, tq, sa_ref)
        update(j + 1, tq, sb_ref, False)
        return c

    lax.fori_loop(0, t // 2, body, 0)

    @pl.when(t % 2 == 0)
    def _():
        update(t, tq, sa_ref, True)

    @pl.when(t % 2 == 1)
    def _():
        logits(t, tq, sb_ref)
        update(t - 1, tq, sa_ref, False)
        update(t, tq, sb_ref, True)

    o_fox = []
    for p in range(N_HEADS // 2):
        a0, a1 = acc_ref[2 * p], acc_ref[2 * p + 1]
        o_fox.append(jnp.where((lane // HEAD_DIM) == 0, a0 / pltpu.roll(a0, HEAD_DIM, axis=1),
                               a1 / pltpu.roll(a1, HEAD_DIM, axis=1)))
    y_in = jnp.concatenate([og_ref[0], op_ref[0]] + o_fox, axis=1) * _silu(gates_ref[0])
    y = _dot(y_in.astype(BF16), wo_ref[...])
    out_ref[0] = h_ref[0] + y * lax.rsqrt(jnp.mean(y * y, axis=-1, keepdims=True) + EPS) * post_ref[...]


def _fox_out(x, h, og, op, lw, consts, tq, n_pre):
    B, T, _ = x["fq"].shape
    L = n_pre + T
    whole = lambda w: pl.BlockSpec((1, L, w), lambda b, t: (b, 0, 0))
    tile = lambda w: pl.BlockSpec((1, tq, w), lambda b, t: (b, t, 0))
    in_specs = [tile(HEADS_W), whole(HEADS_W), whole(HEADS_W), whole(LANES),
                _const_spec((PAGE, PAGE)), _const_spec((6, LANES, LANES)),
                tile(D_MODEL), tile(HEADS_W), tile(POOL_W), tile(GATES_W), _const_spec((GATES_W, D_MODEL)), _const_spec((1, D_MODEL))]
    return pl.pallas_call(
        functools.partial(_fox_kernel, T=T, n_pre=n_pre, tq=tq),
        grid=(B, T // tq), in_specs=in_specs,
        out_specs=tile(D_MODEL),
        out_shape=jax.ShapeDtypeStruct((B, T, D_MODEL), F32),
        scratch_shapes=[pltpu.VMEM((L, HEADS_W), BF16), pltpu.VMEM((L, HEADS_W), BF16),
                        pltpu.VMEM((L, LANES), BF16), pltpu.VMEM((L, LANES), BF16),
                        pltpu.VMEM((N_HEADS, tq, 2 * LANES), BF16), pltpu.VMEM((N_HEADS, tq, LANES), F32),
                        pltpu.VMEM((N_HEADS, tq, LANES), F32),
                        pltpu.VMEM((N_HEADS, tq, tq), F32), pltpu.VMEM((N_HEADS, tq, tq), F32)],
        compiler_params=_cparams(("parallel", "arbitrary"), 48), name="fox")(
            x["fq"], x["fk"], x["fv"], x["aux"], consts["tri"], consts["place"],
            h, og, op, x["gates"], lw["w_out"], lw["post"])


def _outproj_kernel(h_ref, og_ref, op_ref, of_ref, gates_ref, wo_ref, post_ref, out_ref):
    y_in = jnp.concatenate([og_ref[0], op_ref[0], of_ref[0]], axis=1) * _silu(gates_ref[0])
    y = _dot(y_in.astype(BF16), wo_ref[...])
    out_ref[0] = h_ref[0] + y * lax.rsqrt(jnp.mean(y * y, axis=-1, keepdims=True) + EPS) * post_ref[...]


def _outproj(h, og, op, of, gates, lw, tm):
    B, T, _ = h.shape
    tile = lambda w: pl.BlockSpec((1, tm, w), lambda b, t: (b, t, 0))
    return pl.pallas_call(
        _outproj_kernel, grid=(B, T // tm),
        in_specs=[tile(D_MODEL), tile(HEADS_W), tile(POOL_W), tile(HEADS_W), tile(GATES_W),
                  _const_spec((GATES_W, D_MODEL)), _const_spec((1, D_MODEL))],
        out_specs=tile(D_MODEL), out_shape=jax.ShapeDtypeStruct((B, T, D_MODEL), F32),
        compiler_params=_cparams(("parallel", "parallel"), 48), name="outproj")(
            h, og, op, of, gates, lw["w_out"], lw["post"])


def _sample_mix_kernel(q_ref, k_ref, g_ref, v_ref, pv_ref, s_ref, hist_ref, gn_ref, gmat_ref, pw_ref, ps_ref,
                       og_ref, op_ref, sout_ref, *, nseq, ts, pos0):
    masks = _gla_masks(ts)
    rr = lax.broadcasted_iota(jnp.int32, (QK_W, HEADS_W), 0) // GLA_DK
    cc = lax.broadcasted_iota(jnp.int32, (QK_W, HEADS_W), 1) // HEAD_DIM
    bd = rr == cc
    ones = jnp.ones((3 * ts, LANES), BF16)
    outs, deltas = [], []
    for i in range(nseq):
        r = slice(i * ts, (i + 1) * ts)
        g = g_ref[r, :]
        v_bf = v_ref[r, :].astype(BF16)
        o_intra, qe_bf, kd_bf, _ = _gla_intra(q_ref[r, :], k_ref[r, :], g, v_bf, masks, ts)
        s2 = s_ref[i]
        s_bd = jnp.where(bd, jnp.concatenate([s2, s2, s2], axis=1), 0.0).astype(BF16)
        outs.append(o_intra + _dot(qe_bf, s_bd))
        u = jnp.where(bd, _dot_tn(kd_bf, v_bf), 0.0)
        x = jnp.concatenate([u[64 * j:64 * (j + 1), LANES * j:LANES * (j + 1)] for j in range(N_HEADS // 2)], axis=0)
        x = x + pltpu.roll(x, HEAD_DIM, axis=1)
        gh, gm, gl = _split3(g)
        gparts = jnp.concatenate([gh.astype(F32), gm.astype(F32), gl.astype(F32)], axis=0).astype(BF16)
        sout_ref[i] = s2 * jnp.exp(_dot_tn(gparts, ones)) + x
        deltas.append(_pool_delta(hist_ref[i], pv_ref[r, :], pos0, ts)[0])
    og_ref[...] = _head_norm(jnp.concatenate(outs, axis=0), gmat_ref[...], gn_ref[...])
    op_ref[...] = _dot(jnp.concatenate(deltas, axis=0).astype(BF16), pw_ref[...]) * ps_ref[...]


def _sample_mix(x, state2, hist16, lw, consts, nseq, ts, pos0):
    n = x["q"].shape[0]
    rows = nseq * ts
    tile = lambda w: pl.BlockSpec((rows, w), lambda i: (i, 0))
    in_specs = [tile(QK_W), tile(QK_W), tile(QK_W), tile(HEADS_W), tile(POOL_W),
                pl.BlockSpec((nseq, QK_W, LANES), lambda i: (i, 0, 0)), pl.BlockSpec((nseq, HIST_ROWS, POOL_W), lambda i: (i, 0, 0)),
                _const_spec((1, HEADS_W)), _const_spec((HEADS_W, HEADS_W)), _const_spec((POOL_W, POOL_W)), _const_spec((1, POOL_W))]
    out_shape = [jax.ShapeDtypeStruct((n, HEADS_W), F32), jax.ShapeDtypeStruct((n, POOL_W), F32),
                 jax.ShapeDtypeStruct(state2.shape, F32)]
    out_specs = [tile(HEADS_W), tile(POOL_W), pl.BlockSpec((nseq, QK_W, LANES), lambda i: (i, 0, 0))]
    return pl.pallas_call(
        functools.partial(_sample_mix_kernel, nseq=nseq, ts=ts, pos0=pos0),
        grid=(n // rows,), in_specs=in_specs, out_specs=out_specs, out_shape=out_shape,
        compiler_params=_cparams(("parallel",), 48), name="sample_mix")(
            x["q"], x["k"], x["g"], x["v"], x["pv"], state2, hist16, lw["gla_norm"], consts["gmat"], lw["pool_w"], lw["pool_scale"])


def _suffix_rows(x, period):
    n = x.shape[0]
    row = lax.broadcasted_iota(jnp.int32, x.shape, 0) % period
    k = 1
    while k < period:
        x = x + jnp.where(row + k < period, pltpu.roll(x, n - k, axis=0), 0.0)
        k *= 2
    return x


def _lfsum_kernel(lf_ref, sw_ref, o_ref):
    n = lf_ref.shape[0]
    hi, mid, lo = _split3(lf_ref[...])
    parts = jnp.concatenate([hi.astype(F32), mid.astype(F32), lo.astype(F32)], axis=0).astype(BF16)
    r3 = _dot(parts, sw_ref[...])
    o_ref[...] = r3[0:n] + r3[n:2 * n] + r3[2 * n:3 * n]


def _lfsum(lf2, sw, tr):
    n = lf2.shape[0]
    return pl.pallas_call(
        _lfsum_kernel, grid=(n // tr,),
        in_specs=[pl.BlockSpec((tr, PAGE), lambda i: (i, 0)), _const_spec((PAGE, 2 * PAGE))],
        out_specs=pl.BlockSpec((tr, 2 * PAGE), lambda i: (i, 0)), out_shape=jax.ShapeDtypeStruct((n, 2 * PAGE), F32),
        compiler_params=_cparams(("parallel",), 32), name="lfsum")(lf2, sw)


def _sample_fox_kernel(pt_ref, fq_ref, kn_ref, vn_ref, an_ref, rt_ref, ck_ref, cv_ref, o_ref,
                       kbuf, vbuf, ksem, vsem, *, n_pages, ts, n_phys, layer):
    b = pl.program_id(0)
    nb = pl.num_programs(0)
    slot = b % 2
    nxt = jnp.minimum(b + 1, nb - 1)

    def k_copy(seq, s, p):
        return pltpu.make_async_copy(ck_ref.at[layer, pt_ref[seq, p]], kbuf.at[s, p], ksem.at[s])

    def v_copy(seq, s, p):
        return pltpu.make_async_copy(cv_ref.at[layer, pt_ref[seq, p]], vbuf.at[s, p], vsem.at[s])

    @pl.when(b == 0)
    def _():
        for p in range(n_pages):
            k_copy(0, 0, p).start()
            v_copy(0, 0, p).start()

    for p in range(n_pages):
        k_copy(b, slot, p).wait()
    r2 = jnp.concatenate([rt_ref[pl.ds(h * n_phys + pt_ref[b, p], 1), :]
                          for h in range(N_HEADS) for p in range(n_pages)], axis=0)
    tot = r2[:, PAGE:]
    rfull = r2[:, :PAGE] + (_suffix_rows(tot, n_pages) - tot)
    cn = _cumsum_rows(an_ref[0])
    cn2 = jnp.concatenate([cn, cn], axis=0)
    cparts = _split3(cn)
    lane = lax.broadcasted_iota(jnp.int32, (2 * ts, LANES), 1)
    first = lax.broadcasted_iota(jnp.int32, (2 * ts, LANES), 0) < ts
    own = (lane < HEAD_DIM) == first
    causal = (lax.broadcasted_iota(jnp.int32, (2 * ts, ts), 0) % ts) >= lax.broadcasted_iota(jnp.int32, (2 * ts, ts), 1)
    for pr in range(N_HEADS // 2):
        cols = slice(pr * LANES, (pr + 1) * LANES)
        fq2 = jnp.concatenate([fq_ref[0, :, cols], fq_ref[0, :, cols]], axis=0)
        q2 = jnp.where(own, fq2, 0.0).astype(BF16)
        onehot = lane == jnp.where(first, 2 * pr, 2 * pr + 1)
        cq = jnp.sum(jnp.where(onehot, cn2, 0.0), axis=-1, keepdims=True)
        sel = jnp.where(onehot, 1.0, 0.0).astype(BF16)
        cn_t = _dot_nt(sel, cparts[0]) + _dot_nt(sel, cparts[1]) + _dot_nt(sel, cparts[2])
        s_new = jnp.where(causal, _dot_nt(q2, kn_ref[0, :, cols].astype(BF16)) + cq - cn_t, NEG)
        s_pages = []
        for p in range(n_pages):
            if pr == 0:
                k_copy(nxt, 1 - slot, p).start()
            kt = kbuf[slot, p, 2 * pr:2 * pr + 2].reshape(LANES, PAGE).astype(BF16)
            r0, r1 = 2 * pr * n_pages + p, (2 * pr + 1) * n_pages + p
            bias = jnp.concatenate([jnp.broadcast_to(rfull[r0:r0 + 1, :], (ts, PAGE)),
                                    jnp.broadcast_to(rfull[r1:r1 + 1, :], (ts, PAGE))], axis=0)
            s_pages.append(_dot(q2, kt) + bias + cq)
        mm = s_pages[0]
        for s in s_pages[1:]:
            mm = jnp.maximum(mm, s)
        m = jnp.maximum(jnp.max(mm, axis=-1, keepdims=True), jnp.max(s_new, axis=-1, keepdims=True))
        p_new = jnp.exp(s_new - m)
        l = jnp.sum(p_new, axis=-1, keepdims=True)
        acc = _dot(p_new.astype(BF16), vn_ref[0, :, cols].astype(BF16))
        if pr == 0:
            for p in range(n_pages):
                v_copy(b, slot, p).wait()
        for p in range(n_pages):
            if pr == 0:
                v_copy(nxt, 1 - slot, p).start()
            pe = jnp.exp(s_pages[p] - m)
            l = l + jnp.sum(pe, axis=-1, keepdims=True)
            vt = vbuf[slot, p, 2 * pr:2 * pr + 2].reshape(LANES, PAGE).astype(BF16)
            acc = acc + _dot_nt(pe.astype(BF16), vt)
        o2 = acc / l
        o_ref[0, :, cols] = jnp.where(lane[:ts] < HEAD_DIM, o2[:ts], o2[ts:])

    @pl.when(b == nb - 1)
    def _():
        for p in range(n_pages):
            k_copy(nxt, 1 - slot, p).wait()
            v_copy(nxt, 1 - slot, p).wait()


def _sample_fox(layer, page_table, fq, kn, vn, an, rt2, cache_kt, cache_vt):
    nb, ts, _ = fq.shape
    n_pages = page_table.shape[1]
    n_phys = cache_kt.shape[1]
    seq = lambda w: pl.BlockSpec((1, ts, w), lambda b, pt: (b, 0, 0))
    rt_spec = pl.BlockSpec((N_HEADS * n_phys, 2 * PAGE), lambda b, pt: (layer, 0), pipeline_mode=pl.Buffered(1))
    hbm = pl.BlockSpec(memory_space=pl.ANY)
    page_buf = pltpu.VMEM((2, n_pages, N_HEADS, HEAD_DIM, PAGE), F32)
    grid_spec = pltpu.PrefetchScalarGridSpec(
        num_scalar_prefetch=1, grid=(nb,),
        in_specs=[seq(HEADS_W), seq(HEADS_W), seq(HEADS_W), seq(LANES), rt_spec, hbm, hbm],
        out_specs=pl.BlockSpec((1, ts, HEADS_W), lambda b, pt: (b, 0, 0)),
        scratch_shapes=[page_buf, page_buf, pltpu.SemaphoreType.DMA((2,)), pltpu.SemaphoreType.DMA((2,))])
    return pl.pallas_call(
        functools.partial(_sample_fox_kernel, n_pages=n_pages, ts=ts, n_phys=n_phys, layer=layer), grid_spec=grid_spec,
        out_shape=jax.ShapeDtypeStruct((nb, ts, HEADS_W), F32),
        compiler_params=_cparams(("arbitrary",), 56), name="sample_fox")(
            page_table, fq, kn, vn, an, rt2, cache_kt, cache_vt)


def _wprep_kernel(w_ref, o_ref, *, depth):
    per_col = (D_MODEL // LANES) * depth
    for l in range(depth):
        for rt in range(D_MODEL // LANES):
            blk = w_ref[pl.ds(rt * depth + l, LANES, stride=per_col), :]
            o_ref[l, rt * LANES:(rt + 1) * LANES, :] = blk.T.astype(BF16)


def _wprep(w_in):
    depth, d, c = w_in.shape
    per_col = (d // LANES) * depth
    cp = -(-c // LANES) * LANES
    w2 = jnp.transpose(w_in, (2, 0, 1)).reshape(c, depth, d // LANES, LANES)
    w2 = jnp.transpose(w2, (0, 2, 1, 3)).reshape(c * per_col, LANES)
    w2 = jnp.pad(w2, ((0, (cp - c) * per_col), (0, 0)))
    return pl.pallas_call(
        functools.partial(_wprep_kernel, depth=depth), grid=(cp // LANES,),
        in_specs=[pl.BlockSpec((LANES * per_col, LANES), lambda j: (j, 0))],
        out_specs=pl.BlockSpec((depth, d, LANES), lambda j: (0, 0, j)),
        out_shape=jax.ShapeDtypeStruct((depth, d, cp), BF16),
        compiler_params=_cparams(("parallel",), 32), name="wprep")(w2)


def _layer_weights(l, norm_pre, norm_post, w_rm, gla_w_up, gla_b_gate, gla_norm, pool_w, pool_scale, fox_b_f, w_out):
    idx = [sum(SPLIT_SIZES[:i + 1]) for i in range(len(SPLIT_SIZES) - 1)]
    gq, gk, gv, glr, gg, pv, pg, fq, fk, fv, ff, fg = jnp.split(w_rm[l, :, :sum(SPLIT_SIZES)], idx, axis=1)
    zc = lambda n: jnp.zeros((D_MODEL, n), BF16)
    aux = jnp.concatenate([ff, zc(AUX_LR - N_HEADS), glr, zc(LANES - AUX_LR - GLA_LR)], axis=1)
    w = jnp.concatenate([gq, zc(COL_K - QK_W), gk, zc(COL_GV - COL_K - QK_W), gv, gg, pg, fg, pv, fq, fk, fv, aux], axis=1)
    w_up = jnp.zeros((LANES, QK_W), F32).at[AUX_LR:AUX_LR + GLA_LR].set(gla_w_up[l])
    bf = jnp.zeros((1, LANES), F32).at[0, AUX_FF:AUX_FF + N_HEADS].set(fox_b_f[l])
    pw = jnp.zeros((POOL_W, POOL_W), F32)
    for gi in range(len(POOL_WINDOWS)):
        sl = slice(gi * POOL_GROUP, (gi + 1) * POOL_GROUP)
        pw = pw.at[sl, sl].set(pool_w[l, gi])
    return {"pre": norm_pre[l][None], "post": norm_post[l][None], "w_in": w, "w_up": w_up.astype(BF16),
            "b_gate": gla_b_gate[l][None], "bf": bf, "gla_norm": jnp.tile(gla_norm[l], N_HEADS)[None],
            "pool_w": pw.astype(BF16), "pool_scale": pool_scale[l][None], "w_out": w_out[l].astype(BF16)}


def _constants():
    i = np.arange(HEADS_W)
    gmat = np.where((i[:, None] // HEAD_DIM) == (i[None, :] // HEAD_DIM), 1.0 / HEAD_DIM, 0.0)
    r = np.arange(PAGE)
    tri = r[:, None] >= r[None, :]
    a = np.arange(LANES)
    place = np.stack([(a[:, None] < N_HEADS) & (a[None, :] == a[:, None] + off) for off in (0, 6, 12, 18, 24, 30)])
    sw = np.concatenate([r[:, None] > r[None, :], np.ones((PAGE, PAGE), bool)], axis=1)
    bf = lambda x: jnp.asarray(x.astype(np.float32), dtype=BF16)
    return {"gmat": bf(gmat), "tri": bf(tri), "place": bf(place), "sw": bf(sw)}


def _state_from_bd(st):
    B = st.shape[0]
    s = st.reshape(B, N_HEADS, HEAD_DIM, N_HEADS, GLA_DK)
    s = jnp.stack([s[:, h, :, h, :] for h in range(N_HEADS)], axis=1)
    return jnp.swapaxes(s, 2, 3)


def _sequence_layer(h, lw, consts, prefix, tm, tq, chunk):
    n_pre = 0 if prefix is None else prefix["k"].shape[0]
    x = _seqmix(h, lw, consts, prefix, tm, chunk)
    h_new = _fox_out(x, h, x["og"], x["op"], lw, consts, tq, n_pre)
    return h_new, x


def _sample_layer(l, h, lw, consts, page_table, rt5, cache_kt, cache_vt, state2, hist16, nb, ts, tm, nseq):
    past = page_table.shape[1] * PAGE
    x = _inproj(h, lw, tm)
    flat = {n: x[n][0] for n in ("q", "k", "g", "pv")}
    flat["v"] = x["v"][0].astype(F32)
    og, op, s_new = _sample_mix(flat, state2, hist16, lw, consts, nseq, ts, past)
    per_seq = lambda a, w: a.reshape(nb, ts, w)
    of = _sample_fox(l, page_table, per_seq(x["fq"][0].astype(F32), HEADS_W), per_seq(x["fk"][0], HEADS_W),
                     per_seq(x["fv"][0], HEADS_W), per_seq(x["aux"][0], LANES), rt5, cache_kt, cache_vt)
    h_new = _outproj(h, og[None], op[None], of.reshape(1, nb * ts, HEADS_W), x["gates"], lw, tm)
    return h_new, x, s_new


def _largest_tile(n, cap):
    t = min(n, cap)
    while n % t:
        t -= 8
    return t


def kernel(x_prompt, x_sample, cache_fox_k, cache_fox_v, cache_fox_logf, state_gla, state_pool, page_table,
           meta_tokens, norm_pre, norm_post, w_in, gla_w_up, gla_b_gate, gla_norm, pool_w, pool_scale,
           fox_b_f, w_out):
    B, T, _ = x_prompt.shape
    nb, ts, _ = x_sample.shape
    depth, n_phys = cache_fox_k.shape[:2]
    consts = _constants()
    cache_kt = jnp.transpose(cache_fox_k, (0, 1, 3, 4, 2))
    cache_vt = jnp.transpose(cache_fox_v, (0, 1, 3, 4, 2))
    lf2 = jnp.transpose(cache_fox_logf, (0, 3, 1, 2)).reshape(depth * N_HEADS * n_phys, PAGE)
    rt5 = _lfsum(lf2, consts["sw"], _largest_tile(depth * N_HEADS * n_phys, 512))
    state2 = state_gla.reshape(depth, nb, QK_W, HEAD_DIM)
    state2 = jnp.concatenate([state2, state2], axis=-1)
    hist16 = jnp.pad(state_pool, ((0, 0), (0, 0), (HIST_ROWS - POOL_HIST, 0), (0, 0)))
    tm_p, tm_s = _largest_tile(T, 256), _largest_tile(nb * ts, 512)
    w_rm = _wprep(w_in)

    h_m, h_p, h_s = meta_tokens[None], x_prompt, x_sample.reshape(1, nb * ts, D_MODEL)
    out = [[] for _ in range(10)]
    for l in range(depth):
        lw = _layer_weights(l, norm_pre, norm_post, w_rm, gla_w_up, gla_b_gate, gla_norm, pool_w, pool_scale, fox_b_f, w_out)
        h_m_new, xm = _sequence_layer(h_m, lw, consts, None, N_META, N_META, N_META)
        prefix = {"k": xm["fk"][0], "v": xm["fv"][0], "aux": xm["aux"][0], "state": xm["state"][0], "hist": xm["hist"][0]}
        h_p, xp = _sequence_layer(h_p, lw, consts, prefix, tm_p, tm_p, GLA_CHUNK)
        h_m = h_m_new
        h_s, xs, s_new = _sample_layer(l, h_s, lw, consts, page_table, rt5, cache_kt, cache_vt,
                                       state2[l], hist16[l], nb, ts, tm_s, 8)
        L = N_META + T
        out[0].append(xp["fk"].reshape(B, L, N_HEADS, HEAD_DIM))
        out[1].append(xp["fv"].reshape(B, L, N_HEADS, HEAD_DIM))
        out[2].append(xp["aux"][:, :, :N_HEADS])
        out[3].append(_state_from_bd(xp["state"]))
        out[4].append(xp["hist"][:, HIST_ROWS - POOL_HIST:])
        out[5].append(xs["fk"].reshape(nb, ts, N_HEADS, HEAD_DIM))
        out[6].append(xs["fv"].reshape(nb, ts, N_HEADS, HEAD_DIM))
        out[7].append(xs["aux"][0, :, :N_HEADS].reshape(nb, ts, N_HEADS))
        out[8].append(s_new[:, :, :HEAD_DIM].reshape(nb, N_HEADS, GLA_DK, HEAD_DIM))
        out[9].append(jnp.concatenate([state_pool[l], xs["pv"][0].reshape(nb, ts, POOL_W)], axis=1)[:, ts:])
    return (h_p, h_s.reshape(nb, ts, D_MODEL)) + tuple(jnp.stack(o) for o in out)
```

```python
import functools

import jax
import jax.numpy as jnp
import numpy as np
from jax import lax
from jax.experimental import pallas as pl
from jax.experimental.pallas import tpu as pltpu

F32 = jnp.float32
BF16 = jnp.bfloat16

D_MODEL = 1024
N_HEADS = 6
GLA_DK = 32
HEAD_DIM = 64
GLA_LR = 16
GLA_GATE_NORM = 16.0
GLA_CHUNK = 64
QK_W = N_HEADS * GLA_DK
HEADS_W = N_HEADS * HEAD_DIM
POOL_W = 256
POOL_GROUP = 64
POOL_WINDOWS = (2, 4, 8, 16)
POOL_HIST = 15
HIST_ROWS = 16
N_META = 16
PAGE = 128
EPS = 1e-6
LANES = 128
NEG = -1e30

COL_Q, COL_K, COL_GV, COL_GATES, COL_PV = 0, 256, 512, 896, 1920
COL_FQ, COL_FK, COL_FV, COL_AUX, W_IN_COLS = 2176, 2560, 2944, 3328, 3456
GATES_W = HEADS_W + POOL_W + HEADS_W
AUX_FF, AUX_LR = 0, 8
SPLIT_SIZES = (QK_W, QK_W, HEADS_W, GLA_LR, HEADS_W, POOL_W, POOL_W, HEADS_W, HEADS_W, HEADS_W, N_HEADS, HEADS_W)

AUXK_ONES = (18, 36)


def _cparams(sem, vmem_mb):
    return pltpu.CompilerParams(dimension_semantics=sem, vmem_limit_bytes=vmem_mb * 1024 * 1024)


def _const_spec(shape):
    return pl.BlockSpec(shape, lambda *_: (0,) * len(shape))


def _dot(a, b):
    return jnp.dot(a, b, preferred_element_type=F32)


def _dot_nt(a, b):
    return lax.dot_general(a, b, (((1,), (1,)), ((), ())), preferred_element_type=F32)


def _dot_tn(a, b):
    return lax.dot_general(a, b, (((0,), (0,)), ((), ())), preferred_element_type=F32)


def _log_sigmoid(x):
    return jnp.minimum(x, 0.0) - jnp.log1p(jnp.exp(-jnp.abs(x)))


def _silu(x):
    return x * (1.0 / (1.0 + jnp.exp(-x)))


def _split3(x):
    hi = x.astype(BF16)
    r = x - hi.astype(F32)
    mid = r.astype(BF16)
    lo = (r - mid.astype(F32)).astype(BF16)
    return hi, mid, lo


def _cumsum_rows(x):
    n = x.shape[0]
    row = lax.broadcasted_iota(jnp.int32, x.shape, 0)
    k = 1
    while k < n:
        x = x + jnp.where(row >= k, pltpu.roll(x, k, axis=0), 0.0)
        k *= 2
    return x


def _inproj_kernel(x_ref, pre_ref, w_ref, wup_ref, bg_ref, bf_ref,
                   q_ref, k_ref, g_ref, v_ref, gates_ref, pv_ref, fq_ref, fk_ref, fv_ref, aux_ref):
    x = x_ref[0]
    xn = x * lax.rsqrt(jnp.mean(x * x, axis=-1, keepdims=True) + EPS) * pre_ref[...]
    xb = xn.astype(BF16)

    def seg(c0, width):
        return _dot(xb, w_ref[:, c0:c0 + width])

    q_ref[0] = seg(COL_Q, QK_W) * (GLA_DK ** -0.5)
    k_ref[0] = seg(COL_K, QK_W)
    v_ref[0] = seg(COL_GV, HEADS_W).astype(BF16)
    gates_ref[0] = seg(COL_GATES, GATES_W)
    pv_ref[0] = seg(COL_PV, POOL_W)
    fq_ref[0] = (seg(COL_FQ, HEADS_W) * (HEAD_DIM ** -0.5)).astype(BF16)
    aux = seg(COL_AUX, LANES)
    g_ref[0] = _log_sigmoid(_dot(aux.astype(BF16), wup_ref[...]) + bg_ref[...]) * (1.0 / GLA_GATE_NORM)
    fk_ref[0] = seg(COL_FK, HEADS_W)
    fv_ref[0] = seg(COL_FV, HEADS_W)
    aux_ref[0] = _log_sigmoid(aux + bf_ref[...])


def _inproj(h, lw, tm):
    B, T, _ = h.shape
    tile = lambda w: pl.BlockSpec((1, tm, w), lambda b, t: (b, t, 0))
    in_specs = [tile(D_MODEL), _const_spec((1, D_MODEL)), _const_spec((D_MODEL, W_IN_COLS)),
                _const_spec((LANES, QK_W)), _const_spec((1, QK_W)), _const_spec((1, LANES))]
    widths = (("q", QK_W, F32), ("k", QK_W, F32), ("g", QK_W, F32), ("v", HEADS_W, BF16), ("gates", GATES_W, F32),
              ("pv", POOL_W, F32), ("fq", HEADS_W, BF16), ("fk", HEADS_W, F32), ("fv", HEADS_W, F32), ("aux", LANES, F32))
    outs = pl.pallas_call(
        _inproj_kernel, grid=(B, T // tm), in_specs=in_specs,
        out_specs=[tile(w) for _, w, _ in widths],
        out_shape=[jax.ShapeDtypeStruct((B, T, w), dt) for _, w, dt in widths],
        compiler_params=_cparams(("parallel", "parallel"), 56), name="inproj")(
            h, lw["pre"], lw["w_in"], lw["w_up"], lw["b_gate"], lw["bf"])
    return dict(zip([n for n, _, _ in widths], outs))


def _gla_masks(chunk):
    n = N_HEADS * chunk
    bd_k = (lax.broadcasted_iota(jnp.int32, (n, QK_W), 0) // chunk) == (lax.broadcasted_iota(jnp.int32, (n, QK_W), 1) // GLA_DK)
    bd_v = (lax.broadcasted_iota(jnp.int32, (n, HEADS_W), 0) // chunk) == (lax.broadcasted_iota(jnp.int32, (n, HEADS_W), 1) // HEAD_DIM)
    tril = (lax.broadcasted_iota(jnp.int32, (chunk, n), 1) % chunk) <= lax.broadcasted_iota(jnp.int32, (chunk, n), 0)
    return bd_k, bd_v, tril


def _gla_intra(q, k, g, v_bf, masks, chunk):
    bd_k, bd_v, tril = masks
    bcum = _cumsum_rows(g)
    qe_bf = (q * jnp.exp(bcum)).astype(BF16)
    ke = k * jnp.exp(-bcum)
    b_end = bcum[chunk - 1:chunk, :]
    kd_bf = (k * jnp.exp(b_end - bcum)).astype(BF16)
    if chunk % 16 == 0:
        ke_rows = jnp.where(bd_k, jnp.concatenate([ke.astype(BF16)] * N_HEADS, axis=0), jnp.zeros((), BF16))
        v_rows = jnp.where(bd_v, jnp.concatenate([v_bf] * N_HEADS, axis=0), jnp.zeros((), BF16))
    else:
        ke_rows = jnp.where(bd_k, jnp.concatenate([ke] * N_HEADS, axis=0), 0.0).astype(BF16)
        v_rows = jnp.where(bd_v, jnp.concatenate([v_bf.astype(F32)] * N_HEADS, axis=0), 0.0).astype(BF16)
    a = jnp.where(tril, _dot_nt(qe_bf, ke_rows), 0.0)
    o_intra = _dot(a.astype(BF16), v_rows)
    return o_intra, qe_bf, kd_bf, jnp.exp(b_end)


def _head_norm(o, gmat_bf, gn):
    o2 = o * o
    hi = o2.astype(BF16)
    lo = (o2 - hi.astype(F32)).astype(BF16)
    ms = _dot(hi, gmat_bf) + _dot(lo, gmat_bf)
    return o * lax.rsqrt(ms + EPS) * gn


def _pool_delta(hist, pv, pos0, tm):
    assert POOL_WINDOWS == (2, 4, 8, 16)
    ext = jnp.concatenate([hist, pv], axis=0)
    lane_g = lax.broadcasted_iota(jnp.int32, (1, POOL_W), 1) // POOL_GROUP
    pos = pos0 + lax.broadcasted_iota(jnp.int32, (tm, 1), 0)
    run = ext
    sums = jnp.zeros((tm, POOL_W), F32)
    cnt = jnp.zeros((tm, POOL_W), F32)
    for gi, w in enumerate(POOL_WINDOWS):
        run = run + pltpu.roll(run, w // 2, axis=0)
        sums = jnp.where(lane_g == gi, run[HIST_ROWS:, :], sums)
        cnt = jnp.where(lane_g == gi, jnp.minimum(pos + 1, w).astype(F32), cnt)
    return sums / cnt - pv, ext[tm:tm + HIST_ROWS, :]


def _seqmix_kernel(*refs, T, tm, chunk, n_pre, aliased):
    if n_pre:
        (x_ref, pre_ref, w_ref, wup_ref, bg_ref, bf_ref, kpre_ref, vpre_ref, apre_ref, s0_ref, hist0_ref, gn_ref, gmat_ref,
         pw_ref, ps_ref) = refs[:15]
        refs = refs[15:]
    else:
        (x_ref, pre_ref, w_ref, wup_ref, bg_ref, bf_ref, s0_ref, hist0_ref, gn_ref, gmat_ref, pw_ref, ps_ref) = refs[:12]
        refs = refs[12:]
    if aliased:
        refs = refs[2:]
    (gates_ref, fq_ref, fk_ref, fv_ref, kt_ref, vt_ref, aux_ref, og_ref, op_ref, sout_ref, hout_ref,
     st_ref, ext_ref, kext_ref, vext_ref) = refs
    t = pl.program_id(1)

    @pl.when(t == 0)
    def _():
        st_ref[...] = s0_ref[...]
        ext_ref[...] = hist0_ref[...]
        if n_pre:
            fk_ref[0, 0:n_pre, :] = kpre_ref[...].astype(BF16)
            fv_ref[0, 0:n_pre, :] = vpre_ref[...].astype(BF16)
            aux_ref[0, 0:n_pre, :] = apre_ref[...]
            kext_ref[0:n_pre, :] = kpre_ref[...]
            vext_ref[0:n_pre, :] = vpre_ref[...]

    def transposed(x):
        r = x.shape[0]
        if r % LANES == 0:
            return x.T
        return jnp.concatenate([x, jnp.zeros((LANES - r, HEADS_W), F32)], axis=0).T[:, :r]

    def emit_transposed(out_ref, carry_ref, tile_rows):
        if n_pre:
            carry_ref[n_pre:n_pre + tm, :] = tile_rows
            tile_rows = carry_ref[0:tm, :]
        lanes = slice(0, tm) if T == tm else pl.ds(pl.multiple_of(t * tm, LANES), tm)
        out_ref[0, 0, :, lanes] = transposed(tile_rows)
        if n_pre:
            tail = carry_ref[tm:tm + n_pre, :]
            carry_ref[0:n_pre, :] = tail

            @pl.when(t == pl.num_programs(1) - 1)
            def _():
                out_ref[0, 0, :, T:T + n_pre] = transposed(tail)

    x = x_ref[0]
    xn = x * lax.rsqrt(jnp.mean(x * x, axis=-1, keepdims=True) + EPS) * pre_ref[...]
    xb = xn.astype(BF16)

    def seg(c0, width):
        return _dot(xb, w_ref[:, c0:c0 + width])

    gates_ref[0] = seg(COL_GATES, GATES_W)
    fq_ref[0] = (seg(COL_FQ, HEADS_W) * (HEAD_DIM ** -0.5)).astype(BF16)
    aux = seg(COL_AUX, LANES)
    rows = pl.ds(pl.multiple_of(n_pre + t * tm, 8), tm)
    fk = seg(COL_FK, HEADS_W)
    fv = seg(COL_FV, HEADS_W)
    fk_ref[0, rows, :] = fk.astype(BF16)
    fv_ref[0, rows, :] = fv.astype(BF16)
    emit_transposed(kt_ref, kext_ref, fk)
    emit_transposed(vt_ref, vext_ref, fv)
    aux_ref[0, rows, :] = _log_sigmoid(aux + bf_ref[...])
    q = seg(COL_Q, QK_W) * (GLA_DK ** -0.5)
    k = seg(COL_K, QK_W)
    v_bf = seg(COL_GV, HEADS_W).astype(BF16)
    g = _log_sigmoid(_dot(aux.astype(BF16), wup_ref[...]) + bg_ref[...]) * (1.0 / GLA_GATE_NORM)

    masks = _gla_masks(chunk)
    bd = (lax.broadcasted_iota(jnp.int32, (HEADS_W, QK_W), 0) // HEAD_DIM) == (lax.broadcasted_iota(jnp.int32, (HEADS_W, QK_W), 1) // GLA_DK)
    st = st_ref[...]
    outs = []
    for c in range(tm // chunk):
        r = slice(c * chunk, (c + 1) * chunk)
        o_intra, qe_bf, kd_bf, decay = _gla_intra(q[r], k[r], g[r], v_bf[r], masks, chunk)
        outs.append(o_intra + _dot_nt(qe_bf, st.astype(BF16)))
        st = st * decay + jnp.where(bd, _dot_tn(v_bf[r], kd_bf), 0.0)
    st_ref[...] = st
    o = outs[0] if len(outs) == 1 else jnp.concatenate(outs, axis=0)
    og_ref[0] = _head_norm(o, gmat_ref[...], gn_ref[...])
    d, ext_ref[...] = _pool_delta(ext_ref[...], seg(COL_PV, POOL_W), n_pre + t * tm, tm)
    op_ref[0] = _dot(d.astype(BF16), pw_ref[...]) * ps_ref[...]

    @pl.when(t == pl.num_programs(1) - 1)
    def _():
        sout_ref[0] = st
        hout_ref[0] = ext_ref[...]


def _seqmix(h, lw, consts, prefix, tm, chunk, layer=0, depth=1, kv_t=None):
    B, T, _ = h.shape
    n_pre = 0 if prefix is None else prefix["k"].shape[0]
    L = n_pre + T
    tile = lambda w: pl.BlockSpec((1, tm, w), lambda b, t: (b, t, 0))
    whole = lambda w: pl.BlockSpec((1, L, w), lambda b, t: (b, 0, 0))
    per_seq = lambda r, w: pl.BlockSpec((1, r, w), lambda b, t: (b, 0, 0))
    in_specs = [tile(D_MODEL), _const_spec((1, D_MODEL)), _const_spec((D_MODEL, W_IN_COLS)),
                _const_spec((LANES, QK_W)), _const_spec((1, QK_W)), _const_spec((1, LANES))]
    args = [h, lw["pre"], lw["w_in"], lw["w_up"], lw["b_gate"], lw["bf"]]
    if n_pre:
        in_specs += [_const_spec((n_pre, HEADS_W)), _const_spec((n_pre, HEADS_W)), _const_spec((n_pre, LANES))]
        args += [prefix["k"], prefix["v"], prefix["aux"]]
    in_specs += [_const_spec((HEADS_W, QK_W)), _const_spec((HIST_ROWS, POOL_W)), _const_spec((1, HEADS_W)),
                 _const_spec((HEADS_W, HEADS_W)), _const_spec((POOL_W, POOL_W)), _const_spec((1, POOL_W))]
    s0 = jnp.zeros((HEADS_W, QK_W), F32) if prefix is None else prefix["state"]
    hist0 = jnp.zeros((HIST_ROWS, POOL_W), F32) if prefix is None else prefix["hist"]
    args += [s0, hist0, lw["gla_norm"], consts["gmat"], lw["pool_w"], lw["pool_scale"]]
    aliases = {}
    if kv_t is not None:
        aliases = {len(args): 4, len(args) + 1: 5}
        in_specs += [pl.BlockSpec(memory_space=pl.ANY)] * 2
        args += list(kv_t)
    slab = pl.BlockSpec((1, 1, HEADS_W, L), lambda b, t: (layer, b, 0, 0))
    out_shape = [jax.ShapeDtypeStruct((B, T, GATES_W), F32), jax.ShapeDtypeStruct((B, T, HEADS_W), BF16),
                 jax.ShapeDtypeStruct((B, L, HEADS_W), BF16), jax.ShapeDtypeStruct((B, L, HEADS_W), BF16),
                 jax.ShapeDtypeStruct((depth, B, HEADS_W, L), F32), jax.ShapeDtypeStruct((depth, B, HEADS_W, L), F32),
                 jax.ShapeDtypeStruct((B, L, LANES), F32), jax.ShapeDtypeStruct((B, T, HEADS_W), F32),
                 jax.ShapeDtypeStruct((B, T, POOL_W), F32), jax.ShapeDtypeStruct((B, HEADS_W, QK_W), F32),
                 jax.ShapeDtypeStruct((B, HIST_ROWS, POOL_W), F32)]
    out_specs = [tile(GATES_W), tile(HEADS_W), whole(HEADS_W), whole(HEADS_W), slab, slab, whole(LANES), tile(HEADS_W),
                 tile(POOL_W), per_seq(HEADS_W, QK_W), per_seq(HIST_ROWS, POOL_W)]
    names = ("gates", "fq", "fk", "fv", "kt", "vt", "aux", "og", "op", "state", "hist")
    outs = pl.pallas_call(
        functools.partial(_seqmix_kernel, T=T, tm=tm, chunk=chunk, n_pre=n_pre, aliased=kv_t is not None),
        grid=(B, T // tm), in_specs=in_specs, out_specs=out_specs, out_shape=out_shape,
        input_output_aliases=aliases,
        scratch_shapes=[pltpu.VMEM((HEADS_W, QK_W), F32), pltpu.VMEM((HIST_ROWS, POOL_W), F32),
                        pltpu.VMEM((tm + n_pre, HEADS_W), F32), pltpu.VMEM((tm + n_pre, HEADS_W), F32)],
        compiler_params=_cparams(("parallel", "arbitrary"), 56), name="seqmix")(*args)
    return dict(zip(names, outs))


def _fox_features(lf, carry, tri_bf, place_ref):
    r = lf.shape[0]
    lane = lax.broadcasted_iota(jnp.int32, (1, LANES), 1)
    tri = tri_bf[0:r, 0:r]
    hi, mid, lo = _split3(lf)
    f = _dot(tri, hi) + _dot(tri, mid) + _dot(tri, lo) + carry
    fh, fm, fl = _split3(f)
    ones_k = jnp.where((lane >= AUXK_ONES[0]) & (lane < AUXK_ONES[1]), 1.0, 0.0)
    ones_q = jnp.where(lane < AUXK_ONES[0], 1.0, 0.0)
    kaux = ones_k - (_dot(fh, place_ref[0]) + _dot(fm, place_ref[1]) + _dot(fl, place_ref[2]))
    qaux = ones_q + (_dot(fh, place_ref[3]) + _dot(fm, place_ref[4]) + _dot(fl, place_ref[5]))
    return kaux.astype(BF16), qaux.astype(BF16), f[r - 1:r, :]


def _fox_kernel(fq_ref, k_ref, v_ref, aux_ref, tri_ref, place_ref, h_ref, og_ref, op_ref, gates_ref, wo_ref, post_ref,
                out_ref, kaux_ref, qaux_ref, q2_ref, m_ref, acc_ref, sa_ref, sb_ref, *, T, n_pre, tq):
    t = pl.program_id(1)

    @pl.when(t == 0)
    def _():
        cb = min(PAGE, T)
        blocks = ([(0, n_pre)] if n_pre else []) + [(n_pre + i * cb, cb) for i in range(T // cb)]
        carry = jnp.zeros((1, LANES), F32)
        for r0, r in blocks:
            ka, qa, carry = _fox_features(aux_ref[0, r0:r0 + r, :], carry, tri_ref[...], place_ref)
            kaux_ref[r0:r0 + r, :] = ka
            qaux_ref[r0:r0 + r, :] = qa

    lane = lax.broadcasted_iota(jnp.int32, (1, LANES), 1)
    q0 = pl.multiple_of(n_pre + t * tq, 16)
    qa = qaux_ref[pl.ds(q0, tq), :]
    fq = fq_ref[0]
    zero_bf = jnp.zeros((tq, LANES), BF16)
    for h in range(N_HEADS):
        fq_p = fq[:, (h // 2) * LANES:(h // 2 + 1) * LANES]
        qmask = (lane == h) | (lane == 6 + h) | (lane == 12 + h) | (lane == 18 + h) | (lane == 24 + h) | (lane == 30 + h)
        q2_ref[h] = jnp.concatenate([jnp.where((lane // HEAD_DIM) == (h % 2), fq_p, zero_bf), jnp.where(qmask, qa, zero_bf)], axis=1)
    m_ref[...] = jnp.full(m_ref.shape, NEG, F32)
    acc_ref[...] = jnp.zeros(acc_ref.shape, F32)

    def logits(j, n, s_out):
        r0 = 0 if j is None else pl.multiple_of(n_pre + j * tq, 16)
        kaux = kaux_ref[pl.ds(r0, n), :]
        for p in range(N_HEADS // 2):
            k2 = jnp.concatenate([k_ref[0, pl.ds(r0, n), p * LANES:(p + 1) * LANES], kaux], axis=1)
            for h in (2 * p, 2 * p + 1):
                s_out[h] = _dot_nt(q2_ref[h], k2)

    def update(j, n, s_in, causal):
        r0 = 0 if j is None else pl.multiple_of(n_pre + j * tq, 16)
        ones = jnp.ones((n, LANES), BF16)
        if causal:
            keep = lax.broadcasted_iota(jnp.int32, (tq, n), 0) >= lax.broadcasted_iota(jnp.int32, (tq, n), 1)
        for p in range(N_HEADS // 2):
            vp = v_ref[0, pl.ds(r0, n), p * LANES:(p + 1) * LANES]
            for h in (2 * p, 2 * p + 1):
                v2 = jnp.where((lane // HEAD_DIM) == (h % 2), vp, ones)
                def read(z):
                    s = s_in[h] if j is None else s_in[h + jnp.minimum(z, 0)]
                    return jnp.where(keep, s, NEG) if causal else s
                m_old = m_ref[h]
                m_new = jnp.maximum(m_old, jnp.max(read(t), axis=-1, keepdims=True))
                alpha = jnp.exp(m_old - m_new)
                m_b = m_new[:, :n] if n < LANES else jnp.concatenate([m_new] * (n // LANES), axis=1)
                pe = jnp.exp(read(pl.program_id(0)) - m_b).astype(BF16)
                acc_ref[h] = alpha * acc_ref[h] + _dot(pe, v2)
                m_ref[h] = m_new

    if n_pre:
        pre = {}
        logits(None, n_pre, pre)
        update(None, n_pre, pre, False)

    logits(0, tq, sa_ref)

    def body(jj, c):
        j = 2 * jj
        logits(j + 1, tq, sb_ref)
        update(j, tq, sa_ref, False)
        logits(j + 2, tq, sa_ref)
        update(j + 1, tq, sb_ref, False)
        return c

    lax.fori_loop(0, t // 2, body, 0)

    @pl.when(t % 2 == 0)
    def _():
        update(t, tq, sa_ref, True)

    @pl.when(t % 2 == 1)
    def _():
        logits(t, tq, sb_ref)
        update(t - 1, tq, sa_ref, False)
        update(t, tq, sb_ref, True)

    o_fox = []
    for p in range(N_HEADS // 2):
        a0, a1 = acc_ref[2 * p], acc_ref[2 * p + 1]
        o_fox.append(jnp.where((lane // HEAD_DIM) == 0, a0 / pltpu.roll(a0, HEAD_DIM, axis=1),
                               a1 / pltpu.roll(a1, HEAD_DIM, axis=1)))
    y_in = jnp.concatenate([og_ref[0], op_ref[0]] + o_fox, axis=1) * _silu(gates_ref[0])
    y = _dot(y_in.astype(BF16), wo_ref[...])
    out_ref[0] = h_ref[0] + y * lax.rsqrt(jnp.mean(y * y, axis=-1, keepdims=True) + EPS) * post_ref[...]


def _fox_out(x, h, og, op, lw, consts, tq, n_pre):
    B, T, _ = x["fq"].shape
    L = n_pre + T
    whole = lambda w: pl.BlockSpec((1, L, w), lambda b, t: (b, 0, 0))
    tile = lambda w: pl.BlockSpec((1, tq, w), lambda b, t: (b, t, 0))
    in_specs = [tile(HEADS_W), whole(HEADS_W), whole(HEADS_W), whole(LANES),
                _const_spec((PAGE, PAGE)), _const_spec((6, LANES, LANES)),
                tile(D_MODEL), tile(HEADS_W), tile(POOL_W), tile(GATES_W), _const_spec((GATES_W, D_MODEL)), _const_spec((1, D_MODEL))]
    return pl.pallas_call(
        functools.partial(_fox_kernel, T=T, n_pre=n_pre, tq=tq),
        grid=(B, T // tq), in_specs=in_specs,
        out_specs=tile(D_MODEL),
        out_shape=jax.ShapeDtypeStruct((B, T, D_MODEL), F32),
        scratch_shapes=[pltpu.VMEM((L, LANES), BF16), pltpu.VMEM((L, LANES), BF16),
                        pltpu.VMEM((N_HEADS, tq, 2 * LANES), BF16), pltpu.VMEM((N_HEADS, tq, LANES), F32),
                        pltpu.VMEM((N_HEADS, tq, LANES), F32),
                        pltpu.VMEM((N_HEADS, tq, tq), F32), pltpu.VMEM((N_HEADS, tq, tq), F32)],
        compiler_params=_cparams(("parallel", "arbitrary"), 48), name="fox")(
            x["fq"], x["fk"], x["fv"], x["aux"], consts["tri"], consts["place"],
            h, og, op, x["gates"], lw["w_out"], lw["post"])


def _outproj_kernel(h_ref, og_ref, op_ref, of_ref, gates_ref, wo_ref, post_ref, out_ref):
    y_in = jnp.concatenate([og_ref[0], op_ref[0], of_ref[0]], axis=1) * _silu(gates_ref[0])
    y = _dot(y_in.astype(BF16), wo_ref[...])
    out_ref[0] = h_ref[0] + y * lax.rsqrt(jnp.mean(y * y, axis=-1, keepdims=True) + EPS) * post_ref[...]


def _outproj(h, og, op, of, gates, lw, tm):
    B, T, _ = h.shape
    tile = lambda w: pl.BlockSpec((1, tm, w), lambda b, t: (b, t, 0))
    return pl.pallas_call(
        _outproj_kernel, grid=(B, T // tm),
        in_specs=[tile(D_MODEL), tile(HEADS_W), tile(POOL_W), tile(HEADS_W), tile(GATES_W),
                  _const_spec((GATES_W, D_MODEL)), _const_spec((1, D_MODEL))],
        out_specs=tile(D_MODEL), out_shape=jax.ShapeDtypeStruct((B, T, D_MODEL), F32),
        compiler_params=_cparams(("parallel", "parallel"), 48), name="outproj")(
            h, og, op, of, gates, lw["w_out"], lw["post"])


def _sample_mix_kernel(q_ref, k_ref, g_ref, v_ref, pv_ref, s_ref, hist_ref, gn_ref, gmat_ref, pw_ref, ps_ref,
                       og_ref, op_ref, sout_ref, *, nseq, ts, pos0):
    masks = _gla_masks(ts)
    rr = lax.broadcasted_iota(jnp.int32, (QK_W, HEADS_W), 0) // GLA_DK
    cc = lax.broadcasted_iota(jnp.int32, (QK_W, HEADS_W), 1) // HEAD_DIM
    bd = rr == cc
    ones = jnp.ones((3 * ts, LANES), BF16)
    outs, deltas = [], []
    for i in range(nseq):
        r = slice(i * ts, (i + 1) * ts)
        g = g_ref[r, :]
        v_bf = v_ref[r, :].astype(BF16)
        o_intra, qe_bf, kd_bf, _ = _gla_intra(q_ref[r, :], k_ref[r, :], g, v_bf, masks, ts)
        s2 = s_ref[i]
        s_bd = jnp.where(bd, jnp.concatenate([s2, s2, s2], axis=1), 0.0).astype(BF16)
        outs.append(o_intra + _dot(qe_bf, s_bd))
        u = jnp.where(bd, _dot_tn(kd_bf, v_bf), 0.0)
        x = jnp.concatenate([u[64 * j:64 * (j + 1), LANES * j:LANES * (j + 1)] for j in range(N_HEADS // 2)], axis=0)
        x = x + pltpu.roll(x, HEAD_DIM, axis=1)
        gh, gm, gl = _split3(g)
        gparts = jnp.concatenate([gh.astype(F32), gm.astype(F32), gl.astype(F32)], axis=0).astype(BF16)
        sout_ref[i] = s2 * jnp.exp(_dot_tn(gparts, ones)) + x
        deltas.append(_pool_delta(hist_ref[i], pv_ref[r, :], pos0, ts)[0])
    og_ref[...] = _head_norm(jnp.concatenate(outs, axis=0), gmat_ref[...], gn_ref[...])
    op_ref[...] = _dot(jnp.concatenate(deltas, axis=0).astype(BF16), pw_ref[...]) * ps_ref[...]


def _sample_mix(x, state2, hist16, lw, consts, nseq, ts, pos0):
    n = x["q"].shape[0]
    rows = nseq * ts
    tile = lambda w: pl.BlockSpec((rows, w), lambda i: (i, 0))
    in_specs = [tile(QK_W), tile(QK_W), tile(QK_W), tile(HEADS_W), tile(POOL_W),
                pl.BlockSpec((nseq, QK_W, LANES), lambda i: (i, 0, 0)), pl.BlockSpec((nseq, HIST_ROWS, POOL_W), lambda i: (i, 0, 0)),
                _const_spec((1, HEADS_W)), _const_spec((HEADS_W, HEADS_W)), _const_spec((POOL_W, POOL_W)), _const_spec((1, POOL_W))]
    out_shape = [jax.ShapeDtypeStruct((n, HEADS_W), F32), jax.ShapeDtypeStruct((n, POOL_W), F32),
                 jax.ShapeDtypeStruct(state2.shape, F32)]
    out_specs = [tile(HEADS_W), tile(POOL_W), pl.BlockSpec((nseq, QK_W, LANES), lambda i: (i, 0, 0))]
    return pl.pallas_call(
        functools.partial(_sample_mix_kernel, nseq=nseq, ts=ts, pos0=pos0),
        grid=(n // rows,), in_specs=in_specs, out_specs=out_specs, out_shape=out_shape,
        compiler_params=_cparams(("parallel",), 48), name="sample_mix")(
            x["q"], x["k"], x["g"], x["v"], x["pv"], state2, hist16, lw["gla_norm"], consts["gmat"], lw["pool_w"], lw["pool_scale"])


def _suffix_rows(x, period):
    n = x.shape[0]
    row = lax.broadcasted_iota(jnp.int32, x.shape, 0) % period
    k = 1
    while k < period:
        x = x + jnp.where(row + k < period, pltpu.roll(x, n - k, axis=0), 0.0)
        k *= 2
    return x


def _lfsum_kernel(lf_ref, sw_ref, o_ref):
    n = lf_ref.shape[0]
    hi, mid, lo = _split3(lf_ref[...])
    parts = jnp.concatenate([hi.astype(F32), mid.astype(F32), lo.astype(F32)], axis=0).astype(BF16)
    r3 = _dot(parts, sw_ref[...])
    o_ref[...] = r3[0:n] + r3[n:2 * n] + r3[2 * n:3 * n]


def _lfsum(lf2, sw, tr):
    n = lf2.shape[0]
    return pl.pallas_call(
        _lfsum_kernel, grid=(n // tr,),
        in_specs=[pl.BlockSpec((tr, PAGE), lambda i: (i, 0)), _const_spec((PAGE, 2 * PAGE))],
        out_specs=pl.BlockSpec((tr, 2 * PAGE), lambda i: (i, 0)), out_shape=jax.ShapeDtypeStruct((n, 2 * PAGE), F32),
        compiler_params=_cparams(("parallel",), 32), name="lfsum")(lf2, sw)


def _sample_fox_kernel(pt_ref, fq_ref, kn_ref, vn_ref, an_ref, rt_ref, ck_ref, cv_ref, o_ref,
                       kbuf, vbuf, ksem, vsem, *, n_pages, ts, n_phys, layer):
    b = pl.program_id(0)
    nb = pl.num_programs(0)
    slot = b % 2
    nxt = jnp.minimum(b + 1, nb - 1)

    def k_copy(seq, s, p):
        return pltpu.make_async_copy(ck_ref.at[layer, pt_ref[seq, p]], kbuf.at[s, p], ksem.at[s])

    def v_copy(seq, s, p):
        return pltpu.make_async_copy(cv_ref.at[layer, pt_ref[seq, p]], vbuf.at[s, p], vsem.at[s])

    @pl.when(b == 0)
    def _():
        for p in range(n_pages):
            k_copy(0, 0, p).start()
            v_copy(0, 0, p).start()

    for p in range(n_pages):
        k_copy(b, slot, p).wait()
    r2 = jnp.concatenate([rt_ref[pl.ds(h * n_phys + pt_ref[b, p], 1), :]
                          for h in range(N_HEADS) for p in range(n_pages)], axis=0)
    tot = r2[:, PAGE:]
    rfull = r2[:, :PAGE] + (_suffix_rows(tot, n_pages) - tot)
    cn = _cumsum_rows(an_ref[0])
    cn2 = jnp.concatenate([cn, cn], axis=0)
    cparts = _split3(cn)
    lane = lax.broadcasted_iota(jnp.int32, (2 * ts, LANES), 1)
    first = lax.broadcasted_iota(jnp.int32, (2 * ts, LANES), 0) < ts
    own = (lane < HEAD_DIM) == first
    causal = (lax.broadcasted_iota(jnp.int32, (2 * ts, ts), 0) % ts) >= lax.broadcasted_iota(jnp.int32, (2 * ts, ts), 1)
    for pr in range(N_HEADS // 2):
        cols = slice(pr * LANES, (pr + 1) * LANES)
        fq2 = jnp.concatenate([fq_ref[0, :, cols], fq_ref[0, :, cols]], axis=0)
        q2 = jnp.where(own, fq2, 0.0).astype(BF16)
        onehot = lane == jnp.where(first, 2 * pr, 2 * pr + 1)
        cq = jnp.sum(jnp.where(onehot, cn2, 0.0), axis=-1, keepdims=True)
        sel = jnp.where(onehot, 1.0, 0.0).astype(BF16)
        cn_t = _dot_nt(sel, cparts[0]) + _dot_nt(sel, cparts[1]) + _dot_nt(sel, cparts[2])
        s_new = jnp.where(causal, _dot_nt(q2, kn_ref[0, :, cols].astype(BF16)) + cq - cn_t, NEG)
        s_pages = []
        for p in range(n_pages):
            if pr == 0:
                k_copy(nxt, 1 - slot, p).start()
            kt = kbuf[slot, p, 2 * pr:2 * pr + 2].reshape(LANES, PAGE).astype(BF16)
            r0, r1 = 2 * pr * n_pages + p, (2 * pr + 1) * n_pages + p
            bias = jnp.concatenate([jnp.broadcast_to(rfull[r0:r0 + 1, :], (ts, PAGE)),
                                    jnp.broadcast_to(rfull[r1:r1 + 1, :], (ts, PAGE))], axis=0)
            s_pages.append(_dot(q2, kt) + bias + cq)
        mm = s_pages[0]
        for s in s_pages[1:]:
            mm = jnp.maximum(mm, s)
        m = jnp.maximum(jnp.max(mm, axis=-1, keepdims=True), jnp.max(s_new, axis=-1, keepdims=True))
        p_new = jnp.exp(s_new - m)
        l = jnp.sum(p_new, axis=-1, keepdims=True)
        acc = _dot(p_new.astype(BF16), vn_ref[0, :, cols].astype(BF16))
        if pr == 0:
            for p in range(n_pages):
                v_copy(b, slot, p).wait()
        for p in range(n_pages):
            if pr == 0:
                v_copy(nxt, 1 - slot, p).start()
            pe = jnp.exp(s_pages[p] - m)
            l = l + jnp.sum(pe, axis=-1, keepdims=True)
            vt = vbuf[slot, p, 2 * pr:2 * pr + 2].reshape(LANES, PAGE).astype(BF16)
            acc = acc + _dot_nt(pe.astype(BF16), vt)
        o2 = acc / l
        o_ref[0, :, cols] = jnp.where(lane[:ts] < HEAD_DIM, o2[:ts], o2[ts:])

    @pl.when(b == nb - 1)
    def _():
        for p in range(n_pages):
            k_copy(nxt, 1 - slot, p).wait()
            v_copy(nxt, 1 - slot, p).wait()


def _sample_fox(layer, page_table, fq, kn, vn, an, rt2, cache_kt, cache_vt):
    nb, ts, _ = fq.shape
    n_pages = page_table.shape[1]
    n_phys = cache_kt.shape[1]
    seq = lambda w: pl.BlockSpec((1, ts, w), lambda b, pt: (b, 0, 0))
    rt_spec = pl.BlockSpec((N_HEADS * n_phys, 2 * PAGE), lambda b, pt: (layer, 0), pipeline_mode=pl.Buffered(1))
    hbm = pl.BlockSpec(memory_space=pl.ANY)
    page_buf = pltpu.VMEM((2, n_pages, N_HEADS, HEAD_DIM, PAGE), F32)
    grid_spec = pltpu.PrefetchScalarGridSpec(
        num_scalar_prefetch=1, grid=(nb,),
        in_specs=[seq(HEADS_W), seq(HEADS_W), seq(HEADS_W), seq(LANES), rt_spec, hbm, hbm],
        out_specs=pl.BlockSpec((1, ts, HEADS_W), lambda b, pt: (b, 0, 0)),
        scratch_shapes=[page_buf, page_buf, pltpu.SemaphoreType.DMA((2,)), pltpu.SemaphoreType.DMA((2,))])
    return pl.pallas_call(
        functools.partial(_sample_fox_kernel, n_pages=n_pages, ts=ts, n_phys=n_phys, layer=layer), grid_spec=grid_spec,
        out_shape=jax.ShapeDtypeStruct((nb, ts, HEADS_W), F32),
        compiler_params=_cparams(("arbitrary",), 56), name="sample_fox")(
            page_table, fq, kn, vn, an, rt2, cache_kt, cache_vt)


def _wprep_kernel(w_ref, o_ref, *, depth):
    per_col = (D_MODEL // LANES) * depth
    for l in range(depth):
        for rt in range(D_MODEL // LANES):
            blk = w_ref[pl.ds(rt * depth + l, LANES, stride=per_col), :]
            o_ref[l, rt * LANES:(rt + 1) * LANES, :] = blk.T.astype(BF16)


def _wprep(w_in):
    depth, d, c = w_in.shape
    per_col = (d // LANES) * depth
    cp = -(-c // LANES) * LANES
    w2 = jnp.transpose(w_in, (2, 0, 1)).reshape(c, depth, d // LANES, LANES)
    w2 = jnp.transpose(w2, (0, 2, 1, 3)).reshape(c * per_col, LANES)
    w2 = jnp.pad(w2, ((0, (cp - c) * per_col), (0, 0)))
    return pl.pallas_call(
        functools.partial(_wprep_kernel, depth=depth), grid=(cp // LANES,),
        in_specs=[pl.BlockSpec((LANES * per_col, LANES), lambda j: (j, 0))],
        out_specs=pl.BlockSpec((depth, d, LANES), lambda j: (0, 0, j)),
        out_shape=jax.ShapeDtypeStruct((depth, d, cp), BF16),
        compiler_params=_cparams(("parallel",), 32), name="wprep")(w2)


def _layer_weights(l, norm_pre, norm_post, w_rm, gla_w_up, gla_b_gate, gla_norm, pool_w, pool_scale, fox_b_f, w_out):
    idx = [sum(SPLIT_SIZES[:i + 1]) for i in range(len(SPLIT_SIZES) - 1)]
    gq, gk, gv, glr, gg, pv, pg, fq, fk, fv, ff, fg = jnp.split(w_rm[l, :, :sum(SPLIT_SIZES)], idx, axis=1)
    zc = lambda n: jnp.zeros((D_MODEL, n), BF16)
    aux = jnp.concatenate([ff, zc(AUX_LR - N_HEADS), glr, zc(LANES - AUX_LR - GLA_LR)], axis=1)
    w = jnp.concatenate([gq, zc(COL_K - QK_W), gk, zc(COL_GV - COL_K - QK_W), gv, gg, pg, fg, pv, fq, fk, fv, aux], axis=1)
    w_up = jnp.zeros((LANES, QK_W), F32).at[AUX_LR:AUX_LR + GLA_LR].set(gla_w_up[l])
    bf = jnp.zeros((1, LANES), F32).at[0, AUX_FF:AUX_FF + N_HEADS].set(fox_b_f[l])
    pw = jnp.zeros((POOL_W, POOL_W), F32)
    for gi in range(len(POOL_WINDOWS)):
        sl = slice(gi * POOL_GROUP, (gi + 1) * POOL_GROUP)
        pw = pw.at[sl, sl].set(pool_w[l, gi])
    return {"pre": norm_pre[l][None], "post": norm_post[l][None], "w_in": w, "w_up": w_up.astype(BF16),
            "b_gate": gla_b_gate[l][None], "bf": bf, "gla_norm": jnp.tile(gla_norm[l], N_HEADS)[None],
            "pool_w": pw.astype(BF16), "pool_scale": pool_scale[l][None], "w_out": w_out[l].astype(BF16)}


def _constants():
    i = np.arange(HEADS_W)
    gmat = np.where((i[:, None] // HEAD_DIM) == (i[None, :] // HEAD_DIM), 1.0 / HEAD_DIM, 0.0)
    r = np.arange(PAGE)
    tri = r[:, None] >= r[None, :]
    a = np.arange(LANES)
    place = np.stack([(a[:, None] < N_HEADS) & (a[None, :] == a[:, None] + off) for off in (0, 6, 12, 18, 24, 30)])
    sw = np.concatenate([r[:, None] > r[None, :], np.ones((PAGE, PAGE), bool)], axis=1)
    bf = lambda x: jnp.asarray(x.astype(np.float32), dtype=BF16)
    return {"gmat": bf(gmat), "tri": bf(tri), "place": bf(place), "sw": bf(sw)}


def _state_from_bd(st):
    B = st.shape[0]
    s = st.reshape(B, N_HEADS, HEAD_DIM, N_HEADS, GLA_DK)
    s = jnp.stack([s[:, h, :, h, :] for h in range(N_HEADS)], axis=1)
    return jnp.swapaxes(s, 2, 3)


def _sequence_layer(h, lw, consts, prefix, tm, tq, chunk, layer=0, depth=1, kv_t=None):
    n_pre = 0 if prefix is None else prefix["k"].shape[0]
    x = _seqmix(h, lw, consts, prefix, tm, chunk, layer, depth, kv_t)
    h_new = _fox_out(x, h, x["og"], x["op"], lw, consts, tq, n_pre)
    return h_new, x


def _sample_layer(l, h, lw, consts, page_table, rt5, cache_kt, cache_vt, state2, hist16, nb, ts, tm, nseq):
    past = page_table.shape[1] * PAGE
    x = _inproj(h, lw, tm)
    flat = {n: x[n][0] for n in ("q", "k", "g", "pv")}
    flat["v"] = x["v"][0].astype(F32)
    og, op, s_new = _sample_mix(flat, state2, hist16, lw, consts, nseq, ts, past)
    per_seq = lambda a, w: a.reshape(nb, ts, w)
    of = _sample_fox(l, page_table, per_seq(x["fq"][0].astype(F32), HEADS_W), per_seq(x["fk"][0], HEADS_W),
                     per_seq(x["fv"][0], HEADS_W), per_seq(x["aux"][0], LANES), rt5, cache_kt, cache_vt)
    h_new = _outproj(h, og[None], op[None], of.reshape(1, nb * ts, HEADS_W), x["gates"], lw, tm)
    return h_new, x, s_new


def _largest_tile(n, cap):
    t = min(n, cap)
    while n % t:
        t -= 8
    return t


def kernel(x_prompt, x_sample, cache_fox_k, cache_fox_v, cache_fox_logf, state_gla, state_pool, page_table,
           meta_tokens, norm_pre, norm_post, w_in, gla_w_up, gla_b_gate, gla_norm, pool_w, pool_scale,
           fox_b_f, w_out):
    B, T, _ = x_prompt.shape
    nb, ts, _ = x_sample.shape
    depth, n_phys = cache_fox_k.shape[:2]
    consts = _constants()
    cache_kt = jnp.transpose(cache_fox_k, (0, 1, 3, 4, 2))
    cache_vt = jnp.transpose(cache_fox_v, (0, 1, 3, 4, 2))
    lf2 = jnp.transpose(cache_fox_logf, (0, 3, 1, 2)).reshape(depth * N_HEADS * n_phys, PAGE)
    rt5 = _lfsum(lf2, consts["sw"], _largest_tile(depth * N_HEADS * n_phys, 2048))
    state2 = state_gla.reshape(depth, nb, QK_W, HEAD_DIM)
    state2 = jnp.concatenate([state2, state2], axis=-1)
    hist16 = jnp.pad(state_pool, ((0, 0), (0, 0), (HIST_ROWS - POOL_HIST, 0), (0, 0)))
    tm_p, tm_s = _largest_tile(T, 256), _largest_tile(nb * ts, 512)
    w_rm = _wprep(w_in)

    h_m, h_p, h_s = meta_tokens[None], x_prompt, x_sample.reshape(1, nb * ts, D_MODEL)
    out = [[] for _ in range(10)]
    L = N_META + T
    kv_t = (jnp.zeros((depth, B, HEADS_W, L), F32), jnp.zeros((depth, B, HEADS_W, L), F32))
    for l in range(depth):
        lw = _layer_weights(l, norm_pre, norm_post, w_rm, gla_w_up, gla_b_gate, gla_norm, pool_w, pool_scale, fox_b_f, w_out)
        h_m_new, xm = _sequence_layer(h_m, lw, consts, None, N_META, N_META, N_META)
        prefix = {"k": xm["kt"][0, 0].T, "v": xm["vt"][0, 0].T, "aux": xm["aux"][0], "state": xm["state"][0], "hist": xm["hist"][0]}
        h_p, xp = _sequence_layer(h_p, lw, consts, prefix, tm_p, tm_p, GLA_CHUNK, l, depth, kv_t)
        kv_t = (xp["kt"], xp["vt"])
        h_m = h_m_new
        h_s, xs, s_new = _sample_layer(l, h_s, lw, consts, page_table, rt5, cache_kt, cache_vt,
                                       state2[l], hist16[l], nb, ts, tm_s, 8)
        out[2].append(xp["aux"][:, :, :N_HEADS])
        out[3].append(_state_from_bd(xp["state"]))
        out[4].append(xp["hist"][:, HIST_ROWS - POOL_HIST:])
        out[5].append(xs["fk"].reshape(nb, ts, N_HEADS, HEAD_DIM))
        out[6].append(xs["fv"].reshape(nb, ts, N_HEADS, HEAD_DIM))
        out[7].append(xs["aux"][0, :, :N_HEADS].reshape(nb, ts, N_HEADS))
        out[8].append(s_new[:, :, :HEAD_DIM].reshape(nb, N_HEADS, GLA_DK, HEAD_DIM))
        out[9].append(jnp.concatenate([state_pool[l], xs["pv"][0].reshape(nb, ts, POOL_W)], axis=1)[:, ts:])
    kv_out = tuple(jnp.transpose(a.reshape(depth, B, N_HEADS, HEAD_DIM, L), (0, 1, 4, 2, 3)) for a in kv_t)
    return (h_p, h_s.reshape(nb, ts, D_MODEL)) + kv_out + tuple(jnp.stack(o) for o in out[2:])
```

```python
import functools

import jax
import jax.numpy as jnp
import numpy as np
from jax import lax
from jax.experimental import pallas as pl
from jax.experimental.pallas import tpu as pltpu

F32 = jnp.float32
BF16 = jnp.bfloat16

D_MODEL = 1024
N_HEADS = 6
GLA_DK = 32
HEAD_DIM = 64
GLA_LR = 16
GLA_GATE_NORM = 16.0
GLA_CHUNK = 64
QK_W = N_HEADS * GLA_DK
HEADS_W = N_HEADS * HEAD_DIM
POOL_W = 256
POOL_GROUP = 64
POOL_WINDOWS = (2, 4, 8, 16)
POOL_HIST = 15
HIST_ROWS = 16
N_META = 16
PAGE = 128
EPS = 1e-6
LANES = 128
NEG = -1e30

COL_Q, COL_K, COL_GV, COL_FQ, COL_GATES = 0, 256, 512, 896, 1280
COL_PV, COL_FK, COL_FV, COL_AUX, W_IN_COLS = 2304, 2560, 2944, 3328, 3456
GATES_W = HEADS_W + POOL_W + HEADS_W
AUX_FF, AUX_LR = 0, 8
SPLIT_SIZES = (QK_W, QK_W, HEADS_W, GLA_LR, HEADS_W, POOL_W, POOL_W, HEADS_W, HEADS_W, HEADS_W, N_HEADS, HEADS_W)

AUXK_ONES = (18, 36)


def _cparams(sem, vmem_mb):
    return pltpu.CompilerParams(dimension_semantics=sem, vmem_limit_bytes=vmem_mb * 1024 * 1024)


def _const_spec(shape):
    return pl.BlockSpec(shape, lambda *_: (0,) * len(shape))


def _dot(a, b):
    return jnp.dot(a, b, preferred_element_type=F32)


def _dot_nt(a, b):
    return lax.dot_general(a, b, (((1,), (1,)), ((), ())), preferred_element_type=F32)


def _dot_tn(a, b):
    return lax.dot_general(a, b, (((0,), (0,)), ((), ())), preferred_element_type=F32)


def _log_sigmoid(x):
    return jnp.minimum(x, 0.0) - jnp.log1p(jnp.exp(-jnp.abs(x)))


def _silu(x):
    return x * (1.0 / (1.0 + jnp.exp(-x)))


def _split3(x):
    hi = x.astype(BF16)
    r = x - hi.astype(F32)
    mid = r.astype(BF16)
    lo = (r - mid.astype(F32)).astype(BF16)
    return hi, mid, lo


def _cumsum_rows(x):
    n = x.shape[0]
    row = lax.broadcasted_iota(jnp.int32, x.shape, 0)
    k = 1
    while k < n:
        x = x + jnp.where(row >= k, pltpu.roll(x, k, axis=0), 0.0)
        k *= 2
    return x


def _inproj_kernel(x_ref, pre_ref, w_ref, wup_ref, bg_ref, bf_ref,
                   q_ref, k_ref, g_ref, v_ref, gates_ref, pv_ref, fq_ref, fk_ref, fv_ref, aux_ref):
    x = x_ref[0]
    xn = x * lax.rsqrt(jnp.mean(x * x, axis=-1, keepdims=True) + EPS) * pre_ref[...]
    xb = xn.astype(BF16)

    def seg(c0, width):
        return _dot(xb, w_ref[:, c0:c0 + width])

    q_ref[0] = seg(COL_Q, QK_W) * (GLA_DK ** -0.5)
    k_ref[0] = seg(COL_K, QK_W)
    vq = seg(COL_GV, 2 * HEADS_W)
    v_ref[0] = vq[:, :HEADS_W].astype(BF16)
    fq_ref[0] = (vq[:, HEADS_W:] * (HEAD_DIM ** -0.5)).astype(BF16)
    gates_ref[0] = seg(COL_GATES, GATES_W)
    pv_ref[0] = seg(COL_PV, POOL_W)
    aux = seg(COL_AUX, LANES)
    g_ref[0] = _log_sigmoid(_dot(aux.astype(BF16), wup_ref[...]) + bg_ref[...]) * (1.0 / GLA_GATE_NORM)
    kv = seg(COL_FK, 2 * HEADS_W)
    fk_ref[0] = kv[:, :HEADS_W]
    fv_ref[0] = kv[:, HEADS_W:]
    aux_ref[0] = _log_sigmoid(aux + bf_ref[...])


def _inproj(h, lw, tm):
    B, T, _ = h.shape
    tile = lambda w: pl.BlockSpec((1, tm, w), lambda b, t: (b, t, 0))
    in_specs = [tile(D_MODEL), _const_spec((1, D_MODEL)), _const_spec((D_MODEL, W_IN_COLS)),
                _const_spec((LANES, QK_W)), _const_spec((1, QK_W)), _const_spec((1, LANES))]
    widths = (("q", QK_W, F32), ("k", QK_W, F32), ("g", QK_W, F32), ("v", HEADS_W, BF16), ("gates", GATES_W, F32),
              ("pv", POOL_W, F32), ("fq", HEADS_W, BF16), ("fk", HEADS_W, F32), ("fv", HEADS_W, F32), ("aux", LANES, F32))
    outs = pl.pallas_call(
        _inproj_kernel, grid=(B, T // tm), in_specs=in_specs,
        out_specs=[tile(w) for _, w, _ in widths],
        out_shape=[jax.ShapeDtypeStruct((B, T, w), dt) for _, w, dt in widths],
        compiler_params=_cparams(("parallel", "parallel"), 56), name="inproj")(
            h, lw["pre"], lw["w_in"], lw["w_up"], lw["b_gate"], lw["bf"])
    return dict(zip([n for n, _, _ in widths], outs))


def _gla_masks(chunk):
    n = N_HEADS * chunk
    bd_k = (lax.broadcasted_iota(jnp.int32, (n, QK_W), 0) // chunk) == (lax.broadcasted_iota(jnp.int32, (n, QK_W), 1) // GLA_DK)
    bd_v = (lax.broadcasted_iota(jnp.int32, (n, HEADS_W), 0) // chunk) == (lax.broadcasted_iota(jnp.int32, (n, HEADS_W), 1) // HEAD_DIM)
    tril = (lax.broadcasted_iota(jnp.int32, (chunk, n), 1) % chunk) <= lax.broadcasted_iota(jnp.int32, (chunk, n), 0)
    return bd_k, bd_v, tril


def _gla_intra(q, k, g, v_bf, masks, chunk):
    bd_k, bd_v, tril = masks
    bcum = _cumsum_rows(g)
    qe_bf = (q * jnp.exp(bcum)).astype(BF16)
    ke = k * jnp.exp(-bcum)
    b_end = bcum[chunk - 1:chunk, :]
    kd_bf = (k * jnp.exp(b_end - bcum)).astype(BF16)
    if chunk % 16 == 0:
        ke_rows = jnp.where(bd_k, jnp.concatenate([ke.astype(BF16)] * N_HEADS, axis=0), jnp.zeros((), BF16))
        v_rows = jnp.where(bd_v, jnp.concatenate([v_bf] * N_HEADS, axis=0), jnp.zeros((), BF16))
    else:
        ke_rows = jnp.where(bd_k, jnp.concatenate([ke] * N_HEADS, axis=0), 0.0).astype(BF16)
        v_rows = jnp.where(bd_v, jnp.concatenate([v_bf.astype(F32)] * N_HEADS, axis=0), 0.0).astype(BF16)
    a = jnp.where(tril, _dot_nt(qe_bf, ke_rows), 0.0)
    o_intra = _dot(a.astype(BF16), v_rows)
    return o_intra, qe_bf, kd_bf, jnp.exp(b_end)


def _head_norm(o, gmat_bf, gn):
    o2 = o * o
    hi = o2.astype(BF16)
    lo = (o2 - hi.astype(F32)).astype(BF16)
    ms = _dot(hi, gmat_bf) + _dot(lo, gmat_bf)
    return o * lax.rsqrt(ms + EPS) * gn


def _pool_delta(hist, pv, pos0, tm):
    assert POOL_WINDOWS == (2, 4, 8, 16)
    ext = jnp.concatenate([hist, pv], axis=0)
    lane_g = lax.broadcasted_iota(jnp.int32, (1, POOL_W), 1) // POOL_GROUP
    pos = pos0 + lax.broadcasted_iota(jnp.int32, (tm, 1), 0)
    run = ext
    sums = jnp.zeros((tm, POOL_W), F32)
    cnt = jnp.zeros((tm, POOL_W), F32)
    for gi, w in enumerate(POOL_WINDOWS):
        run = run + pltpu.roll(run, w // 2, axis=0)
        sums = jnp.where(lane_g == gi, run[HIST_ROWS:, :], sums)
        cnt = jnp.where(lane_g == gi, jnp.minimum(pos + 1, w).astype(F32), cnt)
    return sums / cnt - pv, ext[tm:tm + HIST_ROWS, :]


def _seqmix_kernel(*refs, T, tm, chunk, n_pre, aliased):
    if n_pre:
        (x_ref, pre_ref, w_ref, wup_ref, bg_ref, bf_ref, kpre_ref, vpre_ref, apre_ref, s0_ref, hist0_ref, gn_ref, gmat_ref,
         pw_ref, ps_ref) = refs[:15]
        refs = refs[15:]
    else:
        (x_ref, pre_ref, w_ref, wup_ref, bg_ref, bf_ref, s0_ref, hist0_ref, gn_ref, gmat_ref, pw_ref, ps_ref) = refs[:12]
        refs = refs[12:]
    if aliased:
        refs = refs[2:]
    (gates_ref, fq_ref, fk_ref, fv_ref, kt_ref, vt_ref, aux_ref, og_ref, op_ref, sout_ref, hout_ref,
     st_ref, ext_ref, kext_ref, vext_ref) = refs
    t = pl.program_id(1)

    @pl.when(t == 0)
    def _():
        st_ref[...] = s0_ref[...]
        ext_ref[...] = hist0_ref[...]
        if n_pre:
            fk_ref[0, 0:n_pre, :] = kpre_ref[...].astype(BF16)
            fv_ref[0, 0:n_pre, :] = vpre_ref[...].astype(BF16)
            aux_ref[0, 0:n_pre, :] = apre_ref[...]
            kext_ref[0:n_pre, :] = kpre_ref[...]
            vext_ref[0:n_pre, :] = vpre_ref[...]

    def transposed(x):
        r = x.shape[0]
        if r % LANES == 0:
            return x.T
        return jnp.concatenate([x, jnp.zeros((LANES - r, HEADS_W), F32)], axis=0).T[:, :r]

    def emit_transposed(out_ref, carry_ref, tile_rows):
        if n_pre:
            carry_ref[n_pre:n_pre + tm, :] = tile_rows
            tile_rows = carry_ref[0:tm, :]
        lanes = slice(0, tm) if T == tm else pl.ds(pl.multiple_of(t * tm, LANES), tm)
        out_ref[0, 0, :, lanes] = transposed(tile_rows)
        if n_pre:
            tail = carry_ref[tm:tm + n_pre, :]
            carry_ref[0:n_pre, :] = tail

            @pl.when(t == pl.num_programs(1) - 1)
            def _():
                out_ref[0, 0, :, T:T + n_pre] = transposed(tail)

    x = x_ref[0]
    xn = x * lax.rsqrt(jnp.mean(x * x, axis=-1, keepdims=True) + EPS) * pre_ref[...]
    xb = xn.astype(BF16)

    def seg(c0, width):
        return _dot(xb, w_ref[:, c0:c0 + width])

    gates_ref[0] = seg(COL_GATES, GATES_W)
    vq = seg(COL_GV, 2 * HEADS_W)
    fq_ref[0] = (vq[:, HEADS_W:] * (HEAD_DIM ** -0.5)).astype(BF16)
    aux = seg(COL_AUX, LANES)
    rows = pl.ds(pl.multiple_of(n_pre + t * tm, 8), tm)
    kv = seg(COL_FK, 2 * HEADS_W)
    fk, fv = kv[:, :HEADS_W], kv[:, HEADS_W:]
    fk_ref[0, rows, :] = fk.astype(BF16)
    fv_ref[0, rows, :] = fv.astype(BF16)
    emit_transposed(kt_ref, kext_ref, fk)
    emit_transposed(vt_ref, vext_ref, fv)
    aux_ref[0, rows, :] = _log_sigmoid(aux + bf_ref[...])
    q = seg(COL_Q, QK_W) * (GLA_DK ** -0.5)
    k = seg(COL_K, QK_W)
    v_bf = vq[:, :HEADS_W].astype(BF16)
    g = _log_sigmoid(_dot(aux.astype(BF16), wup_ref[...]) + bg_ref[...]) * (1.0 / GLA_GATE_NORM)

    masks = _gla_masks(chunk)
    bd = (lax.broadcasted_iota(jnp.int32, (HEADS_W, QK_W), 0) // HEAD_DIM) == (lax.broadcasted_iota(jnp.int32, (HEADS_W, QK_W), 1) // GLA_DK)
    st = st_ref[...]
    outs = []
    for c in range(tm // chunk):
        r = slice(c * chunk, (c + 1) * chunk)
        o_intra, qe_bf, kd_bf, decay = _gla_intra(q[r], k[r], g[r], v_bf[r], masks, chunk)
        outs.append(o_intra + _dot_nt(qe_bf, st.astype(BF16)))
        st = st * decay + jnp.where(bd, _dot_tn(v_bf[r], kd_bf), 0.0)
    st_ref[...] = st
    o = outs[0] if len(outs) == 1 else jnp.concatenate(outs, axis=0)
    og_ref[0] = _head_norm(o, gmat_ref[...], gn_ref[...])
    d, ext_ref[...] = _pool_delta(ext_ref[...], seg(COL_PV, POOL_W), n_pre + t * tm, tm)
    op_ref[0] = _dot(d.astype(BF16), pw_ref[...]) * ps_ref[...]

    @pl.when(t == pl.num_programs(1) - 1)
    def _():
        sout_ref[0] = st
        hout_ref[0] = ext_ref[...]


def _seqmix(h, lw, consts, prefix, tm, chunk, layer=0, depth=1, kv_t=None):
    B, T, _ = h.shape
    n_pre = 0 if prefix is None else prefix["k"].shape[0]
    L = n_pre + T
    tile = lambda w: pl.BlockSpec((1, tm, w), lambda b, t: (b, t, 0))
    whole = lambda w: pl.BlockSpec((1, L, w), lambda b, t: (b, 0, 0))
    per_seq = lambda r, w: pl.BlockSpec((1, r, w), lambda b, t: (b, 0, 0))
    in_specs = [tile(D_MODEL), _const_spec((1, D_MODEL)), _const_spec((D_MODEL, W_IN_COLS)),
                _const_spec((LANES, QK_W)), _const_spec((1, QK_W)), _const_spec((1, LANES))]
    args = [h, lw["pre"], lw["w_in"], lw["w_up"], lw["b_gate"], lw["bf"]]
    if n_pre:
        in_specs += [_const_spec((n_pre, HEADS_W)), _const_spec((n_pre, HEADS_W)), _const_spec((n_pre, LANES))]
        args += [prefix["k"], prefix["v"], prefix["aux"]]
    in_specs += [_const_spec((HEADS_W, QK_W)), _const_spec((HIST_ROWS, POOL_W)), _const_spec((1, HEADS_W)),
                 _const_spec((HEADS_W, HEADS_W)), _const_spec((POOL_W, POOL_W)), _const_spec((1, POOL_W))]
    s0 = jnp.zeros((HEADS_W, QK_W), F32) if prefix is None else prefix["state"]
    hist0 = jnp.zeros((HIST_ROWS, POOL_W), F32) if prefix is None else prefix["hist"]
    args += [s0, hist0, lw["gla_norm"], consts["gmat"], lw["pool_w"], lw["pool_scale"]]
    aliases = {}
    if kv_t is not None:
        aliases = {len(args): 4, len(args) + 1: 5}
        in_specs += [pl.BlockSpec(memory_space=pl.ANY)] * 2
        args += list(kv_t)
    slab = pl.BlockSpec((1, 1, HEADS_W, L), lambda b, t: (layer, b, 0, 0))
    out_shape = [jax.ShapeDtypeStruct((B, T, GATES_W), F32), jax.ShapeDtypeStruct((B, T, HEADS_W), BF16),
                 jax.ShapeDtypeStruct((B, L, HEADS_W), BF16), jax.ShapeDtypeStruct((B, L, HEADS_W), BF16),
                 jax.ShapeDtypeStruct((depth, B, HEADS_W, L), F32), jax.ShapeDtypeStruct((depth, B, HEADS_W, L), F32),
                 jax.ShapeDtypeStruct((B, L, LANES), F32), jax.ShapeDtypeStruct((B, T, HEADS_W), F32),
                 jax.ShapeDtypeStruct((B, T, POOL_W), F32), jax.ShapeDtypeStruct((B, HEADS_W, QK_W), F32),
                 jax.ShapeDtypeStruct((B, HIST_ROWS, POOL_W), F32)]
    out_specs = [tile(GATES_W), tile(HEADS_W), whole(HEADS_W), whole(HEADS_W), slab, slab, whole(LANES), tile(HEADS_W),
                 tile(POOL_W), per_seq(HEADS_W, QK_W), per_seq(HIST_ROWS, POOL_W)]
    names = ("gates", "fq", "fk", "fv", "kt", "vt", "aux", "og", "op", "state", "hist")
    outs = pl.pallas_call(
        functools.partial(_seqmix_kernel, T=T, tm=tm, chunk=chunk, n_pre=n_pre, aliased=kv_t is not None),
        grid=(B, T // tm), in_specs=in_specs, out_specs=out_specs, out_shape=out_shape,
        input_output_aliases=aliases,
        scratch_shapes=[pltpu.VMEM((HEADS_W, QK_W), F32), pltpu.VMEM((HIST_ROWS, POOL_W), F32),
                        pltpu.VMEM((tm + n_pre, HEADS_W), F32), pltpu.VMEM((tm + n_pre, HEADS_W), F32)],
        compiler_params=_cparams(("parallel", "arbitrary"), 56), name="seqmix")(*args)
    return dict(zip(names, outs))


def _fox_features(lf, carry, tri_bf, place_ref):
    r = lf.shape[0]
    lane = lax.broadcasted_iota(jnp.int32, (1, LANES), 1)
    tri = tri_bf[0:r, 0:r]
    hi, mid, lo = _split3(lf)
    f = _dot(tri, hi) + _dot(tri, mid) + _dot(tri, lo) + carry
    fh, fm, fl = _split3(f)
    ones_k = jnp.where((lane >= AUXK_ONES[0]) & (lane < AUXK_ONES[1]), 1.0, 0.0)
    ones_q = jnp.where(lane < AUXK_ONES[0], 1.0, 0.0)
    kaux = ones_k - (_dot(fh, place_ref[0]) + _dot(fm, place_ref[1]) + _dot(fl, place_ref[2]))
    qaux = ones_q + (_dot(fh, place_ref[3]) + _dot(fm, place_ref[4]) + _dot(fl, place_ref[5]))
    return kaux.astype(BF16), qaux.astype(BF16), f[r - 1:r, :]


def _fox_kernel(fq_ref, k_ref, v_ref, aux_ref, tri_ref, place_ref, h_ref, og_ref, op_ref, gates_ref, wo_ref, post_ref,
                out_ref, kaux_ref, qaux_ref, q2_ref, m_ref, acc_ref, sa_ref, sb_ref, *, T, n_pre, tq, tk):
    t = pl.program_id(1)
    r = tq // tk
    assert r * tk == tq and (r % 2 == 0 or T == tq)

    @pl.when(t == 0)
    def _():
        cb = min(PAGE, T)
        blocks = ([(0, n_pre)] if n_pre else []) + [(n_pre + i * cb, cb) for i in range(T // cb)]
        carry = jnp.zeros((1, LANES), F32)
        for r0, r in blocks:
            ka, qa, carry = _fox_features(aux_ref[0, r0:r0 + r, :], carry, tri_ref[...], place_ref)
            kaux_ref[r0:r0 + r, :] = ka
            qaux_ref[r0:r0 + r, :] = qa

    lane = lax.broadcasted_iota(jnp.int32, (1, LANES), 1)
    q0 = pl.multiple_of(n_pre + t * tq, 16)
    qa = qaux_ref[pl.ds(q0, tq), :]
    fq = fq_ref[0]
    zero_bf = jnp.zeros((tq, LANES), BF16)
    for h in range(N_HEADS):
        fq_p = fq[:, (h // 2) * LANES:(h // 2 + 1) * LANES]
        qmask = (lane == h) | (lane == 6 + h) | (lane == 12 + h) | (lane == 18 + h) | (lane == 24 + h) | (lane == 30 + h)
        q2_ref[h] = jnp.concatenate([jnp.where((lane // HEAD_DIM) == (h % 2), fq_p, zero_bf), jnp.where(qmask, qa, zero_bf)], axis=1)
    m_ref[...] = jnp.full(m_ref.shape, NEG, F32)
    acc_ref[...] = jnp.zeros(acc_ref.shape, F32)

    def logits(j, n, s_out):
        r0 = 0 if j is None else pl.multiple_of(n_pre + j * tk, 16)
        kaux = kaux_ref[pl.ds(r0, n), :]
        for p in range(N_HEADS // 2):
            k2 = jnp.concatenate([k_ref[0, pl.ds(r0, n), p * LANES:(p + 1) * LANES], kaux], axis=1)
            for h in (2 * p, 2 * p + 1):
                s_out[h] = _dot_nt(q2_ref[h], k2)

    def update(j, n, s_in, diag):
        r0 = 0 if j is None else pl.multiple_of(n_pre + j * tk, 16)
        ones = jnp.ones((n, LANES), BF16)
        causal = diag is not None
        if causal:
            keep = lax.broadcasted_iota(jnp.int32, (tq, n), 0) >= lax.broadcasted_iota(jnp.int32, (tq, n), 1) + diag * tk
        for p in range(N_HEADS // 2):
            vp = v_ref[0, pl.ds(r0, n), p * LANES:(p + 1) * LANES]
            for h in (2 * p, 2 * p + 1):
                v2 = jnp.where((lane // HEAD_DIM) == (h % 2), vp, ones)
                def read(z):
                    s = s_in[h] if j is None else s_in[h + jnp.minimum(z, 0)]
                    return jnp.where(keep, s, NEG) if causal else s
                m_old = m_ref[h]
                m_new = jnp.maximum(m_old, jnp.max(read(t), axis=-1, keepdims=True))
                alpha = jnp.exp(m_old - m_new)
                m_b = m_new[:, :n] if n < LANES else jnp.concatenate([m_new] * (n // LANES), axis=1)
                pe = jnp.exp(read(pl.program_id(0)) - m_b).astype(BF16)
                acc_ref[h] = alpha * acc_ref[h] + _dot(pe, v2)
                m_ref[h] = m_new

    if n_pre:
        pre = {}
        logits(None, n_pre, pre)
        update(None, n_pre, pre, None)

    logits(0, tk, sa_ref)

    def body(jj, c):
        j = 2 * jj
        logits(j + 1, tk, sb_ref)
        update(j, tk, sa_ref, None)
        logits(j + 2, tk, sa_ref)
        update(j + 1, tk, sb_ref, None)
        return c

    lax.fori_loop(0, (r * t) // 2, body, 0)
    bufs = (sa_ref, sb_ref)
    for i in range(r):
        if i + 1 < r:
            logits(r * t + i + 1, tk, bufs[(i + 1) % 2])
        update(r * t + i, tk, bufs[i % 2], i)

    o_fox = []
    for p in range(N_HEADS // 2):
        a0, a1 = acc_ref[2 * p], acc_ref[2 * p + 1]
        o_fox.append(jnp.where((lane // HEAD_DIM) == 0, a0 / pltpu.roll(a0, HEAD_DIM, axis=1),
                               a1 / pltpu.roll(a1, HEAD_DIM, axis=1)))
    y_in = jnp.concatenate([og_ref[0], op_ref[0]] + o_fox, axis=1) * _silu(gates_ref[0])
    y = _dot(y_in.astype(BF16), wo_ref[...])
    out_ref[0] = h_ref[0] + y * lax.rsqrt(jnp.mean(y * y, axis=-1, keepdims=True) + EPS) * post_ref[...]


def _fox_out(x, h, og, op, lw, consts, tq, n_pre):
    B, T, _ = x["fq"].shape
    L = n_pre + T
    whole = lambda w: pl.BlockSpec((1, L, w), lambda b, t: (b, 0, 0))
    tile = lambda w: pl.BlockSpec((1, tq, w), lambda b, t: (b, t, 0))
    in_specs = [tile(HEADS_W), whole(HEADS_W), whole(HEADS_W), whole(LANES),
                _const_spec((PAGE, PAGE)), _const_spec((6, LANES, LANES)),
                tile(D_MODEL), tile(HEADS_W), tile(POOL_W), tile(GATES_W), _const_spec((GATES_W, D_MODEL)), _const_spec((1, D_MODEL))]
    tk = min(tq, 256)
    return pl.pallas_call(
        functools.partial(_fox_kernel, T=T, n_pre=n_pre, tq=tq, tk=tk),
        grid=(B, T // tq), in_specs=in_specs,
        out_specs=tile(D_MODEL),
        out_shape=jax.ShapeDtypeStruct((B, T, D_MODEL), F32),
        scratch_shapes=[pltpu.VMEM((L, LANES), BF16), pltpu.VMEM((L, LANES), BF16),
                        pltpu.VMEM((N_HEADS, tq, 2 * LANES), BF16), pltpu.VMEM((N_HEADS, tq, LANES), F32),
                        pltpu.VMEM((N_HEADS, tq, LANES), F32),
                        pltpu.VMEM((N_HEADS, tq, tk), F32), pltpu.VMEM((N_HEADS, tq, tk), F32)],
        compiler_params=_cparams(("parallel", "arbitrary"), 56), name="fox")(
            x["fq"], x["fk"], x["fv"], x["aux"], consts["tri"], consts["place"],
            h, og, op, x["gates"], lw["w_out"], lw["post"])


def _outproj_kernel(h_ref, og_ref, op_ref, of_ref, gates_ref, wo_ref, post_ref, out_ref):
    y_in = jnp.concatenate([og_ref[0], op_ref[0], of_ref[0]], axis=1) * _silu(gates_ref[0])
    y = _dot(y_in.astype(BF16), wo_ref[...])
    out_ref[0] = h_ref[0] + y * lax.rsqrt(jnp.mean(y * y, axis=-1, keepdims=True) + EPS) * post_ref[...]


def _outproj(h, og, op, of, gates, lw, tm):
    B, T, _ = h.shape
    tile = lambda w: pl.BlockSpec((1, tm, w), lambda b, t: (b, t, 0))
    return pl.pallas_call(
        _outproj_kernel, grid=(B, T // tm),
        in_specs=[tile(D_MODEL), tile(HEADS_W), tile(POOL_W), tile(HEADS_W), tile(GATES_W),
                  _const_spec((GATES_W, D_MODEL)), _const_spec((1, D_MODEL))],
        out_specs=tile(D_MODEL), out_shape=jax.ShapeDtypeStruct((B, T, D_MODEL), F32),
        compiler_params=_cparams(("parallel", "parallel"), 48), name="outproj")(
            h, og, op, of, gates, lw["w_out"], lw["post"])


def _sample_mix_kernel(q_ref, k_ref, g_ref, v_ref, pv_ref, s_ref, hist_ref, gn_ref, gmat_ref, pw_ref, ps_ref,
                       og_ref, op_ref, sout_ref, *, nseq, ts, pos0):
    masks = _gla_masks(ts)
    rr = lax.broadcasted_iota(jnp.int32, (QK_W, HEADS_W), 0) // GLA_DK
    cc = lax.broadcasted_iota(jnp.int32, (QK_W, HEADS_W), 1) // HEAD_DIM
    bd = rr == cc
    ones = jnp.ones((3 * ts, LANES), BF16)
    outs, deltas = [], []
    for i in range(nseq):
        r = slice(i * ts, (i + 1) * ts)
        g = g_ref[r, :]
        v_bf = v_ref[r, :].astype(BF16)
        o_intra, qe_bf, kd_bf, _ = _gla_intra(q_ref[r, :], k_ref[r, :], g, v_bf, masks, ts)
        s1 = s_ref[i]
        s2 = jnp.concatenate([s1, s1], axis=1)
        s_bd = jnp.where(bd, jnp.concatenate([s2, s2, s2], axis=1), 0.0).astype(BF16)
        outs.append(o_intra + _dot(qe_bf, s_bd))
        u = jnp.where(bd, _dot_tn(kd_bf, v_bf), 0.0)
        x = jnp.concatenate([u[64 * j:64 * (j + 1), LANES * j:LANES * (j + 1)] for j in range(N_HEADS // 2)], axis=0)
        x = x + pltpu.roll(x, HEAD_DIM, axis=1)
        gh, gm, gl = _split3(g)
        gparts = jnp.concatenate([gh.astype(F32), gm.astype(F32), gl.astype(F32)], axis=0).astype(BF16)
        sout_ref[i] = (s2 * jnp.exp(_dot_tn(gparts, ones)) + x)[:, :HEAD_DIM]
        deltas.append(_pool_delta(hist_ref[i], pv_ref[r, :], pos0, ts)[0])
    og_ref[...] = _head_norm(jnp.concatenate(outs, axis=0), gmat_ref[...], gn_ref[...])
    op_ref[...] = _dot(jnp.concatenate(deltas, axis=0).astype(BF16), pw_ref[...]) * ps_ref[...]


def _sample_mix(x, state2, hist16, lw, consts, nseq, ts, pos0):
    n = x["q"].shape[0]
    rows = nseq * ts
    tile = lambda w: pl.BlockSpec((rows, w), lambda i: (i, 0))
    in_specs = [tile(QK_W), tile(QK_W), tile(QK_W), tile(HEADS_W), tile(POOL_W),
                pl.BlockSpec((nseq, QK_W, HEAD_DIM), lambda i: (i, 0, 0)), pl.BlockSpec((nseq, HIST_ROWS, POOL_W), lambda i: (i, 0, 0)),
                _const_spec((1, HEADS_W)), _const_spec((HEADS_W, HEADS_W)), _const_spec((POOL_W, POOL_W)), _const_spec((1, POOL_W))]
    out_shape = [jax.ShapeDtypeStruct((n, HEADS_W), F32), jax.ShapeDtypeStruct((n, POOL_W), F32),
                 jax.ShapeDtypeStruct(state2.shape, F32)]
    out_specs = [tile(HEADS_W), tile(POOL_W), pl.BlockSpec((nseq, QK_W, HEAD_DIM), lambda i: (i, 0, 0))]
    return pl.pallas_call(
        functools.partial(_sample_mix_kernel, nseq=nseq, ts=ts, pos0=pos0),
        grid=(n // rows,), in_specs=in_specs, out_specs=out_specs, out_shape=out_shape,
        compiler_params=_cparams(("parallel",), 48), name="sample_mix")(
            x["q"], x["k"], x["g"], x["v"], x["pv"], state2, hist16, lw["gla_norm"], consts["gmat"], lw["pool_w"], lw["pool_scale"])


def _suffix_rows(x, period):
    n = x.shape[0]
    row = lax.broadcasted_iota(jnp.int32, x.shape, 0) % period
    k = 1
    while k < period:
        x = x + jnp.where(row + k < period, pltpu.roll(x, n - k, axis=0), 0.0)
        k *= 2
    return x


def _lfsum_kernel(lf_ref, sw_ref, o_ref):
    n = lf_ref.shape[0]
    hi, mid, lo = _split3(lf_ref[...])
    parts = jnp.concatenate([hi.astype(F32), mid.astype(F32), lo.astype(F32)], axis=0).astype(BF16)
    r3 = _dot(parts, sw_ref[...])
    o_ref[...] = r3[0:n] + r3[n:2 * n] + r3[2 * n:3 * n]


def _lfsum(lf2, sw, tr):
    n = lf2.shape[0]
    return pl.pallas_call(
        _lfsum_kernel, grid=(n // tr,),
        in_specs=[pl.BlockSpec((tr, PAGE), lambda i: (i, 0)), _const_spec((PAGE, 2 * PAGE))],
        out_specs=pl.BlockSpec((tr, 2 * PAGE), lambda i: (i, 0)), out_shape=jax.ShapeDtypeStruct((n, 2 * PAGE), F32),
        compiler_params=_cparams(("parallel",), 32), name="lfsum")(lf2, sw)


def _sample_fox_kernel(pt_ref, fq_ref, kn_ref, vn_ref, an_ref, rt_ref, ck_ref, cv_ref, o_ref,
                       kbuf, vbuf, ksem, vsem, *, n_pages, ts, n_phys, layer):
    b = pl.program_id(0)
    nb = pl.num_programs(0)
    slot = b % 2
    nxt = jnp.minimum(b + 1, nb - 1)

    def k_copy(seq, s, p):
        return pltpu.make_async_copy(ck_ref.at[layer, pt_ref[seq, p]], kbuf.at[s, p], ksem.at[s])

    def v_copy(seq, s, p):
        return pltpu.make_async_copy(cv_ref.at[layer, pt_ref[seq, p]], vbuf.at[s, p], vsem.at[s])

    @pl.when(b == 0)
    def _():
        for p in range(n_pages):
            k_copy(0, 0, p).start()
            v_copy(0, 0, p).start()

    for p in range(n_pages):
        k_copy(b, slot, p).wait()
    r2 = jnp.concatenate([rt_ref[pl.ds(h * n_phys + pt_ref[b, p], 1), :]
                          for h in range(N_HEADS) for p in range(n_pages)], axis=0)
    tot = r2[:, PAGE:]
    rfull = r2[:, :PAGE] + (_suffix_rows(tot, n_pages) - tot)
    cn = _cumsum_rows(an_ref[0])
    cn2 = jnp.concatenate([cn, cn], axis=0)
    cparts = _split3(cn)
    lane = lax.broadcasted_iota(jnp.int32, (2 * ts, LANES), 1)
    first = lax.broadcasted_iota(jnp.int32, (2 * ts, LANES), 0) < ts
    own = (lane < HEAD_DIM) == first
    causal = (lax.broadcasted_iota(jnp.int32, (2 * ts, ts), 0) % ts) >= lax.broadcasted_iota(jnp.int32, (2 * ts, ts), 1)
    for pr in range(N_HEADS // 2):
        cols = slice(pr * LANES, (pr + 1) * LANES)
        fq2 = jnp.concatenate([fq_ref[0, :, cols], fq_ref[0, :, cols]], axis=0)
        q2 = jnp.where(own, fq2, 0.0).astype(BF16)
        onehot = lane == jnp.where(first, 2 * pr, 2 * pr + 1)
        cq = jnp.sum(jnp.where(onehot, cn2, 0.0), axis=-1, keepdims=True)
        sel = jnp.where(onehot, 1.0, 0.0).astype(BF16)
        cn_t = _dot_nt(sel, cparts[0]) + _dot_nt(sel, cparts[1]) + _dot_nt(sel, cparts[2])
        s_new = jnp.where(causal, _dot_nt(q2, kn_ref[0, :, cols].astype(BF16)) + cq - cn_t, NEG)
        s_pages = []
        for p in range(n_pages):
            if pr == 0:
                k_copy(nxt, 1 - slot, p).start()
            kt = kbuf[slot, p, 2 * pr:2 * pr + 2].reshape(LANES, PAGE).astype(BF16)
            r0, r1 = 2 * pr * n_pages + p, (2 * pr + 1) * n_pages + p
            bias = jnp.concatenate([jnp.broadcast_to(rfull[r0:r0 + 1, :], (ts, PAGE)),
                                    jnp.broadcast_to(rfull[r1:r1 + 1, :], (ts, PAGE))], axis=0)
            s_pages.append(_dot(q2, kt) + bias + cq)
        mm = s_pages[0]
        for s in s_pages[1:]:
            mm = jnp.maximum(mm, s)
        m = jnp.maximum(jnp.max(mm, axis=-1, keepdims=True), jnp.max(s_new, axis=-1, keepdims=True))
        p_new = jnp.exp(s_new - m)
        l = jnp.sum(p_new, axis=-1, keepdims=True)
        acc = _dot(p_new.astype(BF16), vn_ref[0, :, cols].astype(BF16))
        if pr == 0:
            for p in range(n_pages):
                v_copy(b, slot, p).wait()
        for p in range(n_pages):
            if pr == 0:
                v_copy(nxt, 1 - slot, p).start()
            pe = jnp.exp(s_pages[p] - m)
            l = l + jnp.sum(pe, axis=-1, keepdims=True)
            vt = vbuf[slot, p, 2 * pr:2 * pr + 2].reshape(LANES, PAGE).astype(BF16)
            acc = acc + _dot_nt(pe.astype(BF16), vt)
        o2 = acc / l
        o_ref[0, :, cols] = jnp.where(lane[:ts] < HEAD_DIM, o2[:ts], o2[ts:])

    @pl.when(b == nb - 1)
    def _():
        for p in range(n_pages):
            k_copy(nxt, 1 - slot, p).wait()
            v_copy(nxt, 1 - slot, p).wait()


def _sample_fox(layer, page_table, fq, kn, vn, an, rt2, cache_kt, cache_vt):
    nb, ts, _ = fq.shape
    n_pages = page_table.shape[1]
    n_phys = cache_kt.shape[1]
    seq = lambda w: pl.BlockSpec((1, ts, w), lambda b, pt: (b, 0, 0))
    rt_spec = pl.BlockSpec((N_HEADS * n_phys, 2 * PAGE), lambda b, pt: (layer, 0), pipeline_mode=pl.Buffered(1))
    hbm = pl.BlockSpec(memory_space=pl.ANY)
    page_buf = pltpu.VMEM((2, n_pages, N_HEADS, HEAD_DIM, PAGE), F32)
    grid_spec = pltpu.PrefetchScalarGridSpec(
        num_scalar_prefetch=1, grid=(nb,),
        in_specs=[seq(HEADS_W), seq(HEADS_W), seq(HEADS_W), seq(LANES), rt_spec, hbm, hbm],
        out_specs=pl.BlockSpec((1, ts, HEADS_W), lambda b, pt: (b, 0, 0)),
        scratch_shapes=[page_buf, page_buf, pltpu.SemaphoreType.DMA((2,)), pltpu.SemaphoreType.DMA((2,))])
    return pl.pallas_call(
        functools.partial(_sample_fox_kernel, n_pages=n_pages, ts=ts, n_phys=n_phys, layer=layer), grid_spec=grid_spec,
        out_shape=jax.ShapeDtypeStruct((nb, ts, HEADS_W), F32),
        compiler_params=_cparams(("arbitrary",), 56), name="sample_fox")(
            page_table, fq, kn, vn, an, rt2, cache_kt, cache_vt)


def _wprep_kernel(w_ref, o_ref, *, depth):
    per_col = (D_MODEL // LANES) * depth
    for l in range(depth):
        for rt in range(D_MODEL // LANES):
            blk = w_ref[pl.ds(rt * depth + l, LANES, stride=per_col), :]
            o_ref[l, rt * LANES:(rt + 1) * LANES, :] = blk.T.astype(BF16)


def _wprep(w_in):
    depth, d, c = w_in.shape
    per_col = (d // LANES) * depth
    cp = -(-c // LANES) * LANES
    w2 = jnp.transpose(w_in, (2, 0, 1)).reshape(c, depth, d // LANES, LANES)
    w2 = jnp.transpose(w2, (0, 2, 1, 3)).reshape(c * per_col, LANES)
    w2 = jnp.pad(w2, ((0, (cp - c) * per_col), (0, 0)))
    return pl.pallas_call(
        functools.partial(_wprep_kernel, depth=depth), grid=(cp // LANES,),
        in_specs=[pl.BlockSpec((LANES * per_col, LANES), lambda j: (j, 0))],
        out_specs=pl.BlockSpec((depth, d, LANES), lambda j: (0, 0, j)),
        out_shape=jax.ShapeDtypeStruct((depth, d, cp), BF16),
        compiler_params=_cparams(("parallel",), 32), name="wprep")(w2)


def _layer_weights(l, norm_pre, norm_post, w_rm, gla_w_up, gla_b_gate, gla_norm, pool_w, pool_scale, fox_b_f, w_out):
    idx = [sum(SPLIT_SIZES[:i + 1]) for i in range(len(SPLIT_SIZES) - 1)]
    gq, gk, gv, glr, gg, pv, pg, fq, fk, fv, ff, fg = jnp.split(w_rm[l, :, :sum(SPLIT_SIZES)], idx, axis=1)
    zc = lambda n: jnp.zeros((D_MODEL, n), BF16)
    aux = jnp.concatenate([ff, zc(AUX_LR - N_HEADS), glr, zc(LANES - AUX_LR - GLA_LR)], axis=1)
    w = jnp.concatenate([gq, zc(COL_K - QK_W), gk, zc(COL_GV - COL_K - QK_W), gv, fq, gg, pg, fg, pv, fk, fv, aux], axis=1)
    w_up = jnp.zeros((LANES, QK_W), F32).at[AUX_LR:AUX_LR + GLA_LR].set(gla_w_up[l])
    bf = jnp.zeros((1, LANES), F32).at[0, AUX_FF:AUX_FF + N_HEADS].set(fox_b_f[l])
    pw = jnp.zeros((POOL_W, POOL_W), F32)
    for gi in range(len(POOL_WINDOWS)):
        sl = slice(gi * POOL_GROUP, (gi + 1) * POOL_GROUP)
        pw = pw.at[sl, sl].set(pool_w[l, gi])
    return {"pre": norm_pre[l][None], "post": norm_post[l][None], "w_in": w, "w_up": w_up.astype(BF16),
            "b_gate": gla_b_gate[l][None], "bf": bf, "gla_norm": jnp.tile(gla_norm[l], N_HEADS)[None],
            "pool_w": pw.astype(BF16), "pool_scale": pool_scale[l][None], "w_out": w_out[l].astype(BF16)}


def _constants():
    i = np.arange(HEADS_W)
    gmat = np.where((i[:, None] // HEAD_DIM) == (i[None, :] // HEAD_DIM), 1.0 / HEAD_DIM, 0.0)
    r = np.arange(PAGE)
    tri = r[:, None] >= r[None, :]
    a = np.arange(LANES)
    place = np.stack([(a[:, None] < N_HEADS) & (a[None, :] == a[:, None] + off) for off in (0, 6, 12, 18, 24, 30)])
    sw = np.concatenate([r[:, None] > r[None, :], np.ones((PAGE, PAGE), bool)], axis=1)
    bf = lambda x: jnp.asarray(x.astype(np.float32), dtype=BF16)
    return {"gmat": bf(gmat), "tri": bf(tri), "place": bf(place), "sw": bf(sw)}


def _state_from_bd(st):
    B = st.shape[0]
    s = st.reshape(B, N_HEADS, HEAD_DIM, N_HEADS, GLA_DK)
    s = jnp.stack([s[:, h, :, h, :] for h in range(N_HEADS)], axis=1)
    return jnp.swapaxes(s, 2, 3)


def _sequence_layer(h, lw, consts, prefix, tm, tq, chunk, layer=0, depth=1, kv_t=None):
    n_pre = 0 if prefix is None else prefix["k"].shape[0]
    x = _seqmix(h, lw, consts, prefix, tm, chunk, layer, depth, kv_t)
    h_new = _fox_out(x, h, x["og"], x["op"], lw, consts, tq, n_pre)
    return h_new, x


def _sample_layer(l, h, lw, consts, page_table, rt5, cache_kt, cache_vt, state2, hist16, nb, ts, tm, nseq):
    past = page_table.shape[1] * PAGE
    x = _inproj(h, lw, tm)
    flat = {n: x[n][0] for n in ("q", "k", "g", "pv")}
    flat["v"] = x["v"][0].astype(F32)
    og, op, s_new = _sample_mix(flat, state2, hist16, lw, consts, nseq, ts, past)
    per_seq = lambda a, w: a.reshape(nb, ts, w)
    of = _sample_fox(l, page_table, per_seq(x["fq"][0].astype(F32), HEADS_W), per_seq(x["fk"][0], HEADS_W),
                     per_seq(x["fv"][0], HEADS_W), per_seq(x["aux"][0], LANES), rt5, cache_kt, cache_vt)
    h_new = _outproj(h, og[None], op[None], of.reshape(1, nb * ts, HEADS_W), x["gates"], lw, tm)
    return h_new, x, s_new


def _largest_tile(n, cap):
    t = min(n, cap)
    while n % t:
        t -= 8
    return t


def kernel(x_prompt, x_sample, cache_fox_k, cache_fox_v, cache_fox_logf, state_gla, state_pool, page_table,
           meta_tokens, norm_pre, norm_post, w_in, gla_w_up, gla_b_gate, gla_norm, pool_w, pool_scale,
           fox_b_f, w_out):
    B, T, _ = x_prompt.shape
    nb, ts, _ = x_sample.shape
    depth, n_phys = cache_fox_k.shape[:2]
    consts = _constants()
    cache_kt = jnp.transpose(cache_fox_k, (0, 1, 3, 4, 2))
    cache_vt = jnp.transpose(cache_fox_v, (0, 1, 3, 4, 2))
    lf2 = jnp.transpose(cache_fox_logf, (0, 3, 1, 2)).reshape(depth * N_HEADS * n_phys, PAGE)
    rt5 = _lfsum(lf2, consts["sw"], _largest_tile(depth * N_HEADS * n_phys, 2048))
    state2 = state_gla.reshape(depth, nb, QK_W, HEAD_DIM)
    hist16 = jnp.pad(state_pool, ((0, 0), (0, 0), (HIST_ROWS - POOL_HIST, 0), (0, 0)))
    tm_p, tm_s = _largest_tile(T, 256), _largest_tile(nb * ts, 512)
    tq_p = 2 * tm_p if T % (2 * tm_p) == 0 else tm_p
    w_rm = _wprep(w_in)

    h_m, h_p, h_s = meta_tokens[None], x_prompt, x_sample.reshape(1, nb * ts, D_MODEL)
    out = [[] for _ in range(10)]
    L = N_META + T
    kv_t = (jnp.zeros((depth, B, HEADS_W, L), F32), jnp.zeros((depth, B, HEADS_W, L), F32))
    for l in range(depth):
        lw = _layer_weights(l, norm_pre, norm_post, w_rm, gla_w_up, gla_b_gate, gla_norm, pool_w, pool_scale, fox_b_f, w_out)
        h_m_new, xm = _sequence_layer(h_m, lw, consts, None, N_META, N_META, N_META)
        prefix = {"k": xm["kt"][0, 0].T, "v": xm["vt"][0, 0].T, "aux": xm["aux"][0], "state": xm["state"][0], "hist": xm["hist"][0]}
        h_p, xp = _sequence_layer(h_p, lw, consts, prefix, tm_p, tq_p, GLA_CHUNK, l, depth, kv_t)
        kv_t = (xp["kt"], xp["vt"])
        h_m = h_m_new
        h_s, xs, s_new = _sample_layer(l, h_s, lw, consts, page_table, rt5, cache_kt, cache_vt,
                                       state2[l], hist16[l], nb, ts, tm_s, 8)
        out[2].append(xp["aux"][:, :, :N_HEADS])
        out[3].append(_state_from_bd(xp["state"]))
        out[4].append(xp["hist"][:, HIST_ROWS - POOL_HIST:])
        out[5].append(xs["fk"].reshape(nb, ts, N_HEADS, HEAD_DIM))
        out[6].append(xs["fv"].reshape(nb, ts, N_HEADS, HEAD_DIM))
        out[7].append(xs["aux"][0, :, :N_HEADS].reshape(nb, ts, N_HEADS))
        out[8].append(s_new.reshape(nb, N_HEADS, GLA_DK, HEAD_DIM))
        out[9].append(jnp.concatenate([state_pool[l], xs["pv"][0].reshape(nb, ts, POOL_W)], axis=1)[:, ts:])
    kv_out = tuple(jnp.transpose(a.reshape(depth, B, N_HEADS, HEAD_DIM, L), (0, 1, 4, 2, 3)) for a in kv_t)
    return (h_p, h_s.reshape(nb, ts, D_MODEL)) + kv_out + tuple(jnp.stack(o) for o in out[2:])
```

```python
import functools

import jax
import jax.numpy as jnp
import numpy as np
from jax import lax
from jax.experimental import pallas as pl
from jax.experimental.pallas import tpu as pltpu

F32 = jnp.float32
BF16 = jnp.bfloat16

D_MODEL = 1024
N_HEADS = 6
GLA_DK = 32
HEAD_DIM = 64
GLA_LR = 16
GLA_GATE_NORM = 16.0
GLA_CHUNK = 64
QK_W = N_HEADS * GLA_DK
HEADS_W = N_HEADS * HEAD_DIM
POOL_W = 256
POOL_GROUP = 64
POOL_WINDOWS = (2, 4, 8, 16)
POOL_HIST = 15
HIST_ROWS = 16
N_META = 16
PAGE = 128
EPS = 1e-6
LANES = 128
NEG = -1e30

COL_Q, COL_K, COL_GV, COL_FQ, COL_GATES = 0, 256, 512, 896, 1280
COL_PV, COL_FK, COL_FV, COL_AUX, W_IN_COLS = 2304, 2560, 2944, 3328, 3456
GATES_W = HEADS_W + POOL_W + HEADS_W
AUX_FF, AUX_LR = 0, 8
SPLIT_SIZES = (QK_W, QK_W, HEADS_W, GLA_LR, HEADS_W, POOL_W, POOL_W, HEADS_W, HEADS_W, HEADS_W, N_HEADS, HEADS_W)

AUXK_ONES = (18, 36)


def _cparams(sem, vmem_mb):
    return pltpu.CompilerParams(dimension_semantics=sem, vmem_limit_bytes=vmem_mb * 1024 * 1024)


def _const_spec(shape):
    return pl.BlockSpec(shape, lambda *_: (0,) * len(shape))


def _dot(a, b):
    return jnp.dot(a, b, preferred_element_type=F32)


def _dot_nt(a, b):
    return lax.dot_general(a, b, (((1,), (1,)), ((), ())), preferred_element_type=F32)


def _dot_tn(a, b):
    return lax.dot_general(a, b, (((0,), (0,)), ((), ())), preferred_element_type=F32)


def _log_sigmoid(x):
    return jnp.minimum(x, 0.0) - jnp.log1p(jnp.exp(-jnp.abs(x)))


def _silu(x):
    return x * (1.0 / (1.0 + jnp.exp(-x)))


def _split3(x):
    hi = x.astype(BF16)
    r = x - hi.astype(F32)
    mid = r.astype(BF16)
    lo = (r - mid.astype(F32)).astype(BF16)
    return hi, mid, lo


def _cumsum_rows(x):
    n = x.shape[0]
    row = lax.broadcasted_iota(jnp.int32, x.shape, 0)
    k = 1
    while k < n:
        x = x + jnp.where(row >= k, pltpu.roll(x, k, axis=0), 0.0)
        k *= 2
    return x


def _inproj_kernel(x_ref, pre_ref, w_ref, wup_ref, bg_ref, bf_ref,
                   q_ref, k_ref, g_ref, v_ref, gates_ref, pv_ref, fq_ref, fk_ref, fv_ref, aux_ref):
    x = x_ref[0]
    xn = x * lax.rsqrt(jnp.mean(x * x, axis=-1, keepdims=True) + EPS) * pre_ref[...]
    xb = xn.astype(BF16)

    def seg(c0, width):
        return _dot(xb, w_ref[:, c0:c0 + width])

    q_ref[0] = seg(COL_Q, QK_W) * (GLA_DK ** -0.5)
    k_ref[0] = seg(COL_K, QK_W)
    vq = seg(COL_GV, 2 * HEADS_W)
    v_ref[0] = vq[:, :HEADS_W].astype(BF16)
    fq_ref[0] = (vq[:, HEADS_W:] * (HEAD_DIM ** -0.5)).astype(BF16)
    gates_ref[0] = seg(COL_GATES, GATES_W)
    pv_ref[0] = seg(COL_PV, POOL_W)
    aux = seg(COL_AUX, LANES)
    g_ref[0] = _log_sigmoid(_dot(aux.astype(BF16), wup_ref[...]) + bg_ref[...]) * (1.0 / GLA_GATE_NORM)
    kv = seg(COL_FK, 2 * HEADS_W)
    fk_ref[0] = kv[:, :HEADS_W]
    fv_ref[0] = kv[:, HEADS_W:]
    aux_ref[0] = _log_sigmoid(aux + bf_ref[...])


def _inproj(h, lw, tm):
    B, T, _ = h.shape
    tile = lambda w: pl.BlockSpec((1, tm, w), lambda b, t: (b, t, 0))
    in_specs = [tile(D_MODEL), _const_spec((1, D_MODEL)), _const_spec((D_MODEL, W_IN_COLS)),
                _const_spec((LANES, QK_W)), _const_spec((1, QK_W)), _const_spec((1, LANES))]
    widths = (("q", QK_W, F32), ("k", QK_W, F32), ("g", QK_W, F32), ("v", HEADS_W, BF16), ("gates", GATES_W, F32),
              ("pv", POOL_W, F32), ("fq", HEADS_W, BF16), ("fk", HEADS_W, F32), ("fv", HEADS_W, F32), ("aux", LANES, F32))
    outs = pl.pallas_call(
        _inproj_kernel, grid=(B, T // tm), in_specs=in_specs,
        out_specs=[tile(w) for _, w, _ in widths],
        out_shape=[jax.ShapeDtypeStruct((B, T, w), dt) for _, w, dt in widths],
        compiler_params=_cparams(("parallel", "parallel"), 56), name="inproj")(
            h, lw["pre"], lw["w_in"], lw["w_up"], lw["b_gate"], lw["bf"])
    return dict(zip([n for n, _, _ in widths], outs))


def _gla_masks(chunk):
    n = N_HEADS * chunk
    bd_k = (lax.broadcasted_iota(jnp.int32, (n, QK_W), 0) // chunk) == (lax.broadcasted_iota(jnp.int32, (n, QK_W), 1) // GLA_DK)
    bd_v = (lax.broadcasted_iota(jnp.int32, (n, HEADS_W), 0) // chunk) == (lax.broadcasted_iota(jnp.int32, (n, HEADS_W), 1) // HEAD_DIM)
    tril = (lax.broadcasted_iota(jnp.int32, (chunk, n), 1) % chunk) <= lax.broadcasted_iota(jnp.int32, (chunk, n), 0)
    return bd_k, bd_v, tril


def _gla_intra(q, k, g, v_bf, masks, chunk):
    bd_k, bd_v, tril = masks
    bcum = _cumsum_rows(g)
    qe_bf = (q * jnp.exp(bcum)).astype(BF16)
    ke = k * jnp.exp(-bcum)
    b_end = bcum[chunk - 1:chunk, :]
    kd_bf = (k * jnp.exp(b_end - bcum)).astype(BF16)
    if chunk % 16 == 0:
        ke_rows = jnp.where(bd_k, jnp.concatenate([ke.astype(BF16)] * N_HEADS, axis=0), jnp.zeros((), BF16))
        v_rows = jnp.where(bd_v, jnp.concatenate([v_bf] * N_HEADS, axis=0), jnp.zeros((), BF16))
    else:
        ke_rows = jnp.where(bd_k, jnp.concatenate([ke] * N_HEADS, axis=0), 0.0).astype(BF16)
        v_rows = jnp.where(bd_v, jnp.concatenate([v_bf.astype(F32)] * N_HEADS, axis=0), 0.0).astype(BF16)
    a = jnp.where(tril, _dot_nt(qe_bf, ke_rows), 0.0)
    o_intra = _dot(a.astype(BF16), v_rows)
    return o_intra, qe_bf, kd_bf, jnp.exp(b_end)


def _head_norm(o, gmat_bf, gn):
    o2 = o * o
    hi = o2.astype(BF16)
    lo = (o2 - hi.astype(F32)).astype(BF16)
    ms = _dot(hi, gmat_bf) + _dot(lo, gmat_bf)
    return o * lax.rsqrt(ms + EPS) * gn


def _pool_delta(hist, pv, pos0, tm):
    assert POOL_WINDOWS == (2, 4, 8, 16)
    ext = jnp.concatenate([hist, pv], axis=0)
    lane_g = lax.broadcasted_iota(jnp.int32, (1, POOL_W), 1) // POOL_GROUP
    pos = pos0 + lax.broadcasted_iota(jnp.int32, (tm, 1), 0)
    run = ext
    sums = jnp.zeros((tm, POOL_W), F32)
    cnt = jnp.zeros((tm, POOL_W), F32)
    for gi, w in enumerate(POOL_WINDOWS):
        run = run + pltpu.roll(run, w // 2, axis=0)
        sums = jnp.where(lane_g == gi, run[HIST_ROWS:, :], sums)
        cnt = jnp.where(lane_g == gi, jnp.minimum(pos + 1, w).astype(F32), cnt)
    return sums / cnt - pv, ext[tm:tm + HIST_ROWS, :]


def _seqmix_kernel(*refs, T, tm, chunk, n_pre, aliased):
    if n_pre:
        (x_ref, pre_ref, w_ref, wup_ref, bg_ref, bf_ref, kpre_ref, vpre_ref, apre_ref, s0_ref, hist0_ref, gn_ref, gmat_ref,
         pw_ref, ps_ref) = refs[:15]
        refs = refs[15:]
    else:
        (x_ref, pre_ref, w_ref, wup_ref, bg_ref, bf_ref, s0_ref, hist0_ref, gn_ref, gmat_ref, pw_ref, ps_ref) = refs[:12]
        refs = refs[12:]
    if aliased:
        refs = refs[2:]
    (gates_ref, fq_ref, fk_ref, fv_ref, kt_ref, vt_ref, aux_ref, og_ref, op_ref, sout_ref, hout_ref,
     st_ref, ext_ref, kext_ref, vext_ref) = refs
    t = pl.program_id(1)

    @pl.when(t == 0)
    def _():
        st_ref[...] = s0_ref[...]
        ext_ref[...] = hist0_ref[...]
        if n_pre:
            fk_ref[0, 0:n_pre, :] = kpre_ref[...].astype(BF16)
            fv_ref[0, 0:n_pre, :] = vpre_ref[...].astype(BF16)
            aux_ref[0, 0:n_pre, :] = apre_ref[...]
            kext_ref[0:n_pre, :] = kpre_ref[...]
            vext_ref[0:n_pre, :] = vpre_ref[...]

    def transposed(x):
        r = x.shape[0]
        if r % LANES == 0:
            return x.T
        return jnp.concatenate([x, jnp.zeros((LANES - r, HEADS_W), F32)], axis=0).T[:, :r]

    def emit_transposed(out_ref, carry_ref, tile_rows):
        if n_pre:
            carry_ref[n_pre:n_pre + tm, :] = tile_rows
            tile_rows = carry_ref[0:tm, :]
        lanes = slice(0, tm) if T == tm else pl.ds(pl.multiple_of(t * tm, LANES), tm)
        out_ref[0, 0, :, lanes] = transposed(tile_rows)
        if n_pre:
            tail = carry_ref[tm:tm + n_pre, :]
            carry_ref[0:n_pre, :] = tail

            @pl.when(t == pl.num_programs(1) - 1)
            def _():
                out_ref[0, 0, :, T:T + n_pre] = transposed(tail)

    x = x_ref[0]
    xn = x * lax.rsqrt(jnp.mean(x * x, axis=-1, keepdims=True) + EPS) * pre_ref[...]
    xb = xn.astype(BF16)

    def seg(c0, width):
        return _dot(xb, w_ref[:, c0:c0 + width])

    gates_ref[0] = seg(COL_GATES, GATES_W)
    vq = seg(COL_GV, 2 * HEADS_W)
    fq_ref[0] = (vq[:, HEADS_W:] * (HEAD_DIM ** -0.5)).astype(BF16)
    aux = seg(COL_AUX, LANES)
    rows = pl.ds(pl.multiple_of(n_pre + t * tm, 8), tm)
    kv = seg(COL_FK, 2 * HEADS_W)
    fk, fv = kv[:, :HEADS_W], kv[:, HEADS_W:]
    fk_ref[0, rows, :] = fk.astype(BF16)
    fv_ref[0, rows, :] = fv.astype(BF16)
    emit_transposed(kt_ref, kext_ref, fk)
    emit_transposed(vt_ref, vext_ref, fv)
    aux_ref[0, rows, :] = _log_sigmoid(aux + bf_ref[...])
    q = seg(COL_Q, QK_W) * (GLA_DK ** -0.5)
    k = seg(COL_K, QK_W)
    v_bf = vq[:, :HEADS_W].astype(BF16)
    g = _log_sigmoid(_dot(aux.astype(BF16), wup_ref[...]) + bg_ref[...]) * (1.0 / GLA_GATE_NORM)

    masks = _gla_masks(chunk)
    bd = (lax.broadcasted_iota(jnp.int32, (HEADS_W, QK_W), 0) // HEAD_DIM) == (lax.broadcasted_iota(jnp.int32, (HEADS_W, QK_W), 1) // GLA_DK)
    st = st_ref[...]
    outs = []
    for c in range(tm // chunk):
        r = slice(c * chunk, (c + 1) * chunk)
        o_intra, qe_bf, kd_bf, decay = _gla_intra(q[r], k[r], g[r], v_bf[r], masks, chunk)
        outs.append(o_intra + _dot_nt(qe_bf, st.astype(BF16)))
        st = st * decay + jnp.where(bd, _dot_tn(v_bf[r], kd_bf), 0.0)
    st_ref[...] = st
    o = outs[0] if len(outs) == 1 else jnp.concatenate(outs, axis=0)
    og_ref[0] = _head_norm(o, gmat_ref[...], gn_ref[...])
    d, ext_ref[...] = _pool_delta(ext_ref[...], seg(COL_PV, POOL_W), n_pre + t * tm, tm)
    op_ref[0] = _dot(d.astype(BF16), pw_ref[...]) * ps_ref[...]

    @pl.when(t == pl.num_programs(1) - 1)
    def _():
        sout_ref[0] = st
        hout_ref[0] = ext_ref[...]


def _seqmix(h, lw, consts, prefix, tm, chunk, layer=0, depth=1, kv_t=None):
    B, T, _ = h.shape
    n_pre = 0 if prefix is None else prefix["k"].shape[0]
    L = n_pre + T
    tile = lambda w: pl.BlockSpec((1, tm, w), lambda b, t: (b, t, 0))
    whole = lambda w: pl.BlockSpec((1, L, w), lambda b, t: (b, 0, 0))
    per_seq = lambda r, w: pl.BlockSpec((1, r, w), lambda b, t: (b, 0, 0))
    in_specs = [tile(D_MODEL), _const_spec((1, D_MODEL)), _const_spec((D_MODEL, W_IN_COLS)),
                _const_spec((LANES, QK_W)), _const_spec((1, QK_W)), _const_spec((1, LANES))]
    args = [h, lw["pre"], lw["w_in"], lw["w_up"], lw["b_gate"], lw["bf"]]
    if n_pre:
        in_specs += [_const_spec((n_pre, HEADS_W)), _const_spec((n_pre, HEADS_W)), _const_spec((n_pre, LANES))]
        args += [prefix["k"], prefix["v"], prefix["aux"]]
    in_specs += [_const_spec((HEADS_W, QK_W)), _const_spec((HIST_ROWS, POOL_W)), _const_spec((1, HEADS_W)),
                 _const_spec((HEADS_W, HEADS_W)), _const_spec((POOL_W, POOL_W)), _const_spec((1, POOL_W))]
    s0 = jnp.zeros((HEADS_W, QK_W), F32) if prefix is None else prefix["state"]
    hist0 = jnp.zeros((HIST_ROWS, POOL_W), F32) if prefix is None else prefix["hist"]
    args += [s0, hist0, lw["gla_norm"], consts["gmat"], lw["pool_w"], lw["pool_scale"]]
    aliases = {}
    if kv_t is not None:
        aliases = {len(args): 4, len(args) + 1: 5}
        in_specs += [pl.BlockSpec(memory_space=pl.ANY)] * 2
        args += list(kv_t)
    slab = pl.BlockSpec((1, 1, HEADS_W, L), lambda b, t: (layer, b, 0, 0))
    out_shape = [jax.ShapeDtypeStruct((B, T, GATES_W), F32), jax.ShapeDtypeStruct((B, T, HEADS_W), BF16),
                 jax.ShapeDtypeStruct((B, L, HEADS_W), BF16), jax.ShapeDtypeStruct((B, L, HEADS_W), BF16),
                 jax.ShapeDtypeStruct((depth, B, HEADS_W, L), F32), jax.ShapeDtypeStruct((depth, B, HEADS_W, L), F32),
                 jax.ShapeDtypeStruct((B, L, LANES), F32), jax.ShapeDtypeStruct((B, T, HEADS_W), F32),
                 jax.ShapeDtypeStruct((B, T, POOL_W), F32), jax.ShapeDtypeStruct((B, HEADS_W, QK_W), F32),
                 jax.ShapeDtypeStruct((B, HIST_ROWS, POOL_W), F32)]
    out_specs = [tile(GATES_W), tile(HEADS_W), whole(HEADS_W), whole(HEADS_W), slab, slab, whole(LANES), tile(HEADS_W),
                 tile(POOL_W), per_seq(HEADS_W, QK_W), per_seq(HIST_ROWS, POOL_W)]
    names = ("gates", "fq", "fk", "fv", "kt", "vt", "aux", "og", "op", "state", "hist")
    outs = pl.pallas_call(
        functools.partial(_seqmix_kernel, T=T, tm=tm, chunk=chunk, n_pre=n_pre, aliased=kv_t is not None),
        grid=(B, T // tm), in_specs=in_specs, out_specs=out_specs, out_shape=out_shape,
        input_output_aliases=aliases,
        scratch_shapes=[pltpu.VMEM((HEADS_W, QK_W), F32), pltpu.VMEM((HIST_ROWS, POOL_W), F32),
                        pltpu.VMEM((tm + n_pre, HEADS_W), F32), pltpu.VMEM((tm + n_pre, HEADS_W), F32)],
        compiler_params=_cparams(("parallel", "arbitrary"), 56), name="seqmix")(*args)
    return dict(zip(names, outs))


def _cumsum_blocks(lf, carry, tri_bf, cb):
    nblk = lf.shape[0] // cb
    x = lf if nblk == 1 else jnp.concatenate([lf[i * cb:(i + 1) * cb, :] for i in range(nblk)], axis=1)
    tri = tri_bf[0:cb, 0:cb]
    hi, mid, lo = _split3(x)
    c = _dot(tri, hi) + _dot(tri, mid) + _dot(tri, lo)
    out = []
    for i in range(nblk):
        blk = c[:, i * LANES:(i + 1) * LANES] + carry
        carry = blk[cb - 1:cb, :]
        out.append(blk)
    return (out[0] if nblk == 1 else jnp.concatenate(out, axis=0)), carry


def _fox_features(f, place_ref):
    lane = lax.broadcasted_iota(jnp.int32, (1, LANES), 1)
    fh, fm, fl = _split3(f)
    ones_k = jnp.where((lane >= AUXK_ONES[0]) & (lane < AUXK_ONES[1]), 1.0, 0.0)
    ones_q = jnp.where(lane < AUXK_ONES[0], 1.0, 0.0)
    kaux = ones_k - (_dot(fh, place_ref[0]) + _dot(fm, place_ref[1]) + _dot(fl, place_ref[2]))
    qaux = ones_q + (_dot(fh, place_ref[3]) + _dot(fm, place_ref[4]) + _dot(fl, place_ref[5]))
    return kaux.astype(BF16), qaux.astype(BF16)


def _fox_kernel(fq_ref, k_ref, v_ref, aux_ref, tri_ref, place_ref, h_ref, og_ref, op_ref, gates_ref, wo_ref, post_ref,
                out_ref, kaux_ref, qaux_ref, q2_ref, m_ref, acc_ref, sa_ref, sb_ref, *, T, n_pre, tq, tk):
    t = pl.program_id(1)
    r = tq // tk
    assert r * tk == tq and (r % 2 == 0 or T == tq)

    @pl.when(t == 0)
    def _():
        carry = jnp.zeros((1, LANES), F32)
        for r0, n in ([(0, n_pre)] if n_pre else []) + [(n_pre, T)]:
            f, carry = _cumsum_blocks(aux_ref[0, r0:r0 + n, :], carry, tri_ref[...], min(PAGE, n))
            kaux_ref[r0:r0 + n, :], qaux_ref[r0:r0 + n, :] = _fox_features(f, place_ref)

    lane = lax.broadcasted_iota(jnp.int32, (1, LANES), 1)
    q0 = pl.multiple_of(n_pre + t * tq, 16)
    qa = qaux_ref[pl.ds(q0, tq), :]
    fq = fq_ref[0]
    zero_bf = jnp.zeros((tq, LANES), BF16)
    for h in range(N_HEADS):
        fq_p = fq[:, (h // 2) * LANES:(h // 2 + 1) * LANES]
        qmask = (lane == h) | (lane == 6 + h) | (lane == 12 + h) | (lane == 18 + h) | (lane == 24 + h) | (lane == 30 + h)
        q2_ref[h] = jnp.concatenate([jnp.where((lane // HEAD_DIM) == (h % 2), fq_p, zero_bf), jnp.where(qmask, qa, zero_bf)], axis=1)
    m_ref[...] = jnp.full(m_ref.shape, NEG, F32)
    acc_ref[...] = jnp.zeros(acc_ref.shape, F32)

    def logits(j, n, s_out):
        r0 = 0 if j is None else pl.multiple_of(n_pre + j * tk, 16)
        kaux = kaux_ref[pl.ds(r0, n), :]
        for p in range(N_HEADS // 2):
            k2 = jnp.concatenate([k_ref[0, pl.ds(r0, n), p * LANES:(p + 1) * LANES], kaux], axis=1)
            for h in (2 * p, 2 * p + 1):
                s_out[h] = _dot_nt(q2_ref[h], k2)

    def update(j, n, s_in, diag):
        r0 = 0 if j is None else pl.multiple_of(n_pre + j * tk, 16)
        ones = jnp.ones((n, LANES), BF16)
        causal = diag is not None
        if causal:
            keep = lax.broadcasted_iota(jnp.int32, (tq, n), 0) >= lax.broadcasted_iota(jnp.int32, (tq, n), 1) + diag * tk
        for p in range(N_HEADS // 2):
            vp = v_ref[0, pl.ds(r0, n), p * LANES:(p + 1) * LANES]
            for h in (2 * p, 2 * p + 1):
                v2 = jnp.where((lane // HEAD_DIM) == (h % 2), vp, ones)
                def read(z):
                    s = s_in[h] if j is None else s_in[h + jnp.minimum(z, 0)]
                    return jnp.where(keep, s, NEG) if causal else s
                m_old = m_ref[h]
                m_new = jnp.maximum(m_old, jnp.max(read(t), axis=-1, keepdims=True))
                alpha = jnp.exp(m_old - m_new)
                m_b = m_new[:, :n] if n < LANES else jnp.concatenate([m_new] * (n // LANES), axis=1)
                pe = jnp.exp(read(pl.program_id(0)) - m_b).astype(BF16)
                acc_ref[h] = alpha * acc_ref[h] + _dot(pe, v2)
                m_ref[h] = m_new

    if n_pre:
        pre = {}
        logits(None, n_pre, pre)
        update(None, n_pre, pre, None)

    logits(0, tk, sa_ref)

    def body(jj, c):
        j = 2 * jj
        logits(j + 1, tk, sb_ref)
        update(j, tk, sa_ref, None)
        logits(j + 2, tk, sa_ref)
        update(j + 1, tk, sb_ref, None)
        return c

    lax.fori_loop(0, (r * t) // 2, body, 0)
    bufs = (sa_ref, sb_ref)
    for i in range(r):
        if i + 1 < r:
            logits(r * t + i + 1, tk, bufs[(i + 1) % 2])
        update(r * t + i, tk, bufs[i % 2], i)

    o_fox = []
    for p in range(N_HEADS // 2):
        a0, a1 = acc_ref[2 * p], acc_ref[2 * p + 1]
        o_fox.append(jnp.where((lane // HEAD_DIM) == 0, a0 / pltpu.roll(a0, HEAD_DIM, axis=1),
                               a1 / pltpu.roll(a1, HEAD_DIM, axis=1)))
    y_in = jnp.concatenate([og_ref[0], op_ref[0]] + o_fox, axis=1) * _silu(gates_ref[0])
    y = _dot(y_in.astype(BF16), wo_ref[...])
    out_ref[0] = h_ref[0] + y * lax.rsqrt(jnp.mean(y * y, axis=-1, keepdims=True) + EPS) * post_ref[...]


def _fox_out(x, h, og, op, lw, consts, tq, n_pre):
    B, T, _ = x["fq"].shape
    L = n_pre + T
    whole = lambda w: pl.BlockSpec((1, L, w), lambda b, t: (b, 0, 0))
    tile = lambda w: pl.BlockSpec((1, tq, w), lambda b, t: (b, t, 0))
    in_specs = [tile(HEADS_W), whole(HEADS_W), whole(HEADS_W), whole(LANES),
                _const_spec((PAGE, PAGE)), _const_spec((6, LANES, LANES)),
                tile(D_MODEL), tile(HEADS_W), tile(POOL_W), tile(GATES_W), _const_spec((GATES_W, D_MODEL)), _const_spec((1, D_MODEL))]
    tk = min(tq, 256)
    return pl.pallas_call(
        functools.partial(_fox_kernel, T=T, n_pre=n_pre, tq=tq, tk=tk),
        grid=(B, T // tq), in_specs=in_specs,
        out_specs=tile(D_MODEL),
        out_shape=jax.ShapeDtypeStruct((B, T, D_MODEL), F32),
        scratch_shapes=[pltpu.VMEM((L, LANES), BF16), pltpu.VMEM((L, LANES), BF16),
                        pltpu.VMEM((N_HEADS, tq, 2 * LANES), BF16), pltpu.VMEM((N_HEADS, tq, LANES), F32),
                        pltpu.VMEM((N_HEADS, tq, LANES), F32),
                        pltpu.VMEM((N_HEADS, tq, tk), F32), pltpu.VMEM((N_HEADS, tq, tk), F32)],
        compiler_params=_cparams(("parallel", "arbitrary"), 56), name="fox")(
            x["fq"], x["fk"], x["fv"], x["aux"], consts["tri"], consts["place"],
            h, og, op, x["gates"], lw["w_out"], lw["post"])


def _outproj_kernel(h_ref, og_ref, op_ref, of_ref, gates_ref, wo_ref, post_ref, out_ref):
    y_in = jnp.concatenate([og_ref[0], op_ref[0], of_ref[0]], axis=1) * _silu(gates_ref[0])
    y = _dot(y_in.astype(BF16), wo_ref[...])
    out_ref[0] = h_ref[0] + y * lax.rsqrt(jnp.mean(y * y, axis=-1, keepdims=True) + EPS) * post_ref[...]


def _outproj(h, og, op, of, gates, lw, tm):
    B, T, _ = h.shape
    tile = lambda w: pl.BlockSpec((1, tm, w), lambda b, t: (b, t, 0))
    return pl.pallas_call(
        _outproj_kernel, grid=(B, T // tm),
        in_specs=[tile(D_MODEL), tile(HEADS_W), tile(POOL_W), tile(HEADS_W), tile(GATES_W),
                  _const_spec((GATES_W, D_MODEL)), _const_spec((1, D_MODEL))],
        out_specs=tile(D_MODEL), out_shape=jax.ShapeDtypeStruct((B, T, D_MODEL), F32),
        compiler_params=_cparams(("parallel", "parallel"), 48), name="outproj")(
            h, og, op, of, gates, lw["w_out"], lw["post"])


def _sample_mix_kernel(q_ref, k_ref, g_ref, v_ref, pv_ref, s_ref, hist_ref, gn_ref, gmat_ref, pw_ref, ps_ref,
                       og_ref, op_ref, sout_ref, *, nseq, ts, pos0):
    masks = _gla_masks(ts)
    rr = lax.broadcasted_iota(jnp.int32, (QK_W, HEADS_W), 0) // GLA_DK
    cc = lax.broadcasted_iota(jnp.int32, (QK_W, HEADS_W), 1) // HEAD_DIM
    bd = rr == cc
    ones = jnp.ones((3 * ts, LANES), BF16)
    outs, deltas = [], []
    for i in range(nseq):
        r = slice(i * ts, (i + 1) * ts)
        g = g_ref[r, :]
        v_bf = v_ref[r, :].astype(BF16)
        o_intra, qe_bf, kd_bf, _ = _gla_intra(q_ref[r, :], k_ref[r, :], g, v_bf, masks, ts)
        s1 = s_ref[i]
        s2 = jnp.concatenate([s1, s1], axis=1)
        s_bd = jnp.where(bd, jnp.concatenate([s2, s2, s2], axis=1), 0.0).astype(BF16)
        outs.append(o_intra + _dot(qe_bf, s_bd))
        u = jnp.where(bd, _dot_tn(kd_bf, v_bf), 0.0)
        x = jnp.concatenate([u[64 * j:64 * (j + 1), LANES * j:LANES * (j + 1)] for j in range(N_HEADS // 2)], axis=0)
        x = x + pltpu.roll(x, HEAD_DIM, axis=1)
        gh, gm, gl = _split3(g)
        gparts = jnp.concatenate([gh.astype(F32), gm.astype(F32), gl.astype(F32)], axis=0).astype(BF16)
        sout_ref[i] = (s2 * jnp.exp(_dot_tn(gparts, ones)) + x)[:, :HEAD_DIM]
        deltas.append(_pool_delta(hist_ref[i], pv_ref[r, :], pos0, ts)[0])
    og_ref[...] = _head_norm(jnp.concatenate(outs, axis=0), gmat_ref[...], gn_ref[...])
    op_ref[...] = _dot(jnp.concatenate(deltas, axis=0).astype(BF16), pw_ref[...]) * ps_ref[...]


def _sample_mix(x, state2, hist16, lw, consts, nseq, ts, pos0):
    n = x["q"].shape[0]
    rows = nseq * ts
    tile = lambda w: pl.BlockSpec((rows, w), lambda i: (i, 0))
    in_specs = [tile(QK_W), tile(QK_W), tile(QK_W), tile(HEADS_W), tile(POOL_W),
                pl.BlockSpec((nseq, QK_W, HEAD_DIM), lambda i: (i, 0, 0)), pl.BlockSpec((nseq, HIST_ROWS, POOL_W), lambda i: (i, 0, 0)),
                _const_spec((1, HEADS_W)), _const_spec((HEADS_W, HEADS_W)), _const_spec((POOL_W, POOL_W)), _const_spec((1, POOL_W))]
    out_shape = [jax.ShapeDtypeStruct((n, HEADS_W), F32), jax.ShapeDtypeStruct((n, POOL_W), F32),
                 jax.ShapeDtypeStruct(state2.shape, F32)]
    out_specs = [tile(HEADS_W), tile(POOL_W), pl.BlockSpec((nseq, QK_W, HEAD_DIM), lambda i: (i, 0, 0))]
    return pl.pallas_call(
        functools.partial(_sample_mix_kernel, nseq=nseq, ts=ts, pos0=pos0),
        grid=(n // rows,), in_specs=in_specs, out_specs=out_specs, out_shape=out_shape,
        compiler_params=_cparams(("parallel",), 48), name="sample_mix")(
            x["q"], x["k"], x["g"], x["v"], x["pv"], state2, hist16, lw["gla_norm"], consts["gmat"], lw["pool_w"], lw["pool_scale"])


def _suffix_rows(x, period):
    n = x.shape[0]
    row = lax.broadcasted_iota(jnp.int32, x.shape, 0) % period
    k = 1
    while k < period:
        x = x + jnp.where(row + k < period, pltpu.roll(x, n - k, axis=0), 0.0)
        k *= 2
    return x


def _lfsum_kernel(lf_ref, sw_ref, o_ref):
    n = lf_ref.shape[0]
    hi, mid, lo = _split3(lf_ref[...])
    parts = jnp.concatenate([hi.astype(F32), mid.astype(F32), lo.astype(F32)], axis=0).astype(BF16)
    r3 = _dot(parts, sw_ref[...])
    o_ref[...] = r3[0:n] + r3[n:2 * n] + r3[2 * n:3 * n]


def _lfsum(lf2, sw, tr):
    n = lf2.shape[0]
    return pl.pallas_call(
        _lfsum_kernel, grid=(n // tr,),
        in_specs=[pl.BlockSpec((tr, PAGE), lambda i: (i, 0)), _const_spec((PAGE, 2 * PAGE))],
        out_specs=pl.BlockSpec((tr, 2 * PAGE), lambda i: (i, 0)), out_shape=jax.ShapeDtypeStruct((n, 2 * PAGE), F32),
        compiler_params=_cparams(("parallel",), 32), name="lfsum")(lf2, sw)


def _sample_fox_kernel(pt_ref, fq_ref, kn_ref, vn_ref, an_ref, rt_ref, ck_ref, cv_ref, o_ref,
                       kbuf, vbuf, ksem, vsem, *, n_pages, ts, n_phys, layer):
    b = pl.program_id(0)
    nb = pl.num_programs(0)
    slot = b % 2
    nxt = jnp.minimum(b + 1, nb - 1)

    def k_copy(seq, s, p):
        return pltpu.make_async_copy(ck_ref.at[layer, pt_ref[seq, p]], kbuf.at[s, p], ksem.at[s])

    def v_copy(seq, s, p):
        return pltpu.make_async_copy(cv_ref.at[layer, pt_ref[seq, p]], vbuf.at[s, p], vsem.at[s])

    @pl.when(b == 0)
    def _():
        for p in range(n_pages):
            k_copy(0, 0, p).start()
            v_copy(0, 0, p).start()

    for p in range(n_pages):
        k_copy(b, slot, p).wait()
    r2 = jnp.concatenate([rt_ref[pl.ds(h * n_phys + pt_ref[b, p], 1), :]
                          for h in range(N_HEADS) for p in range(n_pages)], axis=0)
    tot = r2[:, PAGE:]
    rfull = r2[:, :PAGE] + (_suffix_rows(tot, n_pages) - tot)
    cn = _cumsum_rows(an_ref[0])
    cn2 = jnp.concatenate([cn, cn], axis=0)
    cparts = _split3(cn)
    lane = lax.broadcasted_iota(jnp.int32, (2 * ts, LANES), 1)
    first = lax.broadcasted_iota(jnp.int32, (2 * ts, LANES), 0) < ts
    own = (lane < HEAD_DIM) == first
    causal = (lax.broadcasted_iota(jnp.int32, (2 * ts, ts), 0) % ts) >= lax.broadcasted_iota(jnp.int32, (2 * ts, ts), 1)
    for pr in range(N_HEADS // 2):
        cols = slice(pr * LANES, (pr + 1) * LANES)
        fq2 = jnp.concatenate([fq_ref[0, :, cols], fq_ref[0, :, cols]], axis=0)
        q2 = jnp.where(own, fq2, 0.0).astype(BF16)
        onehot = lane == jnp.where(first, 2 * pr, 2 * pr + 1)
        cq = jnp.sum(jnp.where(onehot, cn2, 0.0), axis=-1, keepdims=True)
        sel = jnp.where(onehot, 1.0, 0.0).astype(BF16)
        cn_t = _dot_nt(sel, cparts[0]) + _dot_nt(sel, cparts[1]) + _dot_nt(sel, cparts[2])
        s_new = jnp.where(causal, _dot_nt(q2, kn_ref[0, :, cols].astype(BF16)) + cq - cn_t, NEG)
        s_pages = []
        for p in range(n_pages):
            if pr == 0:
                k_copy(nxt, 1 - slot, p).start()
            kt = kbuf[slot, p, 2 * pr:2 * pr + 2].reshape(LANES, PAGE).astype(BF16)
            r0, r1 = 2 * pr * n_pages + p, (2 * pr + 1) * n_pages + p
            bias = jnp.concatenate([jnp.broadcast_to(rfull[r0:r0 + 1, :], (ts, PAGE)),
                                    jnp.broadcast_to(rfull[r1:r1 + 1, :], (ts, PAGE))], axis=0)
            s_pages.append(_dot(q2, kt) + bias + cq)
        mm = s_pages[0]
        for s in s_pages[1:]:
            mm = jnp.maximum(mm, s)
        m = jnp.maximum(jnp.max(mm, axis=-1, keepdims=True), jnp.max(s_new, axis=-1, keepdims=True))
        p_new = jnp.exp(s_new - m)
        l = jnp.sum(p_new, axis=-1, keepdims=True)
        acc = _dot(p_new.astype(BF16), vn_ref[0, :, cols].astype(BF16))
        if pr == 0:
            for p in range(n_pages):
                v_copy(b, slot, p).wait()
        for p in range(n_pages):
            if pr == 0:
                v_copy(nxt, 1 - slot, p).start()
            pe = jnp.exp(s_pages[p] - m)
            l = l + jnp.sum(pe, axis=-1, keepdims=True)
            vt = vbuf[slot, p, 2 * pr:2 * pr + 2].reshape(LANES, PAGE).astype(BF16)
            acc = acc + _dot_nt(pe.astype(BF16), vt)
        o2 = acc / l
        o_ref[0, :, cols] = jnp.where(lane[:ts] < HEAD_DIM, o2[:ts], o2[ts:])

    @pl.when(b == nb - 1)
    def _():
        for p in range(n_pages):
            k_copy(nxt, 1 - slot, p).wait()
            v_copy(nxt, 1 - slot, p).wait()


def _sample_fox(layer, page_table, fq, kn, vn, an, rt2, cache_kt, cache_vt):
    nb, ts, _ = fq.shape
    n_pages = page_table.shape[1]
    n_phys = cache_kt.shape[1]
    seq = lambda w: pl.BlockSpec((1, ts, w), lambda b, pt: (b, 0, 0))
    rt_spec = pl.BlockSpec((N_HEADS * n_phys, 2 * PAGE), lambda b, pt: (layer, 0), pipeline_mode=pl.Buffered(1))
    hbm = pl.BlockSpec(memory_space=pl.ANY)
    page_buf = pltpu.VMEM((2, n_pages, N_HEADS, HEAD_DIM, PAGE), F32)
    grid_spec = pltpu.PrefetchScalarGridSpec(
        num_scalar_prefetch=1, grid=(nb,),
        in_specs=[seq(HEADS_W), seq(HEADS_W), seq(HEADS_W), seq(LANES), rt_spec, hbm, hbm],
        out_specs=pl.BlockSpec((1, ts, HEADS_W), lambda b, pt: (b, 0, 0)),
        scratch_shapes=[page_buf, page_buf, pltpu.SemaphoreType.DMA((2,)), pltpu.SemaphoreType.DMA((2,))])
    return pl.pallas_call(
        functools.partial(_sample_fox_kernel, n_pages=n_pages, ts=ts, n_phys=n_phys, layer=layer), grid_spec=grid_spec,
        out_shape=jax.ShapeDtypeStruct((nb, ts, HEADS_W), F32),
        compiler_params=_cparams(("arbitrary",), 56), name="sample_fox")(
            page_table, fq, kn, vn, an, rt2, cache_kt, cache_vt)


def _wprep_kernel(w_ref, o_ref, *, depth):
    per_col = (D_MODEL // LANES) * depth
    for l in range(depth):
        for rt in range(D_MODEL // LANES):
            blk = w_ref[pl.ds(rt * depth + l, LANES, stride=per_col), :]
            o_ref[l, rt * LANES:(rt + 1) * LANES, :] = blk.T.astype(BF16)


def _wprep(w_in):
    depth, d, c = w_in.shape
    per_col = (d // LANES) * depth
    cp = -(-c // LANES) * LANES
    w2 = jnp.transpose(w_in, (2, 0, 1)).reshape(c, depth, d // LANES, LANES)
    w2 = jnp.transpose(w2, (0, 2, 1, 3)).reshape(c * per_col, LANES)
    w2 = jnp.pad(w2, ((0, (cp - c) * per_col), (0, 0)))
    return pl.pallas_call(
        functools.partial(_wprep_kernel, depth=depth), grid=(cp // LANES,),
        in_specs=[pl.BlockSpec((LANES * per_col, LANES), lambda j: (j, 0))],
        out_specs=pl.BlockSpec((depth, d, LANES), lambda j: (0, 0, j)),
        out_shape=jax.ShapeDtypeStruct((depth, d, cp), BF16),
        compiler_params=_cparams(("parallel",), 32), name="wprep")(w2)


def _pack_weights(norm_pre, norm_post, w_rm, gla_w_up, gla_b_gate, gla_norm, pool_w, pool_scale, fox_b_f, w_out):
    depth = w_rm.shape[0]
    idx = [sum(SPLIT_SIZES[:i + 1]) for i in range(len(SPLIT_SIZES) - 1)]
    gq, gk, gv, glr, gg, pv, pg, fq, fk, fv, ff, fg = jnp.split(w_rm[:, :, :sum(SPLIT_SIZES)], idx, axis=2)
    zc = lambda n: jnp.zeros((depth, D_MODEL, n), BF16)
    aux = jnp.concatenate([ff, zc(AUX_LR - N_HEADS), glr, zc(LANES - AUX_LR - GLA_LR)], axis=2)
    w = jnp.concatenate([gq, zc(COL_K - QK_W), gk, zc(COL_GV - COL_K - QK_W), gv, fq, gg, pg, fg, pv, fk, fv, aux], axis=2)
    w_up = jnp.zeros((depth, LANES, QK_W), F32).at[:, AUX_LR:AUX_LR + GLA_LR].set(gla_w_up)
    bf = jnp.zeros((depth, 1, LANES), F32).at[:, 0, AUX_FF:AUX_FF + N_HEADS].set(fox_b_f)
    pw = jnp.zeros((depth, POOL_W, POOL_W), F32)
    for gi in range(len(POOL_WINDOWS)):
        sl = slice(gi * POOL_GROUP, (gi + 1) * POOL_GROUP)
        pw = pw.at[:, sl, sl].set(pool_w[:, gi])
    return {"pre": norm_pre[:, None], "post": norm_post[:, None], "w_in": w, "w_up": w_up.astype(BF16),
            "b_gate": gla_b_gate[:, None], "bf": bf, "gla_norm": jnp.tile(gla_norm, (1, N_HEADS))[:, None],
            "pool_w": pw.astype(BF16), "pool_scale": pool_scale[:, None], "w_out": w_out.astype(BF16)}


def _constants():
    i = np.arange(HEADS_W)
    gmat = np.where((i[:, None] // HEAD_DIM) == (i[None, :] // HEAD_DIM), 1.0 / HEAD_DIM, 0.0)
    r = np.arange(PAGE)
    tri = r[:, None] >= r[None, :]
    a = np.arange(LANES)
    place = np.stack([(a[:, None] < N_HEADS) & (a[None, :] == a[:, None] + off) for off in (0, 6, 12, 18, 24, 30)])
    sw = np.concatenate([r[:, None] > r[None, :], np.ones((PAGE, PAGE), bool)], axis=1)
    bf = lambda x: jnp.asarray(x.astype(np.float32), dtype=BF16)
    return {"gmat": bf(gmat), "tri": bf(tri), "place": bf(place), "sw": bf(sw)}


def _state_from_bd(st):
    B = st.shape[0]
    s = st.reshape(B, N_HEADS, HEAD_DIM, N_HEADS, GLA_DK)
    s = jnp.stack([s[:, h, :, h, :] for h in range(N_HEADS)], axis=1)
    return jnp.swapaxes(s, 2, 3)


def _sequence_layer(h, lw, consts, prefix, tm, tq, chunk, layer=0, depth=1, kv_t=None):
    n_pre = 0 if prefix is None else prefix["k"].shape[0]
    x = _seqmix(h, lw, consts, prefix, tm, chunk, layer, depth, kv_t)
    h_new = _fox_out(x, h, x["og"], x["op"], lw, consts, tq, n_pre)
    return h_new, x


def _sample_layer(l, h, lw, consts, page_table, rt5, cache_kt, cache_vt, state2, hist16, nb, ts, tm, nseq):
    past = page_table.shape[1] * PAGE
    x = _inproj(h, lw, tm)
    flat = {n: x[n][0] for n in ("q", "k", "g", "pv")}
    flat["v"] = x["v"][0].astype(F32)
    og, op, s_new = _sample_mix(flat, state2, hist16, lw, consts, nseq, ts, past)
    per_seq = lambda a, w: a.reshape(nb, ts, w)
    of = _sample_fox(l, page_table, per_seq(x["fq"][0].astype(F32), HEADS_W), per_seq(x["fk"][0], HEADS_W),
                     per_seq(x["fv"][0], HEADS_W), per_seq(x["aux"][0], LANES), rt5, cache_kt, cache_vt)
    h_new = _outproj(h, og[None], op[None], of.reshape(1, nb * ts, HEADS_W), x["gates"], lw, tm)
    return h_new, x, s_new


def _largest_tile(n, cap):
    t = min(n, cap)
    while n % t:
        t -= 8
    return t


def kernel(x_prompt, x_sample, cache_fox_k, cache_fox_v, cache_fox_logf, state_gla, state_pool, page_table,
           meta_tokens, norm_pre, norm_post, w_in, gla_w_up, gla_b_gate, gla_norm, pool_w, pool_scale,
           fox_b_f, w_out):
    B, T, _ = x_prompt.shape
    nb, ts, _ = x_sample.shape
    depth, n_phys = cache_fox_k.shape[:2]
    consts = _constants()
    cache_kt = jnp.transpose(cache_fox_k, (0, 1, 3, 4, 2))
    cache_vt = jnp.transpose(cache_fox_v, (0, 1, 3, 4, 2))
    lf2 = jnp.transpose(cache_fox_logf, (0, 3, 1, 2)).reshape(depth * N_HEADS * n_phys, PAGE)
    rt5 = _lfsum(lf2, consts["sw"], _largest_tile(depth * N_HEADS * n_phys, 2048))
    state2 = state_gla.reshape(depth, nb, QK_W, HEAD_DIM)
    hist16 = jnp.pad(state_pool, ((0, 0), (0, 0), (HIST_ROWS - POOL_HIST, 0), (0, 0)))
    tm_p, tm_s = _largest_tile(T, 256), _largest_tile(nb * ts, 512)
    tq_p = 2 * tm_p if T % (2 * tm_p) == 0 else tm_p
    w_rm = _wprep(w_in)

    h_m, h_p, h_s = meta_tokens[None], x_prompt, x_sample.reshape(1, nb * ts, D_MODEL)
    out = [[] for _ in range(10)]
    L = N_META + T
    kv_t = (jnp.zeros((depth, B, HEADS_W, L), F32), jnp.zeros((depth, B, HEADS_W, L), F32))
    packed = _pack_weights(norm_pre, norm_post, w_rm, gla_w_up, gla_b_gate, gla_norm, pool_w, pool_scale, fox_b_f, w_out)
    for l in range(depth):
        lw = {name: a[l] for name, a in packed.items()}
        h_m_new, xm = _sequence_layer(h_m, lw, consts, None, N_META, N_META, N_META)
        prefix = {"k": xm["kt"][0, 0].T, "v": xm["vt"][0, 0].T, "aux": xm["aux"][0], "state": xm["state"][0], "hist": xm["hist"][0]}
        h_p, xp = _sequence_layer(h_p, lw, consts, prefix, tm_p, tq_p, GLA_CHUNK, l, depth, kv_t)
        kv_t = (xp["kt"], xp["vt"])
        h_m = h_m_new
        h_s, xs, s_new = _sample_layer(l, h_s, lw, consts, page_table, rt5, cache_kt, cache_vt,
                                       state2[l], hist16[l], nb, ts, tm_s, 8)
        out[2].append(xp["aux"][:, :, :N_HEADS])
        out[3].append(_state_from_bd(xp["state"]))
        out[4].append(xp["hist"][:, HIST_ROWS - POOL_HIST:])
        out[5].append(xs["fk"].reshape(nb, ts, N_HEADS, HEAD_DIM))
        out[6].append(xs["fv"].reshape(nb, ts, N_HEADS, HEAD_DIM))
        out[7].append(xs["aux"][0, :, :N_HEADS].reshape(nb, ts, N_HEADS))
        out[8].append(s_new.reshape(nb, N_HEADS, GLA_DK, HEAD_DIM))
        out[9].append(jnp.concatenate([state_pool[l], xs["pv"][0].reshape(nb, ts, POOL_W)], axis=1)[:, ts:])
    kv_out = tuple(jnp.transpose(a.reshape(depth, B, N_HEADS, HEAD_DIM, L), (0, 1, 4, 2, 3)) for a in kv_t)
    return (h_p, h_s.reshape(nb, ts, D_MODEL)) + kv_out + tuple(jnp.stack(o) for o in out[2:])
```

```python
import functools

import jax
import jax.numpy as jnp
import numpy as np
from jax import lax
from jax.experimental import pallas as pl
from jax.experimental.pallas import tpu as pltpu

F32 = jnp.float32
BF16 = jnp.bfloat16

D_MODEL = 1024
N_HEADS = 6
GLA_DK = 32
HEAD_DIM = 64
GLA_LR = 16
GLA_GATE_NORM = 16.0
GLA_CHUNK = 64
QK_W = N_HEADS * GLA_DK
HEADS_W = N_HEADS * HEAD_DIM
POOL_W = 256
POOL_GROUP = 64
POOL_WINDOWS = (2, 4, 8, 16)
POOL_HIST = 15
HIST_ROWS = 16
N_META = 16
PAGE = 128
EPS = 1e-6
LANES = 128
NEG = -1e30

COL_Q, COL_K, COL_GV, COL_FQ, COL_GATES = 0, 256, 512, 896, 1280
COL_PV, COL_FK, COL_FV, COL_AUX, W_IN_COLS = 2304, 2560, 2944, 3328, 3456
GATES_W = HEADS_W + POOL_W + HEADS_W
AUX_FF, AUX_LR = 0, 8
SPLIT_SIZES = (QK_W, QK_W, HEADS_W, GLA_LR, HEADS_W, POOL_W, POOL_W, HEADS_W, HEADS_W, HEADS_W, N_HEADS, HEADS_W)

AUXK_ONES = (18, 36)


def _cparams(sem, vmem_mb):
    return pltpu.CompilerParams(dimension_semantics=sem, vmem_limit_bytes=vmem_mb * 1024 * 1024)


def _const_spec(shape):
    return pl.BlockSpec(shape, lambda *_: (0,) * len(shape))


def _dot(a, b):
    return jnp.dot(a, b, preferred_element_type=F32)


def _dot_nt(a, b):
    return lax.dot_general(a, b, (((1,), (1,)), ((), ())), preferred_element_type=F32)


def _dot_tn(a, b):
    return lax.dot_general(a, b, (((0,), (0,)), ((), ())), preferred_element_type=F32)


def _log_sigmoid(x):
    return jnp.minimum(x, 0.0) - jnp.log1p(jnp.exp(-jnp.abs(x)))


def _silu(x):
    return x * (1.0 / (1.0 + jnp.exp(-x)))


def _split3(x):
    hi = x.astype(BF16)
    r = x - hi.astype(F32)
    mid = r.astype(BF16)
    lo = (r - mid.astype(F32)).astype(BF16)
    return hi, mid, lo


def _cumsum_rows(x):
    n = x.shape[0]
    row = lax.broadcasted_iota(jnp.int32, x.shape, 0)
    k = 1
    while k < n:
        x = x + jnp.where(row >= k, pltpu.roll(x, k, axis=0), 0.0)
        k *= 2
    return x


def _inproj_kernel(x_ref, pre_ref, w_ref, wup_ref, bg_ref, bf_ref,
                   q_ref, k_ref, g_ref, v_ref, gates_ref, pv_ref, fq_ref, fk_ref, fv_ref, aux_ref):
    x = x_ref[0]
    xn = x * lax.rsqrt(jnp.mean(x * x, axis=-1, keepdims=True) + EPS) * pre_ref[...]
    xb = xn.astype(BF16)

    def seg(c0, width):
        return _dot(xb, w_ref[:, c0:c0 + width])

    q_ref[0] = seg(COL_Q, QK_W) * (GLA_DK ** -0.5)
    k_ref[0] = seg(COL_K, QK_W)
    vq = seg(COL_GV, 2 * HEADS_W)
    v_ref[0] = vq[:, :HEADS_W].astype(BF16)
    fq_ref[0] = (vq[:, HEADS_W:] * (HEAD_DIM ** -0.5)).astype(BF16)
    gates_ref[0] = seg(COL_GATES, GATES_W)
    pv_ref[0] = seg(COL_PV, POOL_W)
    aux = seg(COL_AUX, LANES)
    g_ref[0] = _log_sigmoid(_dot(aux.astype(BF16), wup_ref[...]) + bg_ref[...]) * (1.0 / GLA_GATE_NORM)
    kv = seg(COL_FK, 2 * HEADS_W)
    fk_ref[0] = kv[:, :HEADS_W]
    fv_ref[0] = kv[:, HEADS_W:]
    aux_ref[0] = _log_sigmoid(aux + bf_ref[...])


def _inproj(h, lw, tm):
    B, T, _ = h.shape
    tile = lambda w: pl.BlockSpec((1, tm, w), lambda b, t: (b, t, 0))
    in_specs = [tile(D_MODEL), _const_spec((1, D_MODEL)), _const_spec((D_MODEL, W_IN_COLS)),
                _const_spec((LANES, QK_W)), _const_spec((1, QK_W)), _const_spec((1, LANES))]
    widths = (("q", QK_W, F32), ("k", QK_W, F32), ("g", QK_W, F32), ("v", HEADS_W, BF16), ("gates", GATES_W, F32),
              ("pv", POOL_W, F32), ("fq", HEADS_W, BF16), ("fk", HEADS_W, F32), ("fv", HEADS_W, F32), ("aux", LANES, F32))
    outs = pl.pallas_call(
        _inproj_kernel, grid=(B, T // tm), in_specs=in_specs,
        out_specs=[tile(w) for _, w, _ in widths],
        out_shape=[jax.ShapeDtypeStruct((B, T, w), dt) for _, w, dt in widths],
        compiler_params=_cparams(("parallel", "parallel"), 56), name="inproj")(
            h, lw["pre"], lw["w_in"], lw["w_up"], lw["b_gate"], lw["bf"])
    return dict(zip([n for n, _, _ in widths], outs))


def _gla_masks(chunk):
    n = N_HEADS * chunk
    bd_k = (lax.broadcasted_iota(jnp.int32, (n, QK_W), 0) // chunk) == (lax.broadcasted_iota(jnp.int32, (n, QK_W), 1) // GLA_DK)
    bd_v = (lax.broadcasted_iota(jnp.int32, (n, HEADS_W), 0) // chunk) == (lax.broadcasted_iota(jnp.int32, (n, HEADS_W), 1) // HEAD_DIM)
    tril = (lax.broadcasted_iota(jnp.int32, (chunk, n), 1) % chunk) <= lax.broadcasted_iota(jnp.int32, (chunk, n), 0)
    return bd_k, bd_v, tril


def _gla_intra(q, k, g, v_bf, masks, chunk):
    bd_k, bd_v, tril = masks
    bcum = _cumsum_rows(g)
    qe_bf = (q * jnp.exp(bcum)).astype(BF16)
    ke = k * jnp.exp(-bcum)
    b_end = bcum[chunk - 1:chunk, :]
    kd_bf = (k * jnp.exp(b_end - bcum)).astype(BF16)
    if chunk % 16 == 0:
        ke_rows = jnp.where(bd_k, jnp.concatenate([ke.astype(BF16)] * N_HEADS, axis=0), jnp.zeros((), BF16))
        v_rows = jnp.where(bd_v, jnp.concatenate([v_bf] * N_HEADS, axis=0), jnp.zeros((), BF16))
    else:
        ke_rows = jnp.where(bd_k, jnp.concatenate([ke] * N_HEADS, axis=0), 0.0).astype(BF16)
        v_rows = jnp.where(bd_v, jnp.concatenate([v_bf.astype(F32)] * N_HEADS, axis=0), 0.0).astype(BF16)
    a = jnp.where(tril, _dot_nt(qe_bf, ke_rows), 0.0)
    o_intra = _dot(a.astype(BF16), v_rows)
    return o_intra, qe_bf, kd_bf, jnp.exp(b_end)


def _head_norm(o, gmat_bf, gn):
    o2 = o * o
    hi = o2.astype(BF16)
    lo = (o2 - hi.astype(F32)).astype(BF16)
    ms = _dot(hi, gmat_bf) + _dot(lo, gmat_bf)
    return o * lax.rsqrt(ms + EPS) * gn


def _pool_delta(hist, pv, pos0, tm):
    assert POOL_WINDOWS == (2, 4, 8, 16)
    ext = jnp.concatenate([hist, pv], axis=0)
    lane_g = lax.broadcasted_iota(jnp.int32, (1, POOL_W), 1) // POOL_GROUP
    pos = pos0 + lax.broadcasted_iota(jnp.int32, (tm, 1), 0)
    run = ext
    sums = jnp.zeros((tm, POOL_W), F32)
    cnt = jnp.zeros((tm, POOL_W), F32)
    for gi, w in enumerate(POOL_WINDOWS):
        run = run + pltpu.roll(run, w // 2, axis=0)
        sums = jnp.where(lane_g == gi, run[HIST_ROWS:, :], sums)
        cnt = jnp.where(lane_g == gi, jnp.minimum(pos + 1, w).astype(F32), cnt)
    return sums / cnt - pv, ext[tm:tm + HIST_ROWS, :]


def _seqmix_kernel(*refs, T, tm, chunk, n_pre, aliased):
    if n_pre:
        (x_ref, pre_ref, w_ref, wup_ref, bg_ref, bf_ref, kpre_ref, vpre_ref, apre_ref, s0_ref, hist0_ref, gn_ref, gmat_ref,
         pw_ref, ps_ref) = refs[:15]
        refs = refs[15:]
    else:
        (x_ref, pre_ref, w_ref, wup_ref, bg_ref, bf_ref, s0_ref, hist0_ref, gn_ref, gmat_ref, pw_ref, ps_ref) = refs[:12]
        refs = refs[12:]
    if aliased:
        refs = refs[2:]
    (gates_ref, fq_ref, fk_ref, fv_ref, kt_ref, vt_ref, aux_ref, og_ref, op_ref, sout_ref, hout_ref,
     st_ref, ext_ref, kext_ref, vext_ref) = refs
    t = pl.program_id(1)

    @pl.when(t == 0)
    def _():
        st_ref[...] = s0_ref[...]
        ext_ref[...] = hist0_ref[...]
        if n_pre:
            fk_ref[0, 0:n_pre, :] = kpre_ref[...].astype(BF16)
            fv_ref[0, 0:n_pre, :] = vpre_ref[...].astype(BF16)
            aux_ref[0, 0:n_pre, :] = apre_ref[...]
            kext_ref[0:n_pre, :] = kpre_ref[...]
            vext_ref[0:n_pre, :] = vpre_ref[...]

    def transposed(x):
        r = x.shape[0]
        if r % LANES == 0:
            return x.T
        return jnp.concatenate([x, jnp.zeros((LANES - r, HEADS_W), F32)], axis=0).T[:, :r]

    def emit_transposed(out_ref, carry_ref, tile_rows):
        if n_pre:
            carry_ref[n_pre:n_pre + tm, :] = tile_rows
            tile_rows = carry_ref[0:tm, :]
        lanes = slice(0, tm) if T == tm else pl.ds(pl.multiple_of(t * tm, LANES), tm)
        out_ref[0, 0, :, lanes] = transposed(tile_rows)
        if n_pre:
            tail = carry_ref[tm:tm + n_pre, :]
            carry_ref[0:n_pre, :] = tail

            @pl.when(t == pl.num_programs(1) - 1)
            def _():
                out_ref[0, 0, :, T:T + n_pre] = transposed(tail)

    x = x_ref[0]
    xn = x * lax.rsqrt(jnp.mean(x * x, axis=-1, keepdims=True) + EPS) * pre_ref[...]
    xb = xn.astype(BF16)

    def seg(c0, width):
        return _dot(xb, w_ref[:, c0:c0 + width])

    gates_ref[0] = seg(COL_GATES, GATES_W)
    vq = seg(COL_GV, 2 * HEADS_W)
    fq_ref[0] = (vq[:, HEADS_W:] * (HEAD_DIM ** -0.5)).astype(BF16)
    aux = seg(COL_AUX, LANES)
    rows = pl.ds(pl.multiple_of(n_pre + t * tm, 8), tm)
    kv = seg(COL_FK, 2 * HEADS_W)
    fk, fv = kv[:, :HEADS_W], kv[:, HEADS_W:]
    fk_ref[0, rows, :] = fk.astype(BF16)
    fv_ref[0, rows, :] = fv.astype(BF16)
    emit_transposed(kt_ref, kext_ref, fk)
    emit_transposed(vt_ref, vext_ref, fv)
    aux_ref[0, rows, :] = _log_sigmoid(aux + bf_ref[...])
    q = seg(COL_Q, QK_W) * (GLA_DK ** -0.5)
    k = seg(COL_K, QK_W)
    v_bf = vq[:, :HEADS_W].astype(BF16)
    g = _log_sigmoid(_dot(aux.astype(BF16), wup_ref[...]) + bg_ref[...]) * (1.0 / GLA_GATE_NORM)

    masks = _gla_masks(chunk)
    bd = (lax.broadcasted_iota(jnp.int32, (HEADS_W, QK_W), 0) // HEAD_DIM) == (lax.broadcasted_iota(jnp.int32, (HEADS_W, QK_W), 1) // GLA_DK)
    st = st_ref[...]
    outs = []
    for c in range(tm // chunk):
        r = slice(c * chunk, (c + 1) * chunk)
        o_intra, qe_bf, kd_bf, decay = _gla_intra(q[r], k[r], g[r], v_bf[r], masks, chunk)
        outs.append(o_intra + _dot_nt(qe_bf, st.astype(BF16)))
        st = st * decay + jnp.where(bd, _dot_tn(v_bf[r], kd_bf), 0.0)
    st_ref[...] = st
    o = outs[0] if len(outs) == 1 else jnp.concatenate(outs, axis=0)
    og_ref[0] = _head_norm(o, gmat_ref[...], gn_ref[...])
    d, ext_ref[...] = _pool_delta(ext_ref[...], seg(COL_PV, POOL_W), n_pre + t * tm, tm)
    op_ref[0] = _dot(d.astype(BF16), pw_ref[...]) * ps_ref[...]

    @pl.when(t == pl.num_programs(1) - 1)
    def _():
        sout_ref[0] = st
        hout_ref[0] = ext_ref[...]


def _seqmix(h, lw, consts, prefix, tm, chunk, layer=0, depth=1, kv_t=None):
    B, T, _ = h.shape
    n_pre = 0 if prefix is None else prefix["k"].shape[0]
    L = n_pre + T
    tile = lambda w: pl.BlockSpec((1, tm, w), lambda b, t: (b, t, 0))
    whole = lambda w: pl.BlockSpec((1, L, w), lambda b, t: (b, 0, 0))
    per_seq = lambda r, w: pl.BlockSpec((1, r, w), lambda b, t: (b, 0, 0))
    in_specs = [tile(D_MODEL), _const_spec((1, D_MODEL)), _const_spec((D_MODEL, W_IN_COLS)),
                _const_spec((LANES, QK_W)), _const_spec((1, QK_W)), _const_spec((1, LANES))]
    args = [h, lw["pre"], lw["w_in"], lw["w_up"], lw["b_gate"], lw["bf"]]
    if n_pre:
        in_specs += [_const_spec((n_pre, HEADS_W)), _const_spec((n_pre, HEADS_W)), _const_spec((n_pre, LANES))]
        args += [prefix["k"], prefix["v"], prefix["aux"]]
    in_specs += [_const_spec((HEADS_W, QK_W)), _const_spec((HIST_ROWS, POOL_W)), _const_spec((1, HEADS_W)),
                 _const_spec((HEADS_W, HEADS_W)), _const_spec((POOL_W, POOL_W)), _const_spec((1, POOL_W))]
    s0 = jnp.zeros((HEADS_W, QK_W), F32) if prefix is None else prefix["state"]
    hist0 = jnp.zeros((HIST_ROWS, POOL_W), F32) if prefix is None else prefix["hist"]
    args += [s0, hist0, lw["gla_norm"], consts["gmat"], lw["pool_w"], lw["pool_scale"]]
    aliases = {}
    if kv_t is not None:
        aliases = {len(args): 4, len(args) + 1: 5}
        in_specs += [pl.BlockSpec(memory_space=pl.ANY)] * 2
        args += list(kv_t)
    slab = pl.BlockSpec((1, 1, HEADS_W, L), lambda b, t: (layer, b, 0, 0))
    out_shape = [jax.ShapeDtypeStruct((B, T, GATES_W), F32), jax.ShapeDtypeStruct((B, T, HEADS_W), BF16),
                 jax.ShapeDtypeStruct((B, L, HEADS_W), BF16), jax.ShapeDtypeStruct((B, L, HEADS_W), BF16),
                 jax.ShapeDtypeStruct((depth, B, HEADS_W, L), F32), jax.ShapeDtypeStruct((depth, B, HEADS_W, L), F32),
                 jax.ShapeDtypeStruct((B, L, LANES), F32), jax.ShapeDtypeStruct((B, T, HEADS_W), F32),
                 jax.ShapeDtypeStruct((B, T, POOL_W), F32), jax.ShapeDtypeStruct((B, HEADS_W, QK_W), F32),
                 jax.ShapeDtypeStruct((B, HIST_ROWS, POOL_W), F32)]
    out_specs = [tile(GATES_W), tile(HEADS_W), whole(HEADS_W), whole(HEADS_W), slab, slab, whole(LANES), tile(HEADS_W),
                 tile(POOL_W), per_seq(HEADS_W, QK_W), per_seq(HIST_ROWS, POOL_W)]
    names = ("gates", "fq", "fk", "fv", "kt", "vt", "aux", "og", "op", "state", "hist")
    outs = pl.pallas_call(
        functools.partial(_seqmix_kernel, T=T, tm=tm, chunk=chunk, n_pre=n_pre, aliased=kv_t is not None),
        grid=(B, T // tm), in_specs=in_specs, out_specs=out_specs, out_shape=out_shape,
        input_output_aliases=aliases,
        scratch_shapes=[pltpu.VMEM((HEADS_W, QK_W), F32), pltpu.VMEM((HIST_ROWS, POOL_W), F32),
                        pltpu.VMEM((tm + n_pre, HEADS_W), F32), pltpu.VMEM((tm + n_pre, HEADS_W), F32)],
        compiler_params=_cparams(("parallel", "arbitrary"), 56), name="seqmix")(*args)
    return dict(zip(names, outs))


def _cumsum_blocks(lf, carry, tri_bf, cb):
    nblk = lf.shape[0] // cb
    x = lf if nblk == 1 else jnp.concatenate([lf[i * cb:(i + 1) * cb, :] for i in range(nblk)], axis=1)
    tri = tri_bf[0:cb, 0:cb]
    hi, mid, lo = _split3(x)
    c = _dot(tri, hi) + _dot(tri, mid) + _dot(tri, lo)
    out = []
    for i in range(nblk):
        blk = c[:, i * LANES:(i + 1) * LANES] + carry
        carry = blk[cb - 1:cb, :]
        out.append(blk)
    return (out[0] if nblk == 1 else jnp.concatenate(out, axis=0)), carry


def _fox_features(f, place_ref):
    lane = lax.broadcasted_iota(jnp.int32, (1, LANES), 1)
    fh, fm, fl = _split3(f)
    ones_k = jnp.where((lane >= AUXK_ONES[0]) & (lane < AUXK_ONES[1]), 1.0, 0.0)
    ones_q = jnp.where(lane < AUXK_ONES[0], 1.0, 0.0)
    kaux = ones_k - (_dot(fh, place_ref[0]) + _dot(fm, place_ref[1]) + _dot(fl, place_ref[2]))
    qaux = ones_q + (_dot(fh, place_ref[3]) + _dot(fm, place_ref[4]) + _dot(fl, place_ref[5]))
    return kaux.astype(BF16), qaux.astype(BF16)


def _fox_kernel(fq_ref, k_ref, v_ref, aux_ref, tri_ref, place_ref, h_ref, og_ref, op_ref, gates_ref, wo_ref, post_ref,
                out_ref, kaux_ref, qaux_ref, q2_ref, m_ref, acc_ref, sa_ref, sb_ref, *, T, n_pre, tq, tk):
    t = pl.program_id(1)
    r = tq // tk
    assert r * tk == tq and (r % 2 == 0 or T == tq)

    @pl.when(t == 0)
    def _():
        carry = jnp.zeros((1, LANES), F32)
        for r0, n in ([(0, n_pre)] if n_pre else []) + [(n_pre, T)]:
            f, carry = _cumsum_blocks(aux_ref[0, r0:r0 + n, :], carry, tri_ref[...], min(PAGE, n))
            kaux_ref[r0:r0 + n, :], qaux_ref[r0:r0 + n, :] = _fox_features(f, place_ref)

    lane = lax.broadcasted_iota(jnp.int32, (1, LANES), 1)
    q0 = pl.multiple_of(n_pre + t * tq, 16)
    qa = qaux_ref[pl.ds(q0, tq), :]
    fq = fq_ref[0]
    zero_bf = jnp.zeros((tq, LANES), BF16)
    for h in range(N_HEADS):
        fq_p = fq[:, (h // 2) * LANES:(h // 2 + 1) * LANES]
        qmask = (lane == h) | (lane == 6 + h) | (lane == 12 + h) | (lane == 18 + h) | (lane == 24 + h) | (lane == 30 + h)
        q2_ref[h] = jnp.concatenate([jnp.where((lane // HEAD_DIM) == (h % 2), fq_p, zero_bf), jnp.where(qmask, qa, zero_bf)], axis=1)
    m_ref[...] = jnp.full(m_ref.shape, NEG, F32)
    acc_ref[...] = jnp.zeros(acc_ref.shape, F32)

    def logits(j, n, s_out, row0=0):
        r0 = 0 if j is None else pl.multiple_of(n_pre + j * tk, 16)
        kaux = kaux_ref[pl.ds(r0, n), :]
        for p in range(N_HEADS // 2):
            k2 = jnp.concatenate([k_ref[0, pl.ds(r0, n), p * LANES:(p + 1) * LANES], kaux], axis=1)
            for h in (2 * p, 2 * p + 1):
                if row0:
                    s_out[h, row0:tq, :] = _dot_nt(q2_ref[h, row0:tq, :], k2)
                else:
                    s_out[h] = _dot_nt(q2_ref[h], k2)

    def update(j, n, s_in, diag):
        r0 = 0 if j is None else pl.multiple_of(n_pre + j * tk, 16)
        ones = jnp.ones((n, LANES), BF16)
        causal = diag is not None
        rows = slice((diag or 0) * tk, tq)
        nr = tq - rows.start
        if causal:
            keep = lax.broadcasted_iota(jnp.int32, (nr, n), 0) >= lax.broadcasted_iota(jnp.int32, (nr, n), 1)
        for p in range(N_HEADS // 2):
            vp = v_ref[0, pl.ds(r0, n), p * LANES:(p + 1) * LANES]
            for h in (2 * p, 2 * p + 1):
                v2 = jnp.where((lane // HEAD_DIM) == (h % 2), vp, ones)
                def read(z):
                    s = s_in[h] if j is None else s_in[h + jnp.minimum(z, 0), rows, :]
                    return jnp.where(keep, s, NEG) if causal else s
                m_old = m_ref[h, rows, :]
                m_new = jnp.maximum(m_old, jnp.max(read(t), axis=-1, keepdims=True))
                alpha = jnp.exp(m_old - m_new)
                m_b = m_new[:, :n] if n < LANES else jnp.concatenate([m_new] * (n // LANES), axis=1)
                pe = jnp.exp(read(pl.program_id(0)) - m_b).astype(BF16)
                acc_ref[h, rows, :] = alpha * acc_ref[h, rows, :] + _dot(pe, v2)
                m_ref[h, rows, :] = m_new

    if n_pre:
        pre = {}
        logits(None, n_pre, pre)
        update(None, n_pre, pre, None)

    logits(0, tk, sa_ref)

    def body(jj, c):
        j = 2 * jj
        logits(j + 1, tk, sb_ref)
        update(j, tk, sa_ref, None)
        logits(j + 2, tk, sa_ref)
        update(j + 1, tk, sb_ref, None)
        return c

    lax.fori_loop(0, (r * t) // 2, body, 0)
    bufs = (sa_ref, sb_ref)
    for i in range(r):
        if i + 1 < r:
            logits(r * t + i + 1, tk, bufs[(i + 1) % 2], row0=(i + 1) * tk)
        update(r * t + i, tk, bufs[i % 2], i)

    o_fox = []
    for p in range(N_HEADS // 2):
        a0, a1 = acc_ref[2 * p], acc_ref[2 * p + 1]
        o_fox.append(jnp.where((lane // HEAD_DIM) == 0, a0 / pltpu.roll(a0, HEAD_DIM, axis=1),
                               a1 / pltpu.roll(a1, HEAD_DIM, axis=1)))
    y_in = jnp.concatenate([og_ref[0], op_ref[0]] + o_fox, axis=1) * _silu(gates_ref[0])
    y = _dot(y_in.astype(BF16), wo_ref[...])
    out_ref[0] = h_ref[0] + y * lax.rsqrt(jnp.mean(y * y, axis=-1, keepdims=True) + EPS) * post_ref[...]


def _fox_out(x, h, og, op, lw, consts, tq, n_pre):
    B, T, _ = x["fq"].shape
    L = n_pre + T
    whole = lambda w: pl.BlockSpec((1, L, w), lambda b, t: (b, 0, 0))
    tile = lambda w: pl.BlockSpec((1, tq, w), lambda b, t: (b, t, 0))
    in_specs = [tile(HEADS_W), whole(HEADS_W), whole(HEADS_W), whole(LANES),
                _const_spec((PAGE, PAGE)), _const_spec((6, LANES, LANES)),
                tile(D_MODEL), tile(HEADS_W), tile(POOL_W), tile(GATES_W), _const_spec((GATES_W, D_MODEL)), _const_spec((1, D_MODEL))]
    tk = min(tq, 256)
    return pl.pallas_call(
        functools.partial(_fox_kernel, T=T, n_pre=n_pre, tq=tq, tk=tk),
        grid=(B, T // tq), in_specs=in_specs,
        out_specs=tile(D_MODEL),
        out_shape=jax.ShapeDtypeStruct((B, T, D_MODEL), F32),
        scratch_shapes=[pltpu.VMEM((L, LANES), BF16), pltpu.VMEM((L, LANES), BF16),
                        pltpu.VMEM((N_HEADS, tq, 2 * LANES), BF16), pltpu.VMEM((N_HEADS, tq, LANES), F32),
                        pltpu.VMEM((N_HEADS, tq, LANES), F32),
                        pltpu.VMEM((N_HEADS, tq, tk), F32), pltpu.VMEM((N_HEADS, tq, tk), F32)],
        compiler_params=_cparams(("parallel", "arbitrary"), 56), name="fox")(
            x["fq"], x["fk"], x["fv"], x["aux"], consts["tri"], consts["place"],
            h, og, op, x["gates"], lw["w_out"], lw["post"])


def _outproj_kernel(h_ref, og_ref, op_ref, of_ref, gates_ref, wo_ref, post_ref, out_ref):
    y_in = jnp.concatenate([og_ref[0], op_ref[0], of_ref[0]], axis=1) * _silu(gates_ref[0])
    y = _dot(y_in.astype(BF16), wo_ref[...])
    out_ref[0] = h_ref[0] + y * lax.rsqrt(jnp.mean(y * y, axis=-1, keepdims=True) + EPS) * post_ref[...]


def _outproj(h, og, op, of, gates, lw, tm):
    B, T, _ = h.shape
    tile = lambda w: pl.BlockSpec((1, tm, w), lambda b, t: (b, t, 0))
    return pl.pallas_call(
        _outproj_kernel, grid=(B, T // tm),
        in_specs=[tile(D_MODEL), tile(HEADS_W), tile(POOL_W), tile(HEADS_W), tile(GATES_W),
                  _const_spec((GATES_W, D_MODEL)), _const_spec((1, D_MODEL))],
        out_specs=tile(D_MODEL), out_shape=jax.ShapeDtypeStruct((B, T, D_MODEL), F32),
        compiler_params=_cparams(("parallel", "parallel"), 48), name="outproj")(
            h, og, op, of, gates, lw["w_out"], lw["post"])


def _sample_mix_kernel(q_ref, k_ref, g_ref, v_ref, pv_ref, s_ref, hist_ref, gn_ref, gmat_ref, pw_ref, ps_ref,
                       og_ref, op_ref, sout_ref, *, nseq, ts, pos0):
    masks = _gla_masks(ts)
    rr = lax.broadcasted_iota(jnp.int32, (QK_W, HEADS_W), 0) // GLA_DK
    cc = lax.broadcasted_iota(jnp.int32, (QK_W, HEADS_W), 1) // HEAD_DIM
    bd = rr == cc
    ones = jnp.ones((3 * ts, LANES), BF16)
    outs, deltas = [], []
    for i in range(nseq):
        r = slice(i * ts, (i + 1) * ts)
        g = g_ref[r, :]
        v_bf = v_ref[r, :].astype(BF16)
        o_intra, qe_bf, kd_bf, _ = _gla_intra(q_ref[r, :], k_ref[r, :], g, v_bf, masks, ts)
        s1 = s_ref[i]
        s2 = jnp.concatenate([s1, s1], axis=1)
        s_bd = jnp.where(bd, jnp.concatenate([s2, s2, s2], axis=1), 0.0).astype(BF16)
        outs.append(o_intra + _dot(qe_bf, s_bd))
        u = jnp.where(bd, _dot_tn(kd_bf, v_bf), 0.0)
        x = jnp.concatenate([u[64 * j:64 * (j + 1), LANES * j:LANES * (j + 1)] for j in range(N_HEADS // 2)], axis=0)
        x = x + pltpu.roll(x, HEAD_DIM, axis=1)
        gh, gm, gl = _split3(g)
        gparts = jnp.concatenate([gh.astype(F32), gm.astype(F32), gl.astype(F32)], axis=0).astype(BF16)
        sout_ref[i] = (s2 * jnp.exp(_dot_tn(gparts, ones)) + x)[:, :HEAD_DIM]
        deltas.append(_pool_delta(hist_ref[i], pv_ref[r, :], pos0, ts)[0])
    og_ref[...] = _head_norm(jnp.concatenate(outs, axis=0), gmat_ref[...], gn_ref[...])
    op_ref[...] = _dot(jnp.concatenate(deltas, axis=0).astype(BF16), pw_ref[...]) * ps_ref[...]


def _sample_mix(x, state2, hist16, lw, consts, nseq, ts, pos0):
    n = x["q"].shape[0]
    rows = nseq * ts
    tile = lambda w: pl.BlockSpec((rows, w), lambda i: (i, 0))
    in_specs = [tile(QK_W), tile(QK_W), tile(QK_W), tile(HEADS_W), tile(POOL_W),
                pl.BlockSpec((nseq, QK_W, HEAD_DIM), lambda i: (i, 0, 0)), pl.BlockSpec((nseq, HIST_ROWS, POOL_W), lambda i: (i, 0, 0)),
                _const_spec((1, HEADS_W)), _const_spec((HEADS_W, HEADS_W)), _const_spec((POOL_W, POOL_W)), _const_spec((1, POOL_W))]
    out_shape = [jax.ShapeDtypeStruct((n, HEADS_W), F32), jax.ShapeDtypeStruct((n, POOL_W), F32),
                 jax.ShapeDtypeStruct(state2.shape, F32)]
    out_specs = [tile(HEADS_W), tile(POOL_W), pl.BlockSpec((nseq, QK_W, HEAD_DIM), lambda i: (i, 0, 0))]
    return pl.pallas_call(
        functools.partial(_sample_mix_kernel, nseq=nseq, ts=ts, pos0=pos0),
        grid=(n // rows,), in_specs=in_specs, out_specs=out_specs, out_shape=out_shape,
        compiler_params=_cparams(("parallel",), 48), name="sample_mix")(
            x["q"], x["k"], x["g"], x["v"], x["pv"], state2, hist16, lw["gla_norm"], consts["gmat"], lw["pool_w"], lw["pool_scale"])


def _suffix_rows(x, period):
    n = x.shape[0]
    row = lax.broadcasted_iota(jnp.int32, x.shape, 0) % period
    k = 1
    while k < period:
        x = x + jnp.where(row + k < period, pltpu.roll(x, n - k, axis=0), 0.0)
        k *= 2
    return x


def _lfsum_kernel(lf_ref, sw_ref, o_ref):
    n = lf_ref.shape[0]
    hi, mid, lo = _split3(lf_ref[...])
    parts = jnp.concatenate([hi.astype(F32), mid.astype(F32), lo.astype(F32)], axis=0).astype(BF16)
    r3 = _dot(parts, sw_ref[...])
    o_ref[...] = r3[0:n] + r3[n:2 * n] + r3[2 * n:3 * n]


def _lfsum(lf2, sw, tr):
    n = lf2.shape[0]
    return pl.pallas_call(
        _lfsum_kernel, grid=(n // tr,),
        in_specs=[pl.BlockSpec((tr, PAGE), lambda i: (i, 0)), _const_spec((PAGE, 2 * PAGE))],
        out_specs=pl.BlockSpec((tr, 2 * PAGE), lambda i: (i, 0)), out_shape=jax.ShapeDtypeStruct((n, 2 * PAGE), F32),
        compiler_params=_cparams(("parallel",), 32), name="lfsum")(lf2, sw)


def _sample_fox_kernel(pt_ref, fq_ref, kn_ref, vn_ref, an_ref, rt_ref, ck_ref, cv_ref, o_ref,
                       kbuf, vbuf, ksem, vsem, *, n_pages, ts, n_phys, layer):
    b = pl.program_id(0)
    nb = pl.num_programs(0)
    slot = b % 2
    nxt = jnp.minimum(b + 1, nb - 1)

    def k_copy(seq, s, p):
        return pltpu.make_async_copy(ck_ref.at[layer, pt_ref[seq, p]], kbuf.at[s, p], ksem.at[s])

    def v_copy(seq, s, p):
        return pltpu.make_async_copy(cv_ref.at[layer, pt_ref[seq, p]], vbuf.at[s, p], vsem.at[s])

    @pl.when(b == 0)
    def _():
        for p in range(n_pages):
            k_copy(0, 0, p).start()
            v_copy(0, 0, p).start()

    for p in range(n_pages):
        k_copy(b, slot, p).wait()
    r2 = jnp.concatenate([rt_ref[pl.ds(h * n_phys + pt_ref[b, p], 1), :]
                          for h in range(N_HEADS) for p in range(n_pages)], axis=0)
    tot = r2[:, PAGE:]
    rfull = r2[:, :PAGE] + (_suffix_rows(tot, n_pages) - tot)
    cn = _cumsum_rows(an_ref[0])
    cn2 = jnp.concatenate([cn, cn], axis=0)
    cparts = _split3(cn)
    lane = lax.broadcasted_iota(jnp.int32, (2 * ts, LANES), 1)
    first = lax.broadcasted_iota(jnp.int32, (2 * ts, LANES), 0) < ts
    own = (lane < HEAD_DIM) == first
    causal = (lax.broadcasted_iota(jnp.int32, (2 * ts, ts), 0) % ts) >= lax.broadcasted_iota(jnp.int32, (2 * ts, ts), 1)
    for pr in range(N_HEADS // 2):
        cols = slice(pr * LANES, (pr + 1) * LANES)
        fq2 = jnp.concatenate([fq_ref[0, :, cols], fq_ref[0, :, cols]], axis=0)
        q2 = jnp.where(own, fq2, 0.0).astype(BF16)
        onehot = lane == jnp.where(first, 2 * pr, 2 * pr + 1)
        cq = jnp.sum(jnp.where(onehot, cn2, 0.0), axis=-1, keepdims=True)
        sel = jnp.where(onehot, 1.0, 0.0).astype(BF16)
        cn_t = _dot_nt(sel, cparts[0]) + _dot_nt(sel, cparts[1]) + _dot_nt(sel, cparts[2])
        s_new = jnp.where(causal, _dot_nt(q2, kn_ref[0, :, cols].astype(BF16)) + cq - cn_t, NEG)
        s_pages = []
        for p in range(n_pages):
            if pr == 0:
                k_copy(nxt, 1 - slot, p).start()
            kt = kbuf[slot, p, 2 * pr:2 * pr + 2].reshape(LANES, PAGE).astype(BF16)
            r0, r1 = 2 * pr * n_pages + p, (2 * pr + 1) * n_pages + p
            bias = jnp.concatenate([jnp.broadcast_to(rfull[r0:r0 + 1, :], (ts, PAGE)),
                                    jnp.broadcast_to(rfull[r1:r1 + 1, :], (ts, PAGE))], axis=0)
            s_pages.append(_dot(q2, kt) + bias + cq)
        mm = s_pages[0]
        for s in s_pages[1:]:
            mm = jnp.maximum(mm, s)
        m = jnp.maximum(jnp.max(mm, axis=-1, keepdims=True), jnp.max(s_new, axis=-1, keepdims=True))
        p_new = jnp.exp(s_new - m)
        l = jnp.sum(p_new, axis=-1, keepdims=True)
        acc = _dot(p_new.astype(BF16), vn_ref[0, :, cols].astype(BF16))
        if pr == 0:
            for p in range(n_pages):
                v_copy(b, slot, p).wait()
        for p in range(n_pages):
            if pr == 0:
                v_copy(nxt, 1 - slot, p).start()
            pe = jnp.exp(s_pages[p] - m)
            l = l + jnp.sum(pe, axis=-1, keepdims=True)
            vt = vbuf[slot, p, 2 * pr:2 * pr + 2].reshape(LANES, PAGE).astype(BF16)
            acc = acc + _dot_nt(pe.astype(BF16), vt)
        o2 = acc / l
        o_ref[0, :, cols] = jnp.where(lane[:ts] < HEAD_DIM, o2[:ts], o2[ts:])

    @pl.when(b == nb - 1)
    def _():
        for p in range(n_pages):
            k_copy(nxt, 1 - slot, p).wait()
            v_copy(nxt, 1 - slot, p).wait()


def _sample_fox(layer, page_table, fq, kn, vn, an, rt2, cache_kt, cache_vt):
    nb, ts, _ = fq.shape
    n_pages = page_table.shape[1]
    n_phys = cache_kt.shape[1]
    seq = lambda w: pl.BlockSpec((1, ts, w), lambda b, pt: (b, 0, 0))
    rt_spec = pl.BlockSpec((N_HEADS * n_phys, 2 * PAGE), lambda b, pt: (layer, 0), pipeline_mode=pl.Buffered(1))
    hbm = pl.BlockSpec(memory_space=pl.ANY)
    page_buf = pltpu.VMEM((2, n_pages, N_HEADS, HEAD_DIM, PAGE), F32)
    grid_spec = pltpu.PrefetchScalarGridSpec(
        num_scalar_prefetch=1, grid=(nb,),
        in_specs=[seq(HEADS_W), seq(HEADS_W), seq(HEADS_W), seq(LANES), rt_spec, hbm, hbm],
        out_specs=pl.BlockSpec((1, ts, HEADS_W), lambda b, pt: (b, 0, 0)),
        scratch_shapes=[page_buf, page_buf, pltpu.SemaphoreType.DMA((2,)), pltpu.SemaphoreType.DMA((2,))])
    return pl.pallas_call(
        functools.partial(_sample_fox_kernel, n_pages=n_pages, ts=ts, n_phys=n_phys, layer=layer), grid_spec=grid_spec,
        out_shape=jax.ShapeDtypeStruct((nb, ts, HEADS_W), F32),
        compiler_params=_cparams(("arbitrary",), 56), name="sample_fox")(
            page_table, fq, kn, vn, an, rt2, cache_kt, cache_vt)


def _wprep_kernel(w_ref, o_ref, *, depth, n_cols):
    per_col = (D_MODEL // LANES) * depth
    valid = pl.program_id(0) * LANES + lax.broadcasted_iota(jnp.int32, (LANES, LANES), 0) < n_cols
    for l in range(depth):
        for rt in range(D_MODEL // LANES):
            blk = w_ref[pl.ds(rt * depth + l, LANES, stride=per_col), :]
            o_ref[l, rt * LANES:(rt + 1) * LANES, :] = jnp.where(valid, blk, 0.0).T.astype(BF16)


def _wprep(w_in):
    depth, d, c = w_in.shape
    per_col = (d // LANES) * depth
    cp = -(-c // LANES) * LANES
    w2 = jnp.transpose(w_in, (2, 0, 1)).reshape(c, depth, d // LANES, LANES)
    w2 = jnp.transpose(w2, (0, 2, 1, 3)).reshape(c * per_col, LANES)
    return pl.pallas_call(
        functools.partial(_wprep_kernel, depth=depth, n_cols=c), grid=(cp // LANES,),
        in_specs=[pl.BlockSpec((LANES * per_col, LANES), lambda j: (j, 0))],
        out_specs=pl.BlockSpec((depth, d, LANES), lambda j: (0, 0, j)),
        out_shape=jax.ShapeDtypeStruct((depth, d, cp), BF16),
        compiler_params=_cparams(("parallel",), 32), name="wprep")(w2)


def _pack_weights(norm_pre, norm_post, w_rm, gla_w_up, gla_b_gate, gla_norm, pool_w, pool_scale, fox_b_f, w_out):
    depth = w_rm.shape[0]
    idx = [sum(SPLIT_SIZES[:i + 1]) for i in range(len(SPLIT_SIZES) - 1)]
    gq, gk, gv, glr, gg, pv, pg, fq, fk, fv, ff, fg = jnp.split(w_rm[:, :, :sum(SPLIT_SIZES)], idx, axis=2)
    zc = lambda n: jnp.zeros((depth, D_MODEL, n), BF16)
    aux = jnp.concatenate([ff, zc(AUX_LR - N_HEADS), glr, zc(LANES - AUX_LR - GLA_LR)], axis=2)
    w = jnp.concatenate([gq, zc(COL_K - QK_W), gk, zc(COL_GV - COL_K - QK_W), gv, fq, gg, pg, fg, pv, fk, fv, aux], axis=2)
    w_up = jnp.zeros((depth, LANES, QK_W), F32).at[:, AUX_LR:AUX_LR + GLA_LR].set(gla_w_up)
    bf = jnp.zeros((depth, 1, LANES), F32).at[:, 0, AUX_FF:AUX_FF + N_HEADS].set(fox_b_f)
    pw = jnp.zeros((depth, POOL_W, POOL_W), F32)
    for gi in range(len(POOL_WINDOWS)):
        sl = slice(gi * POOL_GROUP, (gi + 1) * POOL_GROUP)
        pw = pw.at[:, sl, sl].set(pool_w[:, gi])
    return {"pre": norm_pre[:, None], "post": norm_post[:, None], "w_in": w, "w_up": w_up.astype(BF16),
            "b_gate": gla_b_gate[:, None], "bf": bf, "gla_norm": jnp.tile(gla_norm, (1, N_HEADS))[:, None],
            "pool_w": pw.astype(BF16), "pool_scale": pool_scale[:, None], "w_out": w_out.astype(BF16)}


def _constants():
    i = np.arange(HEADS_W)
    gmat = np.where((i[:, None] // HEAD_DIM) == (i[None, :] // HEAD_DIM), 1.0 / HEAD_DIM, 0.0)
    r = np.arange(PAGE)
    tri = r[:, None] >= r[None, :]
    a = np.arange(LANES)
    place = np.stack([(a[:, None] < N_HEADS) & (a[None, :] == a[:, None] + off) for off in (0, 6, 12, 18, 24, 30)])
    sw = np.concatenate([r[:, None] > r[None, :], np.ones((PAGE, PAGE), bool)], axis=1)
    bf = lambda x: jnp.asarray(x.astype(np.float32), dtype=BF16)
    return {"gmat": bf(gmat), "tri": bf(tri), "place": bf(place), "sw": bf(sw)}


def _state_from_bd(st):
    B = st.shape[0]
    s = st.reshape(B, N_HEADS, HEAD_DIM, N_HEADS, GLA_DK)
    s = jnp.stack([s[:, h, :, h, :] for h in range(N_HEADS)], axis=1)
    return jnp.swapaxes(s, 2, 3)


def _sequence_layer(h, lw, consts, prefix, tm, tq, chunk, layer=0, depth=1, kv_t=None):
    n_pre = 0 if prefix is None else prefix["k"].shape[0]
    x = _seqmix(h, lw, consts, prefix, tm, chunk, layer, depth, kv_t)
    h_new = _fox_out(x, h, x["og"], x["op"], lw, consts, tq, n_pre)
    return h_new, x


def _sample_layer(l, h, lw, consts, page_table, rt5, cache_kt, cache_vt, state2, hist16, nb, ts, tm, nseq):
    past = page_table.shape[1] * PAGE
    x = _inproj(h, lw, tm)
    flat = {n: x[n][0] for n in ("q", "k", "g", "pv")}
    flat["v"] = x["v"][0].astype(F32)
    og, op, s_new = _sample_mix(flat, state2, hist16, lw, consts, nseq, ts, past)
    per_seq = lambda a, w: a.reshape(nb, ts, w)
    of = _sample_fox(l, page_table, per_seq(x["fq"][0].astype(F32), HEADS_W), per_seq(x["fk"][0], HEADS_W),
                     per_seq(x["fv"][0], HEADS_W), per_seq(x["aux"][0], LANES), rt5, cache_kt, cache_vt)
    h_new = _outproj(h, og[None], op[None], of.reshape(1, nb * ts, HEADS_W), x["gates"], lw, tm)
    return h_new, x, s_new


def _largest_tile(n, cap):
    t = min(n, cap)
    while n % t:
        t -= 8
    return t


def kernel(x_prompt, x_sample, cache_fox_k, cache_fox_v, cache_fox_logf, state_gla, state_pool, page_table,
           meta_tokens, norm_pre, norm_post, w_in, gla_w_up, gla_b_gate, gla_norm, pool_w, pool_scale,
           fox_b_f, w_out):
    B, T, _ = x_prompt.shape
    nb, ts, _ = x_sample.shape
    depth, n_phys = cache_fox_k.shape[:2]
    consts = _constants()
    cache_kt = jnp.transpose(cache_fox_k, (0, 1, 3, 4, 2))
    cache_vt = jnp.transpose(cache_fox_v, (0, 1, 3, 4, 2))
    lf2 = jnp.transpose(cache_fox_logf, (0, 3, 1, 2)).reshape(depth * N_HEADS * n_phys, PAGE)
    rt5 = _lfsum(lf2, consts["sw"], _largest_tile(depth * N_HEADS * n_phys, 2048))
    state2 = state_gla.reshape(depth, nb, QK_W, HEAD_DIM)
    hist16 = jnp.pad(state_pool, ((0, 0), (0, 0), (HIST_ROWS - POOL_HIST, 0), (0, 0)))
    tm_p, tm_s = _largest_tile(T, 256), _largest_tile(nb * ts, 512)
    tq_p = 2 * tm_p if T % (2 * tm_p) == 0 else tm_p
    w_rm = _wprep(w_in)

    h_m, h_p, h_s = meta_tokens[None], x_prompt, x_sample.reshape(1, nb * ts, D_MODEL)
    out = [[] for _ in range(10)]
    L = N_META + T
    kv_t = (jnp.zeros((depth, B, HEADS_W, L), F32), jnp.zeros((depth, B, HEADS_W, L), F32))
    packed = _pack_weights(norm_pre, norm_post, w_rm, gla_w_up, gla_b_gate, gla_norm, pool_w, pool_scale, fox_b_f, w_out)
    for l in range(depth):
        lw = {name: a[l] for name, a in packed.items()}
        h_m_new, xm = _sequence_layer(h_m, lw, consts, None, N_META, N_META, N_META)
        prefix = {"k": xm["kt"][0, 0].T, "v": xm["vt"][0, 0].T, "aux": xm["aux"][0], "state": xm["state"][0], "hist": xm["hist"][0]}
        h_p, xp = _sequence_layer(h_p, lw, consts, prefix, tm_p, tq_p, GLA_CHUNK, l, depth, kv_t)
        kv_t = (xp["kt"], xp["vt"])
        h_m = h_m_new
        h_s, xs, s_new = _sample_layer(l, h_s, lw, consts, page_table, rt5, cache_kt, cache_vt,
                                       state2[l], hist16[l], nb, ts, tm_s, 8)
        out[2].append(xp["aux"][:, :, :N_HEADS])
        out[3].append(_state_from_bd(xp["state"]))
        out[4].append(xp["hist"][:, HIST_ROWS - POOL_HIST:])
        out[5].append(xs["fk"].reshape(nb, ts, N_HEADS, HEAD_DIM))
        out[6].append(xs["fv"].reshape(nb, ts, N_HEADS, HEAD_DIM))
        out[7].append(xs["aux"][0, :, :N_HEADS].reshape(nb, ts, N_HEADS))
        out[8].append(s_new.reshape(nb, N_HEADS, GLA_DK, HEAD_DIM))
        out[9].append(jnp.concatenate([state_pool[l], xs["pv"][0].reshape(nb, ts, POOL_W)], axis=1)[:, ts:])
    kv_out = tuple(jnp.transpose(a.reshape(depth, B, N_HEADS, HEAD_DIM, L), (0, 1, 4, 2, 3)) for a in kv_t)
    return (h_p, h_s.reshape(nb, ts, D_MODEL)) + kv_out + tuple(jnp.stack(o) for o in out[2:])
```

```python
import functools

import jax
import jax.numpy as jnp
import numpy as np
from jax import lax
from jax.experimental import pallas as pl
from jax.experimental.pallas import tpu as pltpu

F32 = jnp.float32
BF16 = jnp.bfloat16

D_MODEL = 1024
N_HEADS = 6
GLA_DK = 32
HEAD_DIM = 64
GLA_LR = 16
GLA_GATE_NORM = 16.0
GLA_CHUNK = 64
QK_W = N_HEADS * GLA_DK
HEADS_W = N_HEADS * HEAD_DIM
POOL_W = 256
POOL_GROUP = 64
POOL_WINDOWS = (2, 4, 8, 16)
POOL_HIST = 15
HIST_ROWS = 16
N_META = 16
PAGE = 128
EPS = 1e-6
LANES = 128
NEG = -1e30

COL_Q, COL_K, COL_GV, COL_FQ, COL_GATES = 0, 256, 512, 896, 1280
COL_PV, COL_FK, COL_FV, COL_AUX, W_IN_COLS = 2304, 2560, 2944, 3328, 3456
GATES_W = HEADS_W + POOL_W + HEADS_W
AUX_FF, AUX_LR = 0, 8
SPLIT_SIZES = (QK_W, QK_W, HEADS_W, GLA_LR, HEADS_W, POOL_W, POOL_W, HEADS_W, HEADS_W, HEADS_W, N_HEADS, HEADS_W)

AUXK_ONES = (18, 36)


def _cparams(sem, vmem_mb):
    return pltpu.CompilerParams(dimension_semantics=sem, vmem_limit_bytes=vmem_mb * 1024 * 1024)


def _const_spec(shape):
    return pl.BlockSpec(shape, lambda *_: (0,) * len(shape))


def _dot(a, b):
    return jnp.dot(a, b, preferred_element_type=F32)


def _dot_nt(a, b):
    return lax.dot_general(a, b, (((1,), (1,)), ((), ())), preferred_element_type=F32)


def _dot_tn(a, b):
    return lax.dot_general(a, b, (((0,), (0,)), ((), ())), preferred_element_type=F32)


def _log_sigmoid(x):
    return jnp.minimum(x, 0.0) - jnp.log1p(jnp.exp(-jnp.abs(x)))


def _silu(x):
    return x * (1.0 / (1.0 + jnp.exp(-x)))


def _split3(x):
    hi = x.astype(BF16)
    r = x - hi.astype(F32)
    mid = r.astype(BF16)
    lo = (r - mid.astype(F32)).astype(BF16)
    return hi, mid, lo


def _cumsum_rows(x):
    n = x.shape[0]
    row = lax.broadcasted_iota(jnp.int32, x.shape, 0)
    k = 1
    while k < n:
        x = x + jnp.where(row >= k, pltpu.roll(x, k, axis=0), 0.0)
        k *= 2
    return x


def _inproj_kernel(x_ref, pre_ref, w_ref, wup_ref, bg_ref, bf_ref,
                   q_ref, k_ref, g_ref, v_ref, gates_ref, pv_ref, fq_ref, fk_ref, fv_ref, aux_ref):
    x = x_ref[0]
    xn = x * lax.rsqrt(jnp.mean(x * x, axis=-1, keepdims=True) + EPS) * pre_ref[...]
    xb = xn.astype(BF16)

    def seg(c0, width):
        return _dot(xb, w_ref[:, c0:c0 + width])

    q_ref[0] = seg(COL_Q, QK_W) * (GLA_DK ** -0.5)
    k_ref[0] = seg(COL_K, QK_W)
    vq = seg(COL_GV, 2 * HEADS_W)
    v_ref[0] = vq[:, :HEADS_W].astype(BF16)
    fq_ref[0] = (vq[:, HEADS_W:] * (HEAD_DIM ** -0.5)).astype(BF16)
    gates_ref[0] = seg(COL_GATES, GATES_W)
    pv_ref[0] = seg(COL_PV, POOL_W)
    aux = seg(COL_AUX, LANES)
    g_ref[0] = _log_sigmoid(_dot(aux.astype(BF16), wup_ref[...]) + bg_ref[...]) * (1.0 / GLA_GATE_NORM)
    kv = seg(COL_FK, 2 * HEADS_W)
    fk_ref[0] = kv[:, :HEADS_W]
    fv_ref[0] = kv[:, HEADS_W:]
    aux_ref[0] = _log_sigmoid(aux + bf_ref[...])


def _inproj(h, lw, tm):
    B, T, _ = h.shape
    tile = lambda w: pl.BlockSpec((1, tm, w), lambda b, t: (b, t, 0))
    in_specs = [tile(D_MODEL), _const_spec((1, D_MODEL)), _const_spec((D_MODEL, W_IN_COLS)),
                _const_spec((LANES, QK_W)), _const_spec((1, QK_W)), _const_spec((1, LANES))]
    widths = (("q", QK_W, F32), ("k", QK_W, F32), ("g", QK_W, F32), ("v", HEADS_W, BF16), ("gates", GATES_W, F32),
              ("pv", POOL_W, F32), ("fq", HEADS_W, BF16), ("fk", HEADS_W, F32), ("fv", HEADS_W, F32), ("aux", LANES, F32))
    outs = pl.pallas_call(
        _inproj_kernel, grid=(B, T // tm), in_specs=in_specs,
        out_specs=[tile(w) for _, w, _ in widths],
        out_shape=[jax.ShapeDtypeStruct((B, T, w), dt) for _, w, dt in widths],
        compiler_params=_cparams(("parallel", "parallel"), 56), name="inproj")(
            h, lw["pre"], lw["w_in"], lw["w_up"], lw["b_gate"], lw["bf"])
    return dict(zip([n for n, _, _ in widths], outs))


def _gla_masks(chunk):
    n = N_HEADS * chunk
    bd_k = (lax.broadcasted_iota(jnp.int32, (n, QK_W), 0) // chunk) == (lax.broadcasted_iota(jnp.int32, (n, QK_W), 1) // GLA_DK)
    bd_v = (lax.broadcasted_iota(jnp.int32, (n, HEADS_W), 0) // chunk) == (lax.broadcasted_iota(jnp.int32, (n, HEADS_W), 1) // HEAD_DIM)
    tril = (lax.broadcasted_iota(jnp.int32, (chunk, n), 1) % chunk) <= lax.broadcasted_iota(jnp.int32, (chunk, n), 0)
    return bd_k, bd_v, tril


def _gla_intra(q, k, g, v_bf, masks, chunk):
    bd_k, bd_v, tril = masks
    bcum = _cumsum_rows(g)
    qe_bf = (q * jnp.exp(bcum)).astype(BF16)
    ke = k * jnp.exp(-bcum)
    b_end = bcum[chunk - 1:chunk, :]
    kd_bf = (k * jnp.exp(b_end - bcum)).astype(BF16)
    if chunk % 16 == 0:
        ke_rows = jnp.where(bd_k, jnp.concatenate([ke.astype(BF16)] * N_HEADS, axis=0), jnp.zeros((), BF16))
        v_rows = jnp.where(bd_v, jnp.concatenate([v_bf] * N_HEADS, axis=0), jnp.zeros((), BF16))
    else:
        ke_rows = jnp.where(bd_k, jnp.concatenate([ke] * N_HEADS, axis=0), 0.0).astype(BF16)
        v_rows = jnp.where(bd_v, jnp.concatenate([v_bf.astype(F32)] * N_HEADS, axis=0), 0.0).astype(BF16)
    a = jnp.where(tril, _dot_nt(qe_bf, ke_rows), 0.0)
    o_intra = _dot(a.astype(BF16), v_rows)
    return o_intra, qe_bf, kd_bf, jnp.exp(b_end)


def _head_norm(o, gmat_bf, gn):
    o2 = o * o
    hi = o2.astype(BF16)
    lo = (o2 - hi.astype(F32)).astype(BF16)
    ms = _dot(hi, gmat_bf) + _dot(lo, gmat_bf)
    return o * lax.rsqrt(ms + EPS) * gn


def _pool_delta(hist, pv, pos0, tm):
    assert POOL_WINDOWS == (2, 4, 8, 16)
    ext = jnp.concatenate([hist, pv], axis=0)
    lane_g = lax.broadcasted_iota(jnp.int32, (1, POOL_W), 1) // POOL_GROUP
    pos = pos0 + lax.broadcasted_iota(jnp.int32, (tm, 1), 0)
    run = ext
    sums = jnp.zeros((tm, POOL_W), F32)
    cnt = jnp.zeros((tm, POOL_W), F32)
    for gi, w in enumerate(POOL_WINDOWS):
        run = run + pltpu.roll(run, w // 2, axis=0)
        sums = jnp.where(lane_g == gi, run[HIST_ROWS:, :], sums)
        cnt = jnp.where(lane_g == gi, jnp.minimum(pos + 1, w).astype(F32), cnt)
    return sums / cnt - pv, ext[tm:tm + HIST_ROWS, :]


def _seqmix_kernel(*refs, T, tm, chunk, n_pre, aliased):
    if n_pre:
        (x_ref, pre_ref, w_ref, wup_ref, bg_ref, bf_ref, kpre_ref, vpre_ref, apre_ref, s0_ref, hist0_ref, gn_ref, gmat_ref,
         pw_ref, ps_ref) = refs[:15]
        refs = refs[15:]
    else:
        (x_ref, pre_ref, w_ref, wup_ref, bg_ref, bf_ref, s0_ref, hist0_ref, gn_ref, gmat_ref, pw_ref, ps_ref) = refs[:12]
        refs = refs[12:]
    if aliased:
        refs = refs[2:]
    (gates_ref, fq_ref, fk_ref, fv_ref, kt_ref, vt_ref, aux_ref, og_ref, op_ref, sout_ref, hout_ref,
     st_ref, ext_ref, kext_ref, vext_ref) = refs
    t = pl.program_id(1)

    @pl.when(t == 0)
    def _():
        st_ref[...] = s0_ref[...]
        ext_ref[...] = hist0_ref[...]
        if n_pre:
            fk_ref[0, 0:n_pre, :] = kpre_ref[...].astype(BF16)
            fv_ref[0, 0:n_pre, :] = vpre_ref[...].astype(BF16)
            aux_ref[0, 0:n_pre, :] = apre_ref[...]
            kext_ref[0:n_pre, :] = kpre_ref[...]
            vext_ref[0:n_pre, :] = vpre_ref[...]

    def transposed(x):
        r = x.shape[0]
        if r % LANES == 0:
            return x.T
        return jnp.concatenate([x, jnp.zeros((LANES - r, HEADS_W), F32)], axis=0).T[:, :r]

    def emit_transposed(out_ref, carry_ref, tile_rows):
        if n_pre:
            carry_ref[n_pre:n_pre + tm, :] = tile_rows
            tile_rows = carry_ref[0:tm, :]
        lanes = slice(0, tm) if T == tm else pl.ds(pl.multiple_of(t * tm, LANES), tm)
        out_ref[0, 0, :, lanes] = transposed(tile_rows)
        if n_pre:
            tail = carry_ref[tm:tm + n_pre, :]
            carry_ref[0:n_pre, :] = tail

            @pl.when(t == pl.num_programs(1) - 1)
            def _():
                out_ref[0, 0, :, T:T + n_pre] = transposed(tail)

    x = x_ref[0]
    xn = x * lax.rsqrt(jnp.mean(x * x, axis=-1, keepdims=True) + EPS) * pre_ref[...]
    xb = xn.astype(BF16)

    def seg(c0, width):
        return _dot(xb, w_ref[:, c0:c0 + width])

    gates_ref[0] = seg(COL_GATES, GATES_W)
    vq = seg(COL_GV, 2 * HEADS_W)
    fq_ref[0] = (vq[:, HEADS_W:] * (HEAD_DIM ** -0.5)).astype(BF16)
    aux = seg(COL_AUX, LANES)
    rows = pl.ds(pl.multiple_of(n_pre + t * tm, 8), tm)
    kv = seg(COL_FK, 2 * HEADS_W)
    fk, fv = kv[:, :HEADS_W], kv[:, HEADS_W:]
    fk_ref[0, rows, :] = fk.astype(BF16)
    fv_ref[0, rows, :] = fv.astype(BF16)
    emit_transposed(kt_ref, kext_ref, fk)
    emit_transposed(vt_ref, vext_ref, fv)
    aux_ref[0, rows, :] = _log_sigmoid(aux + bf_ref[...])
    q = seg(COL_Q, QK_W) * (GLA_DK ** -0.5)
    k = seg(COL_K, QK_W)
    v_bf = vq[:, :HEADS_W].astype(BF16)
    g = _log_sigmoid(_dot(aux.astype(BF16), wup_ref[...]) + bg_ref[...]) * (1.0 / GLA_GATE_NORM)

    masks = _gla_masks(chunk)
    bd = (lax.broadcasted_iota(jnp.int32, (HEADS_W, QK_W), 0) // HEAD_DIM) == (lax.broadcasted_iota(jnp.int32, (HEADS_W, QK_W), 1) // GLA_DK)
    st = st_ref[...]
    outs = []
    for c in range(tm // chunk):
        r = slice(c * chunk, (c + 1) * chunk)
        o_intra, qe_bf, kd_bf, decay = _gla_intra(q[r], k[r], g[r], v_bf[r], masks, chunk)
        outs.append(o_intra + _dot_nt(qe_bf, st.astype(BF16)))
        st = st * decay + jnp.where(bd, _dot_tn(v_bf[r], kd_bf), 0.0)
    st_ref[...] = st
    o = outs[0] if len(outs) == 1 else jnp.concatenate(outs, axis=0)
    og_ref[0] = _head_norm(o, gmat_ref[...], gn_ref[...])
    d, ext_ref[...] = _pool_delta(ext_ref[...], seg(COL_PV, POOL_W), n_pre + t * tm, tm)
    op_ref[0] = _dot(d.astype(BF16), pw_ref[...]) * ps_ref[...]

    @pl.when(t == pl.num_programs(1) - 1)
    def _():
        sout_ref[0] = st
        hout_ref[0] = ext_ref[...]


def _seqmix(h, lw, consts, prefix, tm, chunk, layer=0, depth=1, kv_t=None):
    B, T, _ = h.shape
    n_pre = 0 if prefix is None else prefix["k"].shape[0]
    L = n_pre + T
    tile = lambda w: pl.BlockSpec((1, tm, w), lambda b, t: (b, t, 0))
    whole = lambda w: pl.BlockSpec((1, L, w), lambda b, t: (b, 0, 0))
    per_seq = lambda r, w: pl.BlockSpec((1, r, w), lambda b, t: (b, 0, 0))
    in_specs = [tile(D_MODEL), _const_spec((1, D_MODEL)), _const_spec((D_MODEL, W_IN_COLS)),
                _const_spec((LANES, QK_W)), _const_spec((1, QK_W)), _const_spec((1, LANES))]
    args = [h, lw["pre"], lw["w_in"], lw["w_up"], lw["b_gate"], lw["bf"]]
    if n_pre:
        in_specs += [_const_spec((n_pre, HEADS_W)), _const_spec((n_pre, HEADS_W)), _const_spec((n_pre, LANES))]
        args += [prefix["k"], prefix["v"], prefix["aux"]]
    in_specs += [_const_spec((HEADS_W, QK_W)), _const_spec((HIST_ROWS, POOL_W)), _const_spec((1, HEADS_W)),
                 _const_spec((HEADS_W, HEADS_W)), _const_spec((POOL_W, POOL_W)), _const_spec((1, POOL_W))]
    s0 = jnp.zeros((HEADS_W, QK_W), F32) if prefix is None else prefix["state"]
    hist0 = jnp.zeros((HIST_ROWS, POOL_W), F32) if prefix is None else prefix["hist"]
    args += [s0, hist0, lw["gla_norm"], consts["gmat"], lw["pool_w"], lw["pool_scale"]]
    aliases = {}
    if kv_t is not None:
        aliases = {len(args): 4, len(args) + 1: 5}
        in_specs += [pl.BlockSpec(memory_space=pl.ANY)] * 2
        args += list(kv_t)
    slab = pl.BlockSpec((1, 1, HEADS_W, L), lambda b, t: (layer, b, 0, 0))
    out_shape = [jax.ShapeDtypeStruct((B, T, GATES_W), F32), jax.ShapeDtypeStruct((B, T, HEADS_W), BF16),
                 jax.ShapeDtypeStruct((B, L, HEADS_W), BF16), jax.ShapeDtypeStruct((B, L, HEADS_W), BF16),
                 jax.ShapeDtypeStruct((depth, B, HEADS_W, L), F32), jax.ShapeDtypeStruct((depth, B, HEADS_W, L), F32),
                 jax.ShapeDtypeStruct((B, L, LANES), F32), jax.ShapeDtypeStruct((B, T, HEADS_W), F32),
                 jax.ShapeDtypeStruct((B, T, POOL_W), F32), jax.ShapeDtypeStruct((B, HEADS_W, QK_W), F32),
                 jax.ShapeDtypeStruct((B, HIST_ROWS, POOL_W), F32)]
    out_specs = [tile(GATES_W), tile(HEADS_W), whole(HEADS_W), whole(HEADS_W), slab, slab, whole(LANES), tile(HEADS_W),
                 tile(POOL_W), per_seq(HEADS_W, QK_W), per_seq(HIST_ROWS, POOL_W)]
    names = ("gates", "fq", "fk", "fv", "kt", "vt", "aux", "og", "op", "state", "hist")
    outs = pl.pallas_call(
        functools.partial(_seqmix_kernel, T=T, tm=tm, chunk=chunk, n_pre=n_pre, aliased=kv_t is not None),
        grid=(B, T // tm), in_specs=in_specs, out_specs=out_specs, out_shape=out_shape,
        input_output_aliases=aliases,
        scratch_shapes=[pltpu.VMEM((HEADS_W, QK_W), F32), pltpu.VMEM((HIST_ROWS, POOL_W), F32),
                        pltpu.VMEM((tm + n_pre, HEADS_W), F32), pltpu.VMEM((tm + n_pre, HEADS_W), F32)],
        compiler_params=_cparams(("parallel", "arbitrary"), 56), name="seqmix")(*args)
    return dict(zip(names, outs))


def _cumsum_blocks(lf, carry, tri_bf, cb):
    nblk = lf.shape[0] // cb
    x = lf if nblk == 1 else jnp.concatenate([lf[i * cb:(i + 1) * cb, :] for i in range(nblk)], axis=1)
    tri = tri_bf[0:cb, 0:cb]
    hi, mid, lo = _split3(x)
    c = _dot(tri, hi) + _dot(tri, mid) + _dot(tri, lo)
    out = []
    for i in range(nblk):
        blk = c[:, i * LANES:(i + 1) * LANES] + carry
        carry = blk[cb - 1:cb, :]
        out.append(blk)
    return (out[0] if nblk == 1 else jnp.concatenate(out, axis=0)), carry


def _fox_features(f, place_ref):
    lane = lax.broadcasted_iota(jnp.int32, (1, LANES), 1)
    fh, fm, fl = _split3(f)
    ones_k = jnp.where((lane >= AUXK_ONES[0]) & (lane < AUXK_ONES[1]), 1.0, 0.0)
    ones_q = jnp.where(lane < AUXK_ONES[0], 1.0, 0.0)
    kaux = ones_k - (_dot(fh, place_ref[0]) + _dot(fm, place_ref[1]) + _dot(fl, place_ref[2]))
    qaux = ones_q + (_dot(fh, place_ref[3]) + _dot(fm, place_ref[4]) + _dot(fl, place_ref[5]))
    return kaux.astype(BF16), qaux.astype(BF16)


def _fox_kernel(fq_ref, k_ref, v_ref, aux_ref, tri_ref, place_ref, h_ref, og_ref, op_ref, gates_ref, wo_ref, post_ref,
                out_ref, kaux_ref, qaux_ref, q2_ref, m_ref, acc_ref, sa_ref, sb_ref, *, T, n_pre, tq, tk):
    t = pl.program_id(1)
    r = tq // tk
    assert r * tk == tq and (r % 2 == 0 or T == tq)

    @pl.when(t == 0)
    def _():
        carry = jnp.zeros((1, LANES), F32)
        for r0, n in ([(0, n_pre)] if n_pre else []) + [(n_pre, T)]:
            f, carry = _cumsum_blocks(aux_ref[0, r0:r0 + n, :], carry, tri_ref[...], min(PAGE, n))
            kaux_ref[r0:r0 + n, :], qaux_ref[r0:r0 + n, :] = _fox_features(f, place_ref)

    lane = lax.broadcasted_iota(jnp.int32, (1, LANES), 1)
    q0 = pl.multiple_of(n_pre + t * tq, 16)
    qa = qaux_ref[pl.ds(q0, tq), :]
    fq = fq_ref[0]
    zero_bf = jnp.zeros((tq, LANES), BF16)
    for h in range(N_HEADS):
        fq_p = fq[:, (h // 2) * LANES:(h // 2 + 1) * LANES]
        qmask = (lane == h) | (lane == 6 + h) | (lane == 12 + h) | (lane == 18 + h) | (lane == 24 + h) | (lane == 30 + h)
        q2_ref[h] = jnp.concatenate([jnp.where((lane // HEAD_DIM) == (h % 2), fq_p, zero_bf), jnp.where(qmask, qa, zero_bf)], axis=1)
    m_ref[...] = jnp.full(m_ref.shape, NEG, F32)
    acc_ref[...] = jnp.zeros(acc_ref.shape, F32)

    def logits(j, n, s_out, row0=0):
        r0 = 0 if j is None else pl.multiple_of(n_pre + j * tk, 16)
        kaux = kaux_ref[pl.ds(r0, n), :]
        for p in range(N_HEADS // 2):
            k2 = jnp.concatenate([k_ref[0, pl.ds(r0, n), p * LANES:(p + 1) * LANES], kaux], axis=1)
            for h in (2 * p, 2 * p + 1):
                if row0:
                    s_out[h, row0:tq, :] = _dot_nt(q2_ref[h, row0:tq, :], k2)
                else:
                    s_out[h] = _dot_nt(q2_ref[h], k2)

    def update(j, n, s_in, diag):
        r0 = 0 if j is None else pl.multiple_of(n_pre + j * tk, 16)
        ones = jnp.ones((n, LANES), BF16)
        causal = diag is not None
        rows = slice((diag or 0) * tk, tq)
        nr = tq - rows.start
        if causal:
            keep = lax.broadcasted_iota(jnp.int32, (nr, n), 0) >= lax.broadcasted_iota(jnp.int32, (nr, n), 1)
        for p in range(N_HEADS // 2):
            vp = v_ref[0, pl.ds(r0, n), p * LANES:(p + 1) * LANES]
            for h in (2 * p, 2 * p + 1):
                v2 = jnp.where((lane // HEAD_DIM) == (h % 2), vp, ones)
                def read(z):
                    s = s_in[h] if j is None else s_in[h + jnp.minimum(z, 0), rows, :]
                    return jnp.where(keep, s, NEG) if causal else s
                m_old = m_ref[h, rows, :]
                m_new = jnp.maximum(m_old, jnp.max(read(t), axis=-1, keepdims=True))
                alpha = jnp.exp(m_old - m_new)
                m_b = m_new[:, :n] if n < LANES else jnp.concatenate([m_new] * (n // LANES), axis=1)
                pe = jnp.exp(read(pl.program_id(0)) - m_b).astype(BF16)
                acc_ref[h, rows, :] = alpha * acc_ref[h, rows, :] + _dot(pe, v2)
                m_ref[h, rows, :] = m_new

    if n_pre:
        pre = {}
        logits(None, n_pre, pre)
        update(None, n_pre, pre, None)

    logits(0, tk, sa_ref)

    def body(jj, c):
        j = 2 * jj
        logits(j + 1, tk, sb_ref)
        update(j, tk, sa_ref, None)
        logits(j + 2, tk, sa_ref)
        update(j + 1, tk, sb_ref, None)
        return c

    lax.fori_loop(0, (r * t) // 2, body, 0)
    bufs = (sa_ref, sb_ref)
    for i in range(r):
        if i + 1 < r:
            logits(r * t + i + 1, tk, bufs[(i + 1) % 2], row0=(i + 1) * tk)
        update(r * t + i, tk, bufs[i % 2], i)

    o_fox = []
    for p in range(N_HEADS // 2):
        a0, a1 = acc_ref[2 * p], acc_ref[2 * p + 1]
        o_fox.append(jnp.where((lane // HEAD_DIM) == 0, a0 / pltpu.roll(a0, HEAD_DIM, axis=1),
                               a1 / pltpu.roll(a1, HEAD_DIM, axis=1)))
    y_in = jnp.concatenate([og_ref[0], op_ref[0]] + o_fox, axis=1) * _silu(gates_ref[0])
    y = _dot(y_in.astype(BF16), wo_ref[...])
    out_ref[0] = h_ref[0] + y * lax.rsqrt(jnp.mean(y * y, axis=-1, keepdims=True) + EPS) * post_ref[...]


def _fox_out(x, h, og, op, lw, consts, tq, n_pre):
    B, T, _ = x["fq"].shape
    L = n_pre + T
    whole = lambda w: pl.BlockSpec((1, L, w), lambda b, t: (b, 0, 0))
    tile = lambda w: pl.BlockSpec((1, tq, w), lambda b, t: (b, t, 0))
    in_specs = [tile(HEADS_W), whole(HEADS_W), whole(HEADS_W), whole(LANES),
                _const_spec((PAGE, PAGE)), _const_spec((6, LANES, LANES)),
                tile(D_MODEL), tile(HEADS_W), tile(POOL_W), tile(GATES_W), _const_spec((GATES_W, D_MODEL)), _const_spec((1, D_MODEL))]
    tk = min(tq, 256)
    return pl.pallas_call(
        functools.partial(_fox_kernel, T=T, n_pre=n_pre, tq=tq, tk=tk),
        grid=(B, T // tq), in_specs=in_specs,
        out_specs=tile(D_MODEL),
        out_shape=jax.ShapeDtypeStruct((B, T, D_MODEL), F32),
        scratch_shapes=[pltpu.VMEM((L, LANES), BF16), pltpu.VMEM((L, LANES), BF16),
                        pltpu.VMEM((N_HEADS, tq, 2 * LANES), BF16), pltpu.VMEM((N_HEADS, tq, LANES), F32),
                        pltpu.VMEM((N_HEADS, tq, LANES), F32),
                        pltpu.VMEM((N_HEADS, tq, tk), F32), pltpu.VMEM((N_HEADS, tq, tk), F32)],
        compiler_params=_cparams(("parallel", "arbitrary"), 56), name="fox")(
            x["fq"], x["fk"], x["fv"], x["aux"], consts["tri"], consts["place"],
            h, og, op, x["gates"], lw["w_out"], lw["post"])


def _outproj_kernel(h_ref, og_ref, op_ref, of_ref, gates_ref, wo_ref, post_ref, out_ref):
    y_in = jnp.concatenate([og_ref[0], op_ref[0], of_ref[0]], axis=1) * _silu(gates_ref[0])
    y = _dot(y_in.astype(BF16), wo_ref[...])
    out_ref[0] = h_ref[0] + y * lax.rsqrt(jnp.mean(y * y, axis=-1, keepdims=True) + EPS) * post_ref[...]


def _outproj(h, og, op, of, gates, lw, tm):
    B, T, _ = h.shape
    tile = lambda w: pl.BlockSpec((1, tm, w), lambda b, t: (b, t, 0))
    return pl.pallas_call(
        _outproj_kernel, grid=(B, T // tm),
        in_specs=[tile(D_MODEL), tile(HEADS_W), tile(POOL_W), tile(HEADS_W), tile(GATES_W),
                  _const_spec((GATES_W, D_MODEL)), _const_spec((1, D_MODEL))],
        out_specs=tile(D_MODEL), out_shape=jax.ShapeDtypeStruct((B, T, D_MODEL), F32),
        compiler_params=_cparams(("parallel", "parallel"), 48), name="outproj")(
            h, og, op, of, gates, lw["w_out"], lw["post"])


def _sample_mix_kernel(q_ref, k_ref, g_ref, v_ref, pv_ref, s_ref, hist_ref, gn_ref, gmat_ref, pw_ref, ps_ref,
                       og_ref, op_ref, sout_ref, *, nseq, ts, pos0):
    masks = _gla_masks(ts)
    rr = lax.broadcasted_iota(jnp.int32, (QK_W, HEADS_W), 0) // GLA_DK
    cc = lax.broadcasted_iota(jnp.int32, (QK_W, HEADS_W), 1) // HEAD_DIM
    bd = rr == cc
    ones = jnp.ones((3 * ts, LANES), BF16)
    outs, deltas = [], []
    for i in range(nseq):
        r = slice(i * ts, (i + 1) * ts)
        g = g_ref[r, :]
        v_bf = v_ref[r, :].astype(BF16)
        o_intra, qe_bf, kd_bf, _ = _gla_intra(q_ref[r, :], k_ref[r, :], g, v_bf, masks, ts)
        s1 = s_ref[i]
        s2 = jnp.concatenate([s1, s1], axis=1)
        s_bd = jnp.where(bd, jnp.concatenate([s2, s2, s2], axis=1), 0.0).astype(BF16)
        outs.append(o_intra + _dot(qe_bf, s_bd))
        u = jnp.where(bd, _dot_tn(kd_bf, v_bf), 0.0)
        x = jnp.concatenate([u[64 * j:64 * (j + 1), LANES * j:LANES * (j + 1)] for j in range(N_HEADS // 2)], axis=0)
        x = x + pltpu.roll(x, HEAD_DIM, axis=1)
        gh, gm, gl = _split3(g)
        gparts = jnp.concatenate([gh.astype(F32), gm.astype(F32), gl.astype(F32)], axis=0).astype(BF16)
        sout_ref[i] = (s2 * jnp.exp(_dot_tn(gparts, ones)) + x)[:, :HEAD_DIM]
        deltas.append(_pool_delta(hist_ref[i], pv_ref[r, :], pos0, ts)[0])
    og_ref[...] = _head_norm(jnp.concatenate(outs, axis=0), gmat_ref[...], gn_ref[...])
    op_ref[...] = _dot(jnp.concatenate(deltas, axis=0).astype(BF16), pw_ref[...]) * ps_ref[...]


def _sample_mix(x, state2, hist16, lw, consts, nseq, ts, pos0):
    n = x["q"].shape[0]
    rows = nseq * ts
    tile = lambda w: pl.BlockSpec((rows, w), lambda i: (i, 0))
    in_specs = [tile(QK_W), tile(QK_W), tile(QK_W), tile(HEADS_W), tile(POOL_W),
                pl.BlockSpec((nseq, QK_W, HEAD_DIM), lambda i: (i, 0, 0)), pl.BlockSpec((nseq, HIST_ROWS, POOL_W), lambda i: (i, 0, 0)),
                _const_spec((1, HEADS_W)), _const_spec((HEADS_W, HEADS_W)), _const_spec((POOL_W, POOL_W)), _const_spec((1, POOL_W))]
    out_shape = [jax.ShapeDtypeStruct((n, HEADS_W), F32), jax.ShapeDtypeStruct((n, POOL_W), F32),
                 jax.ShapeDtypeStruct(state2.shape, F32)]
    out_specs = [tile(HEADS_W), tile(POOL_W), pl.BlockSpec((nseq, QK_W, HEAD_DIM), lambda i: (i, 0, 0))]
    return pl.pallas_call(
        functools.partial(_sample_mix_kernel, nseq=nseq, ts=ts, pos0=pos0),
        grid=(n // rows,), in_specs=in_specs, out_specs=out_specs, out_shape=out_shape,
        compiler_params=_cparams(("parallel",), 48), name="sample_mix")(
            x["q"], x["k"], x["g"], x["v"], x["pv"], state2, hist16, lw["gla_norm"], consts["gmat"], lw["pool_w"], lw["pool_scale"])


def _suffix_rows(x, period):
    n = x.shape[0]
    row = lax.broadcasted_iota(jnp.int32, x.shape, 0) % period
    k = 1
    while k < period:
        x = x + jnp.where(row + k < period, pltpu.roll(x, n - k, axis=0), 0.0)
        k *= 2
    return x


def _lfsum_kernel(lf_ref, sw_ref, o_ref):
    n = lf_ref.shape[0]
    hi, mid, lo = _split3(lf_ref[...])
    parts = jnp.concatenate([hi.astype(F32), mid.astype(F32), lo.astype(F32)], axis=0).astype(BF16)
    r3 = _dot(parts, sw_ref[...])
    o_ref[...] = r3[0:n] + r3[n:2 * n] + r3[2 * n:3 * n]


def _lfsum(lf2, sw, tr):
    n = lf2.shape[0]
    return pl.pallas_call(
        _lfsum_kernel, grid=(n // tr,),
        in_specs=[pl.BlockSpec((tr, PAGE), lambda i: (i, 0)), _const_spec((PAGE, 2 * PAGE))],
        out_specs=pl.BlockSpec((tr, 2 * PAGE), lambda i: (i, 0)), out_shape=jax.ShapeDtypeStruct((n, 2 * PAGE), F32),
        compiler_params=_cparams(("parallel",), 32), name="lfsum")(lf2, sw)


def _sample_fox_kernel(pt_ref, fq_ref, kn_ref, vn_ref, an_ref, rt_ref, ck_ref, cv_ref, o_ref,
                       kbuf, vbuf, ksem, vsem, *, n_pages, ts, n_phys, layer):
    b = pl.program_id(0)
    nb = pl.num_programs(0)
    slot = b % 2
    nxt = jnp.minimum(b + 1, nb - 1)

    def k_copy(seq, s, p):
        return pltpu.make_async_copy(ck_ref.at[layer, pt_ref[seq, p]], kbuf.at[s, p], ksem.at[s])

    def v_copy(seq, s, p):
        return pltpu.make_async_copy(cv_ref.at[layer, pt_ref[seq, p]], vbuf.at[s, p], vsem.at[s])

    @pl.when(b == 0)
    def _():
        for p in range(n_pages):
            k_copy(0, 0, p).start()
            v_copy(0, 0, p).start()

    for p in range(n_pages):
        k_copy(b, slot, p).wait()
    r2 = jnp.concatenate([rt_ref[pl.ds(h * n_phys + pt_ref[b, p], 1), :]
                          for h in range(N_HEADS) for p in range(n_pages)], axis=0)
    tot = r2[:, PAGE:]
    rfull = r2[:, :PAGE] + (_suffix_rows(tot, n_pages) - tot)
    cn = _cumsum_rows(an_ref[0])
    cn2 = jnp.concatenate([cn, cn], axis=0)
    cparts = _split3(cn)
    lane = lax.broadcasted_iota(jnp.int32, (2 * ts, LANES), 1)
    first = lax.broadcasted_iota(jnp.int32, (2 * ts, LANES), 0) < ts
    own = (lane < HEAD_DIM) == first
    causal = (lax.broadcasted_iota(jnp.int32, (2 * ts, ts), 0) % ts) >= lax.broadcasted_iota(jnp.int32, (2 * ts, ts), 1)
    for pr in range(N_HEADS // 2):
        cols = slice(pr * LANES, (pr + 1) * LANES)
        fq2 = jnp.concatenate([fq_ref[0, :, cols], fq_ref[0, :, cols]], axis=0)
        q2 = jnp.where(own, fq2, 0.0).astype(BF16)
        onehot = lane == jnp.where(first, 2 * pr, 2 * pr + 1)
        cq = jnp.sum(jnp.where(onehot, cn2, 0.0), axis=-1, keepdims=True)
        sel = jnp.where(onehot, 1.0, 0.0).astype(BF16)
        cn_t = _dot_nt(sel, cparts[0]) + _dot_nt(sel, cparts[1]) + _dot_nt(sel, cparts[2])
        s_new = jnp.where(causal, _dot_nt(q2, kn_ref[0, :, cols].astype(BF16)) + cq - cn_t, NEG)
        s_pages = []
        for p in range(n_pages):
            if pr == 0:
                k_copy(nxt, 1 - slot, p).start()
            kt = kbuf[slot, p, 2 * pr:2 * pr + 2].reshape(LANES, PAGE).astype(BF16)
            r0, r1 = 2 * pr * n_pages + p, (2 * pr + 1) * n_pages + p
            bias = jnp.concatenate([jnp.broadcast_to(rfull[r0:r0 + 1, :], (ts, PAGE)),
                                    jnp.broadcast_to(rfull[r1:r1 + 1, :], (ts, PAGE))], axis=0)
            s_pages.append(_dot(q2, kt) + bias + cq)
        mm = s_pages[0]
        for s in s_pages[1:]:
            mm = jnp.maximum(mm, s)
        m = jnp.maximum(jnp.max(mm, axis=-1, keepdims=True), jnp.max(s_new, axis=-1, keepdims=True))
        p_new = jnp.exp(s_new - m)
        l = jnp.sum(p_new, axis=-1, keepdims=True)
        acc = _dot(p_new.astype(BF16), vn_ref[0, :, cols].astype(BF16))
        if pr == 0:
            for p in range(n_pages):
                v_copy(b, slot, p).wait()
        for p in range(n_pages):
            if pr == 0:
                v_copy(nxt, 1 - slot, p).start()
            pe = jnp.exp(s_pages[p] - m)
            l = l + jnp.sum(pe, axis=-1, keepdims=True)
            vt = vbuf[slot, p, 2 * pr:2 * pr + 2].reshape(LANES, PAGE).astype(BF16)
            acc = acc + _dot_nt(pe.astype(BF16), vt)
        o2 = acc / l
        o_ref[0, :, cols] = jnp.where(lane[:ts] < HEAD_DIM, o2[:ts], o2[ts:])

    @pl.when(b == nb - 1)
    def _():
        for p in range(n_pages):
            k_copy(nxt, 1 - slot, p).wait()
            v_copy(nxt, 1 - slot, p).wait()


def _sample_fox(layer, page_table, fq, kn, vn, an, rt2, cache_kt, cache_vt):
    nb, ts, _ = fq.shape
    n_pages = page_table.shape[1]
    n_phys = cache_kt.shape[1]
    seq = lambda w: pl.BlockSpec((1, ts, w), lambda b, pt: (b, 0, 0))
    rt_spec = pl.BlockSpec((N_HEADS * n_phys, 2 * PAGE), lambda b, pt: (layer, 0), pipeline_mode=pl.Buffered(1))
    hbm = pl.BlockSpec(memory_space=pl.ANY)
    page_buf = pltpu.VMEM((2, n_pages, N_HEADS, HEAD_DIM, PAGE), F32)
    grid_spec = pltpu.PrefetchScalarGridSpec(
        num_scalar_prefetch=1, grid=(nb,),
        in_specs=[seq(HEADS_W), seq(HEADS_W), seq(HEADS_W), seq(LANES), rt_spec, hbm, hbm],
        out_specs=pl.BlockSpec((1, ts, HEADS_W), lambda b, pt: (b, 0, 0)),
        scratch_shapes=[page_buf, page_buf, pltpu.SemaphoreType.DMA((2,)), pltpu.SemaphoreType.DMA((2,))])
    return pl.pallas_call(
        functools.partial(_sample_fox_kernel, n_pages=n_pages, ts=ts, n_phys=n_phys, layer=layer), grid_spec=grid_spec,
        out_shape=jax.ShapeDtypeStruct((nb, ts, HEADS_W), F32),
        compiler_params=_cparams(("arbitrary",), 56), name="sample_fox")(
            page_table, fq, kn, vn, an, rt2, cache_kt, cache_vt)


def _wprep_kernel(w_ref, o_ref, kz_ref, vz_ref, *, depth, n_cols):
    kz_ref[...] = jnp.zeros(kz_ref.shape, F32)
    vz_ref[...] = jnp.zeros(vz_ref.shape, F32)
    per_col = (D_MODEL // LANES) * depth
    valid = pl.program_id(0) * LANES + lax.broadcasted_iota(jnp.int32, (LANES, LANES), 0) < n_cols
    for l in range(depth):
        for rt in range(D_MODEL // LANES):
            blk = w_ref[pl.ds(rt * depth + l, LANES, stride=per_col), :]
            o_ref[l, rt * LANES:(rt + 1) * LANES, :] = jnp.where(valid, blk, 0.0).T.astype(BF16)


def _wprep(w_in, B, L):
    depth, d, c = w_in.shape
    per_col = (d // LANES) * depth
    cp = -(-c // LANES) * LANES
    steps, slabs = cp // LANES, depth * B
    assert steps >= slabs
    w2 = jnp.transpose(w_in, (2, 0, 1)).reshape(c, depth, d // LANES, LANES)
    w2 = jnp.transpose(w2, (0, 2, 1, 3)).reshape(c * per_col, LANES)
    slab = pl.BlockSpec((1, 1, HEADS_W, L), lambda j: (jnp.minimum(j, slabs - 1) // B, jnp.minimum(j, slabs - 1) % B, 0, 0))
    kv = jax.ShapeDtypeStruct((depth, B, HEADS_W, L), F32)
    w_rm, kz, vz = pl.pallas_call(
        functools.partial(_wprep_kernel, depth=depth, n_cols=c), grid=(steps,),
        in_specs=[pl.BlockSpec((LANES * per_col, LANES), lambda j: (j, 0))],
        out_specs=[pl.BlockSpec((depth, d, LANES), lambda j: (0, 0, j)), slab, slab],
        out_shape=[jax.ShapeDtypeStruct((depth, d, cp), BF16), kv, kv],
        compiler_params=_cparams(("arbitrary",), 40), name="wprep")(w2)
    return w_rm, (kz, vz)


def _pack_weights(norm_pre, norm_post, w_rm, gla_w_up, gla_b_gate, gla_norm, pool_w, pool_scale, fox_b_f, w_out):
    depth = w_rm.shape[0]
    idx = [sum(SPLIT_SIZES[:i + 1]) for i in range(len(SPLIT_SIZES) - 1)]
    gq, gk, gv, glr, gg, pv, pg, fq, fk, fv, ff, fg = jnp.split(w_rm[:, :, :sum(SPLIT_SIZES)], idx, axis=2)
    zc = lambda n: jnp.zeros((depth, D_MODEL, n), BF16)
    aux = jnp.concatenate([ff, zc(AUX_LR - N_HEADS), glr, zc(LANES - AUX_LR - GLA_LR)], axis=2)
    w = jnp.concatenate([gq, zc(COL_K - QK_W), gk, zc(COL_GV - COL_K - QK_W), gv, fq, gg, pg, fg, pv, fk, fv, aux], axis=2)
    w_up = jnp.zeros((depth, LANES, QK_W), F32).at[:, AUX_LR:AUX_LR + GLA_LR].set(gla_w_up)
    bf = jnp.zeros((depth, 1, LANES), F32).at[:, 0, AUX_FF:AUX_FF + N_HEADS].set(fox_b_f)
    pw = jnp.zeros((depth, POOL_W, POOL_W), F32)
    for gi in range(len(POOL_WINDOWS)):
        sl = slice(gi * POOL_GROUP, (gi + 1) * POOL_GROUP)
        pw = pw.at[:, sl, sl].set(pool_w[:, gi])
    return {"pre": norm_pre[:, None], "post": norm_post[:, None], "w_in": w, "w_up": w_up.astype(BF16),
            "b_gate": gla_b_gate[:, None], "bf": bf, "gla_norm": jnp.tile(gla_norm, (1, N_HEADS))[:, None],
            "pool_w": pw.astype(BF16), "pool_scale": pool_scale[:, None], "w_out": w_out.astype(BF16)}


def _constants():
    i = np.arange(HEADS_W)
    gmat = np.where((i[:, None] // HEAD_DIM) == (i[None, :] // HEAD_DIM), 1.0 / HEAD_DIM, 0.0)
    r = np.arange(PAGE)
    tri = r[:, None] >= r[None, :]
    a = np.arange(LANES)
    place = np.stack([(a[:, None] < N_HEADS) & (a[None, :] == a[:, None] + off) for off in (0, 6, 12, 18, 24, 30)])
    sw = np.concatenate([r[:, None] > r[None, :], np.ones((PAGE, PAGE), bool)], axis=1)
    bf = lambda x: jnp.asarray(x.astype(np.float32), dtype=BF16)
    return {"gmat": bf(gmat), "tri": bf(tri), "place": bf(place), "sw": bf(sw)}


def _state_from_bd(st):
    B = st.shape[0]
    s = st.reshape(B, N_HEADS, HEAD_DIM, N_HEADS, GLA_DK)
    s = jnp.stack([s[:, h, :, h, :] for h in range(N_HEADS)], axis=1)
    return jnp.swapaxes(s, 2, 3)


def _sequence_layer(h, lw, consts, prefix, tm, tq, chunk, layer=0, depth=1, kv_t=None):
    n_pre = 0 if prefix is None else prefix["k"].shape[0]
    x = _seqmix(h, lw, consts, prefix, tm, chunk, layer, depth, kv_t)
    h_new = _fox_out(x, h, x["og"], x["op"], lw, consts, tq, n_pre)
    return h_new, x


def _sample_layer(l, h, lw, consts, page_table, rt5, cache_kt, cache_vt, state2, hist16, nb, ts, tm, nseq):
    past = page_table.shape[1] * PAGE
    x = _inproj(h, lw, tm)
    flat = {n: x[n][0] for n in ("q", "k", "g", "pv")}
    flat["v"] = x["v"][0].astype(F32)
    og, op, s_new = _sample_mix(flat, state2, hist16, lw, consts, nseq, ts, past)
    per_seq = lambda a, w: a.reshape(nb, ts, w)
    of = _sample_fox(l, page_table, per_seq(x["fq"][0].astype(F32), HEADS_W), per_seq(x["fk"][0], HEADS_W),
                     per_seq(x["fv"][0], HEADS_W), per_seq(x["aux"][0], LANES), rt5, cache_kt, cache_vt)
    h_new = _outproj(h, og[None], op[None], of.reshape(1, nb * ts, HEADS_W), x["gates"], lw, tm)
    return h_new, x, s_new


def _largest_tile(n, cap):
    t = min(n, cap)
    while n % t:
        t -= 8
    return t


def kernel(x_prompt, x_sample, cache_fox_k, cache_fox_v, cache_fox_logf, state_gla, state_pool, page_table,
           meta_tokens, norm_pre, norm_post, w_in, gla_w_up, gla_b_gate, gla_norm, pool_w, pool_scale,
           fox_b_f, w_out):
    B, T, _ = x_prompt.shape
    nb, ts, _ = x_sample.shape
    depth, n_phys = cache_fox_k.shape[:2]
    consts = _constants()
    cache_kt = jnp.transpose(cache_fox_k, (0, 1, 3, 4, 2))
    cache_vt = jnp.transpose(cache_fox_v, (0, 1, 3, 4, 2))
    lf2 = jnp.transpose(cache_fox_logf, (0, 3, 1, 2)).reshape(depth * N_HEADS * n_phys, PAGE)
    rt5 = _lfsum(lf2, consts["sw"], _largest_tile(depth * N_HEADS * n_phys, 2048))
    state2 = state_gla.reshape(depth, nb, QK_W, HEAD_DIM)
    hist16 = jnp.pad(state_pool, ((0, 0), (0, 0), (HIST_ROWS - POOL_HIST, 0), (0, 0)))
    tm_p, tm_s = _largest_tile(T, 256), _largest_tile(nb * ts, 512)
    tq_p = 2 * tm_p if T % (2 * tm_p) == 0 else tm_p
    L = N_META + T
    w_rm, kv_t = _wprep(w_in, B, L)

    h_m, h_p, h_s = meta_tokens[None], x_prompt, x_sample.reshape(1, nb * ts, D_MODEL)
    out = [[] for _ in range(10)]
    packed = _pack_weights(norm_pre, norm_post, w_rm, gla_w_up, gla_b_gate, gla_norm, pool_w, pool_scale, fox_b_f, w_out)
    for l in range(depth):
        lw = {name: a[l] for name, a in packed.items()}
        h_m_new, xm = _sequence_layer(h_m, lw, consts, None, N_META, N_META, N_META)
        prefix = {"k": xm["kt"][0, 0].T, "v": xm["vt"][0, 0].T, "aux": xm["aux"][0], "state": xm["state"][0], "hist": xm["hist"][0]}
        h_p, xp = _sequence_layer(h_p, lw, consts, prefix, tm_p, tq_p, GLA_CHUNK, l, depth, kv_t)
        kv_t = (xp["kt"], xp["vt"])
        h_m = h_m_new
        h_s, xs, s_new = _sample_layer(l, h_s, lw, consts, page_table, rt5, cache_kt, cache_vt,
                                       state2[l], hist16[l], nb, ts, tm_s, 8)
        out[2].append(xp["aux"][:, :, :N_HEADS])
        out[3].append(_state_from_bd(xp["state"]))
        out[4].append(xp["hist"][:, HIST_ROWS - POOL_HIST:])
        out[5].append(xs["fk"].reshape(nb, ts, N_HEADS, HEAD_DIM))
        out[6].append(xs["fv"].reshape(nb, ts, N_HEADS, HEAD_DIM))
        out[7].append(xs["aux"][0, :, :N_HEADS].reshape(nb, ts, N_HEADS))
        out[8].append(s_new.reshape(nb, N_HEADS, GLA_DK, HEAD_DIM))
        out[9].append(jnp.concatenate([state_pool[l], xs["pv"][0].reshape(nb, ts, POOL_W)], axis=1)[:, ts:])
    kv_out = tuple(jnp.transpose(a.reshape(depth, B, N_HEADS, HEAD_DIM, L), (0, 1, 4, 2, 3)) for a in kv_t)
    return (h_p, h_s.reshape(nb, ts, D_MODEL)) + kv_out + tuple(jnp.stack(o) for o in out[2:])
```
